```python
import math
import jax, jax.numpy as jnp
from jax import lax
import numpy as np

D_MODEL = 1024
BATCH = 8
SEQ = 2048
DEPTH = 4
DEC_BATCH = 128
DEC_SEQ = 8
PAST_LEN = 16384
PAGE_SIZE = 128

N_MIXERS = 2
POOL_WINDOWS = (2, 4, 8, 16)
POOL_GROUPS = len(POOL_WINDOWS)
POOL_GW = D_MODEL // POOL_GROUPS
POOL_MAXW = max(POOL_WINDOWS)
POOL_BUF = POOL_MAXW - 1
GLA_HEADS = 4
GLA_DK = D_MODEL // 2
GLA_DV = D_MODEL
GLA_DKH = GLA_DK // GLA_HEADS
GLA_DVH = GLA_DV // GLA_HEADS
GLA_GATE_RANK = 16
GLA_GATE_TEMP = 16.0
GLA_CHUNK = 32
GLA_IN = 2 * GLA_DK + 2 * GLA_DV + GLA_GATE_RANK
D_FF = ((8 * D_MODEL + 3 * 256 - 1) // (3 * 256)) * 256
N_POOL_LAYERS = (DEPTH + 1) // 2
N_GLA_LAYERS = DEPTH // 2
EPS = 1e-6

kernel_name = "pool_gla_hybrid_decoder_step"


def rmsnorm(x, g):
    xf = x.astype(jnp.float32)
    y = xf * lax.rsqrt(jnp.mean(xf * xf, axis=-1, keepdims=True) + EPS)
    return y.astype(x.dtype) * g


def swiglu(h, w1, w3, w2):
    return (jax.nn.silu(h @ w1) * (h @ w3)) @ w2


def pool_mixer(u, ctx, pos, w_pool, scale):
    B, T, D = u.shape
    ext = jnp.concatenate([jnp.zeros((B, 1, D), u.dtype), ctx.astype(u.dtype), u], axis=1)
    cs = jnp.cumsum(ext.astype(jnp.float32), axis=1)
    lag = jnp.concatenate(
        [cs[:, POOL_MAXW - w:POOL_MAXW - w + T, g * POOL_GW:(g + 1) * POOL_GW]
         for g, w in enumerate(POOL_WINDOWS)], axis=-1)
    sums = (cs[:, POOL_MAXW:] - lag).reshape(B, T, POOL_GROUPS, POOL_GW)
    wins = jnp.array(POOL_WINDOWS, dtype=jnp.int32)
    counts = jnp.minimum(pos[:, None] + 1, wins[None, :]).astype(jnp.float32)
    pooled = sums / counts[None, :, :, None]
    diff = (pooled - u.astype(jnp.float32).reshape(B, T, POOL_GROUPS, POOL_GW)).astype(u.dtype)
    y = jnp.einsum('btgc,gcd->btgd', diff, w_pool).reshape(B, T, D) * scale
    new_ctx = ext[:, -POOL_BUF:]
    return y, new_ctx


def gla_recurrence(q, k, v, log_a, S0):
    B, T, H, _ = q.shape
    C = min(GLA_CHUNK, T)
    n = -(-T // C)
    pad = n * C - T

    def prep(t):
        t = jnp.pad(t.astype(jnp.float32), ((0, 0), (0, pad), (0, 0), (0, 0)))
        return t.reshape(B, n, C, H, t.shape[-1]).transpose(1, 0, 3, 2, 4)

    qs, ks, vs, gs = prep(q), prep(k), prep(v), prep(log_a)
    mask = jnp.tril(jnp.ones((C, C), dtype=bool))[:, :, None]

    def step(S, inp):
        qc, kc, vc, gc = inp
        b = jnp.cumsum(gc, axis=2)
        diff = b[:, :, :, None, :] - b[:, :, None, :, :]
        decay = jnp.exp(jnp.where(mask, diff, -jnp.inf))
        A = jnp.einsum('bhid,bhjd,bhijd->bhij', qc, kc, decay)
        o = jnp.einsum('bhij,bhjv->bhiv', A, vc) + jnp.einsum('bhid,bhdv->bhiv', qc * jnp.exp(b), S)
        b_last = b[:, :, -1:, :]
        S_new = jnp.exp(b_last[:, :, 0, :, None]) * S + jnp.einsum('bhjd,bhjv->bhdv', kc * jnp.exp(b_last - b), vc)
        return S_new, o

    S_fin, os_ = lax.scan(step, S0.astype(jnp.float32), (qs, ks, vs, gs))
    o = os_.transpose(1, 0, 3, 2, 4).reshape(B, n * C, H, -1)[:, :T]
    return o, S_fin


def gla_mixer(u, S0, w_in, w_a2, b_a, norm_w, w_o):
    B, T, _ = u.shape
    proj = u @ w_in
    q = proj[..., :GLA_DK].reshape(B, T, GLA_HEADS, GLA_DKH) * (GLA_DKH ** -0.5)
    k = proj[..., GLA_DK:2 * GLA_DK].reshape(B, T, GLA_HEADS, GLA_DKH)
    v = proj[..., 2 * GLA_DK:2 * GLA_DK + GLA_DV].reshape(B, T, GLA_HEADS, GLA_DVH)
    r = proj[..., 2 * GLA_DK + GLA_DV:2 * GLA_DK + 2 * GLA_DV]
    a_low = proj[..., 2 * GLA_DK + 2 * GLA_DV:]
    log_a = (jax.nn.log_sigmoid((a_low @ w_a2 + b_a).astype(jnp.float32)) / GLA_GATE_TEMP)
    log_a = log_a.reshape(B, T, GLA_HEADS, GLA_DKH)
    o, S = gla_recurrence(q, k, v, log_a, S0)
    o = o * lax.rsqrt(jnp.mean(o * o, axis=-1, keepdims=True) + EPS)
    o = o.astype(u.dtype) * norm_w.reshape(GLA_HEADS, GLA_DVH)
    o = o.reshape(B, T, GLA_DV) * jax.nn.silu(r)
    return o @ w_o, S.astype(S0.dtype)


def trunk(x, pool_ctx, gla_S, pos, norm_mix, norm_ffn, norm_final, pool_w, pool_scale,
          gla_w_in, gla_w_a2, gla_b_a, gla_norm, gla_w_o, ffn_w1, ffn_w3, ffn_w2):
    new_pool, new_gla = [], []
    for i in range(DEPTH):
        h = rmsnorm(x, norm_mix[i])
        j = i // N_MIXERS
        if i % N_MIXERS == 0:
            y, c = pool_mixer(h, pool_ctx[j], pos, pool_w[j], pool_scale[j])
            new_pool.append(c)
        else:
            y, S = gla_mixer(h, gla_S[j], gla_w_in[j], gla_w_a2[j], gla_b_a[j], gla_norm[j], gla_w_o[j])
            new_gla.append(S)
        x = x + y
        x = x + swiglu(rmsnorm(x, norm_ffn[i]), ffn_w1[i], ffn_w3[i], ffn_w2[i])
    return rmsnorm(x, norm_final), jnp.stack(new_pool), jnp.stack(new_gla)


def setup_inputs(seed: int = 0) -> dict:
    key = jax.random.key(seed)
    ks = jax.random.split(key, 20)
    f32 = jnp.float32
    nrm = lambda k, shape, s: jax.random.normal(k, shape, f32) * s
    return {
        "x_prompt": nrm(ks[0], (BATCH, SEQ, D_MODEL), 1.0),
        "x_sample": nrm(ks[1], (DEC_BATCH, DEC_SEQ, D_MODEL), 1.0),
        "state_pool": nrm(ks[2], (N_POOL_LAYERS, DEC_BATCH, POOL_BUF, D_MODEL), 1.0),
        "state_gla": nrm(ks[3], (N_GLA_LAYERS, DEC_BATCH, GLA_HEADS, GLA_DKH, GLA_DVH), 0.3),
        "norm_mix": 1.0 + nrm(ks[4], (DEPTH, D_MODEL), 0.02),
        "norm_ffn": 1.0 + nrm(ks[5], (DEPTH, D_MODEL), 0.02),
        "norm_final": 1.0 + nrm(ks[6], (D_MODEL,), 0.02),
        "pool_w": nrm(ks[7], (N_POOL_LAYERS, POOL_GROUPS, POOL_GW, POOL_GW), POOL_GW ** -0.5),
        "pool_scale": 1.0 + nrm(ks[8], (N_POOL_LAYERS, D_MODEL), 0.1),
        "gla_w_in": nrm(ks[9], (N_GLA_LAYERS, D_MODEL, GLA_IN), D_MODEL ** -0.5),
        "gla_w_a2": nrm(ks[10], (N_GLA_LAYERS, GLA_GATE_RANK, GLA_DK), GLA_GATE_RANK ** -0.5),
        "gla_b_a": nrm(ks[11], (N_GLA_LAYERS, GLA_DK), 0.1),
        "gla_norm": 1.0 + nrm(ks[12], (N_GLA_LAYERS, GLA_DV), 0.02),
        "gla_w_o": nrm(ks[13], (N_GLA_LAYERS, GLA_DV, D_MODEL), GLA_DV ** -0.5),
        "ffn_w1": nrm(ks[14], (DEPTH, D_MODEL, D_FF), D_MODEL ** -0.5),
        "ffn_w3": nrm(ks[15], (DEPTH, D_MODEL, D_FF), D_MODEL ** -0.5),
        "ffn_w2": nrm(ks[16], (DEPTH, D_FF, D_MODEL), D_FF ** -0.5),
    }


def reference(x_prompt, x_sample, state_pool, state_gla, norm_mix, norm_ffn, norm_final, pool_w, pool_scale,
              gla_w_in, gla_w_a2, gla_b_a, gla_norm, gla_w_o, ffn_w1, ffn_w3, ffn_w2):
    B, T = x_prompt.shape[0], x_prompt.shape[1]
    Bs, Ts = x_sample.shape[0], x_sample.shape[1]
    pos_prompt = jnp.arange(T, dtype=jnp.int32)
    pos_sample = PAST_LEN + jnp.arange(Ts, dtype=jnp.int32)
    pool0 = jnp.zeros((N_POOL_LAYERS, B, POOL_BUF, D_MODEL), x_prompt.dtype)
    gla0 = jnp.zeros((N_GLA_LAYERS, B, GLA_HEADS, GLA_DKH, GLA_DVH), jnp.float32)
    y_prompt, new_pool_prompt, new_gla_prompt = trunk(
        x_prompt, pool0, gla0, pos_prompt, norm_mix, norm_ffn, norm_final, pool_w, pool_scale,
        gla_w_in, gla_w_a2, gla_b_a, gla_norm, gla_w_o, ffn_w1, ffn_w3, ffn_w2)
    y_sample, new_pool_sample, new_gla_sample = trunk(
        x_sample, state_pool, state_gla, pos_sample, norm_mix, norm_ffn, norm_final, pool_w, pool_scale,
        gla_w_in, gla_w_a2, gla_b_a, gla_norm, gla_w_o, ffn_w1, ffn_w3, ffn_w2)
    return (y_prompt, y_sample, new_pool_prompt, new_gla_prompt, new_pool_sample, new_gla_sample)
```

```python
import functools

import numpy as np
import jax
import jax.numpy as jnp
from jax import lax
from jax.experimental import pallas as pl
from jax.experimental.pallas import tpu as pltpu

F32 = jnp.float32
BF16 = jnp.bfloat16

PAST_LEN = 16384
POOL_WINDOWS = (2, 4, 8, 16)
POOL_MAXW = max(POOL_WINDOWS)
POOL_BUF = POOL_MAXW - 1
GLA_HEADS = 4
GLA_GATE_RANK = 16
GLA_GATE_TEMP = 16.0
EPS = 1e-6

V7X_LANES = 128
V7X_SUBLANES = 8
V7X_VMEM_BYTES = 64 * 1024 * 1024

FFN_ROWS = 256
POOL_ROWS = 256
POOL_SEQS = 16
GLA_CHUNK = 128
GLA_SEQS = 8


def _vmem_limit(resident_bytes):
    return int(min(V7X_VMEM_BYTES - (4 << 20), 2 * resident_bytes + (16 << 20)))


def _const_spec(shape):
    nd = len(shape)
    return pl.BlockSpec(shape, lambda *_: (0,) * nd, pipeline_mode=pl.Buffered(1))


def _rmsnorm(x, g):
    ms = jnp.mean(x * x, axis=-1, keepdims=True)
    return x * lax.rsqrt(ms + EPS) * g


def _silu(x):
    return x / (1.0 + jnp.exp(-x))


def _dot(a, b):
    return jnp.dot(a, b, preferred_element_type=F32)


def _dot_nt(a, b):
    return lax.dot_general(a, b, (((1,), (1,)), ((), ())), preferred_element_type=F32)


def _dot_tn(a, b):
    return lax.dot_general(a, b, (((0,), (0,)), ((), ())), preferred_element_type=F32)


def _split3(x):
    hi = x.astype(BF16)
    r1 = x - hi.astype(F32)
    mid = r1.astype(BF16)
    lo = (r1 - mid.astype(F32)).astype(BF16)
    return hi, mid, lo


def _ffn_body(x_ref, g_ref, w1_ref, w3_ref, w2_ref, gf_ref, o_ref, *, final):
    x = x_ref[...]
    h = _rmsnorm(x, g_ref[...]).astype(BF16)
    u1 = _dot(h, w1_ref[...])
    u3 = _dot(h, w3_ref[...])
    act = (_silu(u1) * u3).astype(BF16)
    out = x + _dot(act, w2_ref[...])
    if final:
        out = _rmsnorm(out, gf_ref[...])
    o_ref[...] = out


def _ffn(x, g, w1, w3, w2, gf, *, final):
    n, d = x.shape
    dff = w1.shape[1]
    rows = FFN_ROWS
    assert n % rows == 0
    resident = 2 * 3 * d * dff + 4 * rows * d * 4 + 3 * rows * dff * 4
    return pl.pallas_call(
        functools.partial(_ffn_body, final=final),
        out_shape=jax.ShapeDtypeStruct((n, d), F32),
        grid=(n // rows,),
        in_specs=[
            pl.BlockSpec((rows, d), lambda i: (i, 0)),
            _const_spec((1, d)),
            _const_spec((d, dff)),
            _const_spec((d, dff)),
            _const_spec((dff, d)),
            _const_spec((1, d)),
        ],
        out_specs=pl.BlockSpec((rows, d), lambda i: (i, 0)),
        compiler_params=pltpu.CompilerParams(
            dimension_semantics=("arbitrary",), vmem_limit_bytes=_vmem_limit(resident)),
        name="ffn_final" if final else "ffn",
    )(x, g, w1, w3, w2, gf)


def _pool_groups(ext_read, h, pos, w_ref, sc):
    gw = h.shape[-1] // len(POOL_WINDOWS)
    ys = []
    for g, w in enumerate(POOL_WINDOWS):
        cs = slice(g * gw, (g + 1) * gw)
        hg = h[:, cs]
        s = hg
        for k in range(1, w):
            s = s + ext_read(k, cs)
        cnt = jnp.minimum(pos + 1, w).astype(F32)
        diff = (s / cnt - hg).astype(BF16)
        ys.append(_dot(diff, w_ref[g]))
    return jnp.concatenate(ys, axis=-1) * sc


def _pool_prompt_body(x_ref, g_ref, w_ref, sc_ref, o_ref, ctx_ref, ext_ref, *, rows, n_tiles):
    t = pl.program_id(1)
    d = x_ref.shape[-1]

    @pl.when(t == 0)
    def _():
        ext_ref[0:POOL_MAXW, :] = jnp.zeros((POOL_MAXW, d), F32)

    x = x_ref[...]
    h = _rmsnorm(x, g_ref[...])
    ext_ref[POOL_MAXW:POOL_MAXW + rows, :] = h
    pos = t * rows + lax.broadcasted_iota(jnp.int32, (rows, 1), 0)

    def ext_read(k, cs):
        return ext_ref[POOL_MAXW - k:POOL_MAXW - k + rows, cs]

    o_ref[...] = x + _pool_groups(ext_read, h, pos, w_ref, sc_ref[...])
    ext_ref[0:POOL_MAXW, :] = ext_ref[rows:rows + POOL_MAXW, :]

    @pl.when(t == n_tiles - 1)
    def _():
        ctx_ref[...] = ext_ref[1:POOL_MAXW, :]


def _pool_prompt(x, g, w, sc, *, batch, seq):
    n, d = x.shape
    rows = POOL_ROWS
    n_tiles = seq // rows
    assert seq % rows == 0 and rows >= POOL_MAXW
    resident = 4 * rows * d * 4 + (rows + POOL_MAXW) * d * 4 + 2 * w.size
    return pl.pallas_call(
        functools.partial(_pool_prompt_body, rows=rows, n_tiles=n_tiles),
        out_shape=(jax.ShapeDtypeStruct((n, d), F32),
                   jax.ShapeDtypeStruct((batch, POOL_BUF, d), F32)),
        grid=(batch, n_tiles),
        in_specs=[
            pl.BlockSpec((rows, d), lambda b, t: (b * n_tiles + t, 0)),
            _const_spec((1, d)),
            _const_spec(w.shape),
            _const_spec((1, d)),
        ],
        out_specs=(pl.BlockSpec((rows, d), lambda b, t: (b * n_tiles + t, 0)),
                   pl.BlockSpec((None, POOL_BUF, d), lambda b, t: (b, 0, 0))),
        scratch_shapes=[pltpu.VMEM((rows + POOL_MAXW, d), F32)],
        compiler_params=pltpu.CompilerParams(
            dimension_semantics=("arbitrary", "arbitrary"), vmem_limit_bytes=_vmem_limit(resident)),
        name="pool_prompt",
    )(x, g, w, sc)


def _pool_sample_body(x_ref, ctx_ref, g_ref, w_ref, sc_ref, o_ref, nctx_ref, ext_ref, *, seqs, steps):
    d = x_ref.shape[-1]
    x = x_ref[...]
    h = _rmsnorm(x, g_ref[...])
    ext_ref[:, 0:1, :] = jnp.zeros((seqs, 1, d), F32)
    ext_ref[:, 1:POOL_MAXW, :] = ctx_ref[...]
    ext_ref[:, POOL_MAXW:POOL_MAXW + steps, :] = h
    pos = PAST_LEN + lax.broadcasted_iota(jnp.int32, (seqs * steps, 1), 0) % steps
    h2 = h.reshape(seqs * steps, d)

    def ext_read(k, cs):
        return ext_ref[:, POOL_MAXW - k:POOL_MAXW - k + steps, cs].reshape(seqs * steps, cs.stop - cs.start)

    y = _pool_groups(ext_read, h2, pos, w_ref, sc_ref[...])
    o_ref[...] = x + y.reshape(seqs, steps, d)
    nctx_ref[...] = ext_ref[:, steps + 1:steps + POOL_MAXW, :]


def _pool_sample(x, state_pool, layer, g, w, sc):
    bs, steps, d = x.shape
    seqs = POOL_SEQS
    assert bs % seqs == 0 and steps == V7X_SUBLANES
    resident = 2 * seqs * (2 * steps + 2 * POOL_MAXW) * d * 4 + seqs * (steps + POOL_MAXW) * d * 4
    return pl.pallas_call(
        functools.partial(_pool_sample_body, seqs=seqs, steps=steps),
        out_shape=(jax.ShapeDtypeStruct((bs, steps, d), F32),
                   jax.ShapeDtypeStruct((bs, POOL_BUF, d), F32)),
        grid=(bs // seqs,),
        in_specs=[
            pl.BlockSpec((seqs, steps, d), lambda i: (i, 0, 0)),
            pl.BlockSpec((None, seqs, POOL_BUF, d), lambda i: (layer, i, 0, 0)),
            _const_spec((1, d)),
            _const_spec(w.shape),
            _const_spec((1, d)),
        ],
        out_specs=(pl.BlockSpec((seqs, steps, d), lambda i: (i, 0, 0)),
                   pl.BlockSpec((seqs, POOL_BUF, d), lambda i: (i, 0, 0))),
        scratch_shapes=[pltpu.VMEM((seqs, steps + POOL_MAXW, d), F32)],
        compiler_params=pltpu.CompilerParams(
            dimension_semantics=("arbitrary",), vmem_limit_bytes=_vmem_limit(resident)),
        name="pool_sample",
    )(x, state_pool, g, w, sc)


def _gla_tables(rows, seg):
    levels = int(np.log2(seg))
    assert 2 ** levels == seg and rows % seg == 0
    idx = np.arange(rows)
    col = idx[None, :]
    row = idx[:, None]
    wall = [((col >= (row // seg) * seg) & (col <= row))]
    masks = [row == col]
    for l in range(1, levels + 1):
        s = 2 ** l
        mid = (row // s) * s + s // 2 - 1
        second = row > mid
        wall.append(np.where(second, (col > mid) & (col <= row), (col > row) & (col <= mid)))
        masks.append((row // s == col // s) & second & (col <= mid))
    wall = np.concatenate(wall, axis=0).astype(np.float32)
    masks = np.stack(masks, axis=0).astype(np.float32)
    return jnp.asarray(wall, BF16), jnp.asarray(masks, F32), levels


def _gla_front(x, g_ref, wqkvr_ref, wa_ref, wa2_ref, ba_ref, wall_ref):
    rows = x.shape[0]
    dk = wa2_ref.shape[1]
    dkh = dk // GLA_HEADS
    h = _rmsnorm(x, g_ref[...]).astype(BF16)
    proj = _dot(h, wqkvr_ref[...])
    a_low = _dot(h, wa_ref[...])
    xg = _dot(a_low.astype(BF16), wa2_ref[...]) + ba_ref[...]
    la = -(jnp.maximum(-xg, 0.0) + jnp.log1p(jnp.exp(-jnp.abs(xg)))) * (1.0 / GLA_GATE_TEMP)
    parts = _split3(la)
    wall = wall_ref[...]
    xall = _dot(wall, parts[0]) + _dot(wall, parts[1]) + _dot(wall, parts[2])
    q = proj[:, 0:dk] * (dkh ** -0.5)
    k = proj[:, dk:2 * dk]
    v = proj[:, 2 * dk:4 * dk]
    r = proj[:, 4 * dk:6 * dk]
    return q, k, v, r, parts, xall, rows


def _gla_intra(q_h, k_h, xall, masks_ref, levels, rows, hs):
    a = _dot_nt(q_h.astype(BF16), k_h.astype(BF16)) * masks_ref[0]
    row = lax.broadcasted_iota(jnp.int32, (rows, 1), 0)
    for l in range(1, levels + 1):
        second = ((row >> (l - 1)) & 1) == 1
        e = jnp.exp(xall[l * rows:(l + 1) * rows, hs])
        t = (jnp.where(second, q_h, k_h) * e).astype(BF16)
        a = a + _dot_nt(t, t) * masks_ref[l]
    return a


def _gla_out(x, o, r, gn_ref, wo_ref):
    dv = o.shape[-1]
    dvh = dv // GLA_HEADS
    normed = []
    for hd in range(GLA_HEADS):
        oh = o[:, hd * dvh:(hd + 1) * dvh]
        normed.append(oh * lax.rsqrt(jnp.mean(oh * oh, axis=-1, keepdims=True) + EPS))
    o = jnp.concatenate(normed, axis=-1) * gn_ref[...]
    o = (o * _silu(r)).astype(BF16)
    return x + _dot(o, wo_ref[...])


def _gla_prompt_body(x_ref, g_ref, wqkvr_ref, wa_ref, wa2_ref, ba_ref, gn_ref, wo_ref,
                     wall_ref, masks_ref, ones_ref, o_ref, sout_ref, s_ref, *, levels, n_chunks):
    c = pl.program_id(1)

    @pl.when(c == 0)
    def _():
        s_ref[...] = jnp.zeros(s_ref.shape, F32)

    x = x_ref[...]
    q, k, v, r, parts, xall, rows = _gla_front(x, g_ref, wqkvr_ref, wa_ref, wa2_ref, ba_ref, wall_ref)
    dkh = q.shape[-1] // GLA_HEADS
    dvh = v.shape[-1] // GLA_HEADS
    b = xall[0:rows]
    b_last = b[rows - 1:rows, :]
    qb = (q * jnp.exp(b)).astype(BF16)
    kend = (k * jnp.exp(b_last - b)).astype(BF16)
    vb = v.astype(BF16)
    ones = ones_ref[...]
    dec = jnp.exp(_dot_tn(parts[0], ones) + _dot_tn(parts[1], ones) + _dot_tn(parts[2], ones))
    outs = []
    for hd in range(GLA_HEADS):
        hs = slice(hd * dkh, (hd + 1) * dkh)
        vs = slice(hd * dvh, (hd + 1) * dvh)
        s_old = s_ref[hd]
        a = _gla_intra(q[:, hs], k[:, hs], xall, masks_ref, levels, rows, hs)
        outs.append(_dot(a.astype(BF16), vb[:, vs]) + _dot(qb[:, hs], s_old.astype(BF16)))
        s_ref[hd] = dec[hs, :] * s_old + _dot_tn(kend[:, hs], vb[:, vs])
    o_ref[...] = _gla_out(x, jnp.concatenate(outs, axis=-1), r, gn_ref, wo_ref)

    @pl.when(c == n_chunks - 1)
    def _():
        sout_ref[...] = s_ref[...]


def _gla_prompt(x, g, wqkvr, wa, wa2, ba, gn, wo, *, batch, seq):
    n, d = x.shape
    rows = GLA_CHUNK
    n_chunks = seq // rows
    assert seq % rows == 0
    dk = wa2.shape[1]
    dv = wo.shape[0]
    dkh, dvh = dk // GLA_HEADS, dv // GLA_HEADS
    wall, masks, levels = _gla_tables(rows, rows)
    ones = jnp.ones((rows, dvh), BF16)
    consts = (g, wqkvr, wa, wa2, ba, gn, wo, wall, masks, ones)
    resident = sum(a.size * a.dtype.itemsize for a in consts) + 4 * rows * d * 4 + 3 * GLA_HEADS * dkh * dvh * 4
    resident += rows * (wqkvr.shape[1] + (levels + 1) * dk) * 4
    return pl.pallas_call(
        functools.partial(_gla_prompt_body, levels=levels, n_chunks=n_chunks),
        out_shape=(jax.ShapeDtypeStruct((n, d), F32),
                   jax.ShapeDtypeStruct((batch, GLA_HEADS, dkh, dvh), F32)),
        grid=(batch, n_chunks),
        in_specs=[pl.BlockSpec((rows, d), lambda b, c: (b * n_chunks + c, 0))]
        + [_const_spec(a.shape) for a in consts],
        out_specs=(pl.BlockSpec((rows, d), lambda b, c: (b * n_chunks + c, 0)),
                   pl.BlockSpec((None, GLA_HEADS, dkh, dvh), lambda b, c: (b, 0, 0, 0))),
        scratch_shapes=[pltpu.VMEM((GLA_HEADS, dkh, dvh), F32)],
        compiler_params=pltpu.CompilerParams(
            dimension_semantics=("arbitrary", "arbitrary"), vmem_limit_bytes=_vmem_limit(resident)),
        name="gla_prompt",
    )(x, *consts)


def _gla_sample_body(x_ref, s0_ref, g_ref, wqkvr_ref, wa_ref, wa2_ref, ba_ref, gn_ref, wo_ref,
                     wall_ref, masks_ref, segs_ref, o_ref, sout_ref, *, levels, seqs, steps):
    x = x_ref[...]
    q, k, v, r, parts, xall, rows = _gla_front(x, g_ref, wqkvr_ref, wa_ref, wa2_ref, ba_ref, wall_ref)
    dk = q.shape[-1]
    dkh = dk // GLA_HEADS
    dvh = v.shape[-1] // GLA_HEADS
    b = xall[0:rows]
    b3 = b.reshape(seqs, steps, dk)
    b_last = b3[:, steps - 1:steps, :]
    qb = (q * jnp.exp(b)).astype(BF16)
    kend = (k.reshape(seqs, steps, dk) * jnp.exp(b_last - b3)).reshape(rows, dk).astype(BF16)
    vb = v.astype(BF16)
    segs = segs_ref[...]
    dec = jnp.exp(_dot_tn(parts[0], segs) + _dot_tn(parts[1], segs) + _dot_tn(parts[2], segs))
    outs = []
    for hd in range(GLA_HEADS):
        hs = slice(hd * dkh, (hd + 1) * dkh)
        vs = slice(hd * dvh, (hd + 1) * dvh)
        a = _gla_intra(q[:, hs], k[:, hs], xall, masks_ref, levels, rows, hs)
        o_h = _dot(a.astype(BF16), vb[:, vs])
        o_state = []
        for s in range(seqs):
            rs = slice(s * steps, (s + 1) * steps)
            s_old = s0_ref[s, hd]
            o_state.append(_dot(qb[rs, hs], s_old.astype(BF16)))
            sout_ref[s, hd] = (dec[hs, s * dvh:(s + 1) * dvh] * s_old
                               + _dot_tn(kend[rs, hs], vb[rs, vs]))
        outs.append(o_h + jnp.concatenate(o_state, axis=0))
    o_ref[...] = _gla_out(x, jnp.concatenate(outs, axis=-1), r, gn_ref, wo_ref)


def _gla_sample(x, state_gla, layer, g, wqkvr, wa, wa2, ba, gn, wo, *, steps):
    n, d = x.shape
    seqs = GLA_SEQS
    rows = seqs * steps
    bs = n // steps
    assert bs % seqs == 0
    dk = wa2.shape[1]
    dv = wo.shape[0]
    dkh, dvh = dk // GLA_HEADS, dv // GLA_HEADS
    wall, masks, levels = _gla_tables(rows, steps)
    segs = jnp.asarray(np.repeat(np.repeat(np.eye(seqs, dtype=np.float32), steps, axis=0), dvh, axis=1), BF16)
    consts = (g, wqkvr, wa, wa2, ba, gn, wo, wall, masks, segs)
    state_block = seqs * GLA_HEADS * dkh * dvh * 4
    resident = sum(a.size * a.dtype.itemsize for a in consts) + 4 * rows * d * 4 + 4 * state_block
    resident += rows * (wqkvr.shape[1] + (levels + 1) * dk) * 4 + dk * seqs * dvh * 4
    return pl.pallas_call(
        functools.partial(_gla_sample_body, levels=levels, seqs=seqs, steps=steps),
        out_shape=(jax.ShapeDtypeStruct((n, d), F32),
                   jax.ShapeDtypeStruct((bs, GLA_HEADS, dkh, dvh), F32)),
        grid=(bs // seqs,),
        in_specs=[pl.BlockSpec((rows, d), lambda i: (i, 0)),
                  pl.BlockSpec((None, seqs, GLA_HEADS, dkh, dvh), lambda i: (layer, i, 0, 0, 0))]
        + [_const_spec(a.shape) for a in consts],
        out_specs=(pl.BlockSpec((rows, d), lambda i: (i, 0)),
                   pl.BlockSpec((seqs, GLA_HEADS, dkh, dvh), lambda i: (i, 0, 0, 0))),
        compiler_params=pltpu.CompilerParams(
            dimension_semantics=("arbitrary",), vmem_limit_bytes=_vmem_limit(resident)),
        name="gla_sample",
    )(x, state_gla, *consts)


def kernel(x_prompt, x_sample, state_pool, state_gla, norm_mix, norm_ffn, norm_final, pool_w, pool_scale,
           gla_w_in, gla_w_a2, gla_b_a, gla_norm, gla_w_o, ffn_w1, ffn_w3, ffn_w2):
    batch, seq, d = x_prompt.shape
    bs, steps, _ = x_sample.shape
    depth = norm_mix.shape[0]
    dk = gla_w_a2.shape[-1]
    dv = gla_w_o.shape[1]
    rank = gla_w_a2.shape[1]
    n_qkvr = 2 * dk + 2 * dv

    xp = x_prompt.reshape(batch * seq, d)
    xs = x_sample.reshape(bs * steps, d)
    row = lambda a: a.reshape(1, -1)
    gf = row(norm_final)

    new_pool_p, new_pool_s, new_gla_p, new_gla_s = [], [], [], []
    for i in range(depth):
        j = i // 2
        g = row(norm_mix[i])
        if i % 2 == 0:
            w = pool_w[j].astype(BF16)
            sc = row(pool_scale[j])
            xp, c = _pool_prompt(xp, g, w, sc, batch=batch, seq=seq)
            new_pool_p.append(c)
            xs3, c = _pool_sample(xs.reshape(bs, steps, d), state_pool, j, g, w, sc)
            xs = xs3.reshape(bs * steps, d)
            new_pool_s.append(c)
        else:
            w_in = gla_w_in[j]
            wqkvr = w_in[:, :n_qkvr].astype(BF16)
            wa = jnp.pad(w_in[:, n_qkvr:], ((0, 0), (0, V7X_LANES - rank))).astype(BF16)
            wa2 = jnp.pad(gla_w_a2[j], ((0, V7X_LANES - rank), (0, 0))).astype(BF16)
            args = (g, wqkvr, wa, wa2, row(gla_b_a[j]), row(gla_norm[j]), gla_w_o[j].astype(BF16))
            xp, s = _gla_prompt(xp, *args, batch=batch, seq=seq)
            new_gla_p.append(s)
            xs, s = _gla_sample(xs, state_gla, j, *args, steps=steps)
            new_gla_s.append(s)
        final = i == depth - 1
        ffn_args = (row(norm_ffn[i]), ffn_w1[i].astype(BF16), ffn_w3[i].astype(BF16), ffn_w2[i].astype(BF16), gf)
        xp = _ffn(xp, *ffn_args, final=final)
        xs = _ffn(xs, *ffn_args, final=final)

    return (xp.reshape(batch, seq, d), xs.reshape(bs, steps, d),
            jnp.stack(new_pool_p), jnp.stack(new_gla_p),
            jnp.stack(new_pool_s), jnp.stack(new_gla_s))
```

```python
import functools

import numpy as np
import jax
import jax.numpy as jnp
from jax import lax
from jax.experimental import pallas as pl
from jax.experimental.pallas import tpu as pltpu

F32 = jnp.float32
BF16 = jnp.bfloat16

PAST_LEN = 16384
POOL_WINDOWS = (2, 4, 8, 16)
POOL_MAXW = max(POOL_WINDOWS)
POOL_BUF = POOL_MAXW - 1
GLA_HEADS = 4
GLA_GATE_RANK = 16
GLA_GATE_TEMP = 16.0
EPS = 1e-6

V7X_LANES = 128
V7X_SUBLANES = 8
V7X_VMEM_BYTES = 64 * 1024 * 1024

FFN_ROWS = 256
POOL_ROWS = 256
POOL_SEQS = 16
GLA_CHUNK = 128
GLA_ROWS = 256
GLA_SEQS = 8


def _vmem_limit(resident_bytes):
    return int(min(V7X_VMEM_BYTES - (4 << 20), 2 * resident_bytes + (16 << 20)))


def _const_spec(shape):
    nd = len(shape)
    return pl.BlockSpec(shape, lambda *_: (0,) * nd, pipeline_mode=pl.Buffered(1))


def _rmsnorm(x, g):
    ms = jnp.mean(x * x, axis=-1, keepdims=True)
    return x * lax.rsqrt(ms + EPS) * g


def _silu(x):
    return x / (1.0 + jnp.exp(-x))


def _dot(a, b):
    return jnp.dot(a, b, preferred_element_type=F32)


def _dot_nt(a, b):
    return lax.dot_general(a, b, (((1,), (1,)), ((), ())), preferred_element_type=F32)


def _dot_tn(a, b):
    return lax.dot_general(a, b, (((0,), (0,)), ((), ())), preferred_element_type=F32)


def _split3(x):
    hi = x.astype(BF16)
    r1 = x - hi.astype(F32)
    mid = r1.astype(BF16)
    lo = (r1 - mid.astype(F32)).astype(BF16)
    return hi, mid, lo


def _ffn_body(x_ref, g_ref, w1_ref, w3_ref, w2_ref, gf_ref, o_ref, *, final):
    x = x_ref[...]
    h = _rmsnorm(x, g_ref[...]).astype(BF16)
    u1 = _dot(h, w1_ref[...])
    u3 = _dot(h, w3_ref[...])
    act = (_silu(u1) * u3).astype(BF16)
    out = x + _dot(act, w2_ref[...])
    if final:
        out = _rmsnorm(out, gf_ref[...])
    o_ref[...] = out


def _ffn(x, g, w1, w3, w2, gf, *, final):
    n, d = x.shape
    dff = w1.shape[1]
    rows = FFN_ROWS
    assert n % rows == 0
    resident = 2 * 3 * d * dff + 4 * rows * d * 4 + 3 * rows * dff * 4
    return pl.pallas_call(
        functools.partial(_ffn_body, final=final),
        out_shape=jax.ShapeDtypeStruct((n, d), F32),
        grid=(n // rows,),
        in_specs=[
            pl.BlockSpec((rows, d), lambda i: (i, 0)),
            _const_spec((1, d)),
            _const_spec((d, dff)),
            _const_spec((d, dff)),
            _const_spec((dff, d)),
            _const_spec((1, d)),
        ],
        out_specs=pl.BlockSpec((rows, d), lambda i: (i, 0)),
        compiler_params=pltpu.CompilerParams(
            dimension_semantics=("arbitrary",), vmem_limit_bytes=_vmem_limit(resident)),
        name="ffn_final" if final else "ffn",
    )(x, g, w1, w3, w2, gf)


def _pool_groups(ext_read, h, pos, w_ref, sc):
    gw = h.shape[-1] // len(POOL_WINDOWS)
    ys = []
    for g, w in enumerate(POOL_WINDOWS):
        cs = slice(g * gw, (g + 1) * gw)
        hg = h[:, cs]
        s = hg
        for k in range(1, w):
            s = s + ext_read(k, cs)
        cnt = jnp.minimum(pos + 1, w).astype(F32)
        diff = (s / cnt - hg).astype(BF16)
        ys.append(_dot(diff, w_ref[g]))
    return jnp.concatenate(ys, axis=-1) * sc


def _pool_prompt_body(x_ref, g_ref, w_ref, sc_ref, o_ref, ctx_ref, ext_ref, *, rows, n_tiles):
    t = pl.program_id(1)
    d = x_ref.shape[-1]

    @pl.when(t == 0)
    def _():
        ext_ref[0:POOL_MAXW, :] = jnp.zeros((POOL_MAXW, d), F32)

    x = x_ref[...]
    h = _rmsnorm(x, g_ref[...])
    ext_ref[POOL_MAXW:POOL_MAXW + rows, :] = h
    pos = t * rows + lax.broadcasted_iota(jnp.int32, (rows, 1), 0)

    def ext_read(k, cs):
        return ext_ref[POOL_MAXW - k:POOL_MAXW - k + rows, cs]

    o_ref[...] = x + _pool_groups(ext_read, h, pos, w_ref, sc_ref[...])
    ext_ref[0:POOL_MAXW, :] = ext_ref[rows:rows + POOL_MAXW, :]

    @pl.when(t == n_tiles - 1)
    def _():
        ctx_ref[...] = ext_ref[1:POOL_MAXW, :]


def _pool_prompt(x, g, w, sc, *, batch, seq):
    n, d = x.shape
    rows = POOL_ROWS
    n_tiles = seq // rows
    assert seq % rows == 0 and rows >= POOL_MAXW
    resident = 4 * rows * d * 4 + (rows + POOL_MAXW) * d * 4 + 2 * w.size
    return pl.pallas_call(
        functools.partial(_pool_prompt_body, rows=rows, n_tiles=n_tiles),
        out_shape=(jax.ShapeDtypeStruct((n, d), F32),
                   jax.ShapeDtypeStruct((batch, POOL_BUF, d), F32)),
        grid=(batch, n_tiles),
        in_specs=[
            pl.BlockSpec((rows, d), lambda b, t: (b * n_tiles + t, 0)),
            _const_spec((1, d)),
            _const_spec(w.shape),
            _const_spec((1, d)),
        ],
        out_specs=(pl.BlockSpec((rows, d), lambda b, t: (b * n_tiles + t, 0)),
                   pl.BlockSpec((None, POOL_BUF, d), lambda b, t: (b, 0, 0))),
        scratch_shapes=[pltpu.VMEM((rows + POOL_MAXW, d), F32)],
        compiler_params=pltpu.CompilerParams(
            dimension_semantics=("arbitrary", "arbitrary"), vmem_limit_bytes=_vmem_limit(resident)),
        name="pool_prompt",
    )(x, g, w, sc)


def _pool_sample_body(x_ref, ctx_ref, g_ref, w_ref, sc_ref, o_ref, nctx_ref, ext_ref, *, seqs, steps):
    d = x_ref.shape[-1]
    x = x_ref[...]
    h = _rmsnorm(x, g_ref[...])
    ext_ref[:, 0:1, :] = jnp.zeros((seqs, 1, d), F32)
    ext_ref[:, 1:POOL_MAXW, :] = ctx_ref[...]
    ext_ref[:, POOL_MAXW:POOL_MAXW + steps, :] = h
    pos = PAST_LEN + lax.broadcasted_iota(jnp.int32, (seqs * steps, 1), 0) % steps
    h2 = h.reshape(seqs * steps, d)

    def ext_read(k, cs):
        return ext_ref[:, POOL_MAXW - k:POOL_MAXW - k + steps, cs].reshape(seqs * steps, cs.stop - cs.start)

    y = _pool_groups(ext_read, h2, pos, w_ref, sc_ref[...])
    o_ref[...] = x + y.reshape(seqs, steps, d)
    nctx_ref[...] = ext_ref[:, steps + 1:steps + POOL_MAXW, :]


def _pool_sample(x, state_pool, layer, g, w, sc):
    bs, steps, d = x.shape
    seqs = POOL_SEQS
    assert bs % seqs == 0 and steps == V7X_SUBLANES
    resident = 2 * seqs * (2 * steps + 2 * POOL_MAXW) * d * 4 + seqs * (steps + POOL_MAXW) * d * 4
    return pl.pallas_call(
        functools.partial(_pool_sample_body, seqs=seqs, steps=steps),
        out_shape=(jax.ShapeDtypeStruct((bs, steps, d), F32),
                   jax.ShapeDtypeStruct((bs, POOL_BUF, d), F32)),
        grid=(bs // seqs,),
        in_specs=[
            pl.BlockSpec((seqs, steps, d), lambda i: (i, 0, 0)),
            pl.BlockSpec((None, seqs, POOL_BUF, d), lambda i: (layer, i, 0, 0)),
            _const_spec((1, d)),
            _const_spec(w.shape),
            _const_spec((1, d)),
        ],
        out_specs=(pl.BlockSpec((seqs, steps, d), lambda i: (i, 0, 0)),
                   pl.BlockSpec((seqs, POOL_BUF, d), lambda i: (i, 0, 0))),
        scratch_shapes=[pltpu.VMEM((seqs, steps + POOL_MAXW, d), F32)],
        compiler_params=pltpu.CompilerParams(
            dimension_semantics=("arbitrary",), vmem_limit_bytes=_vmem_limit(resident)),
        name="pool_sample",
    )(x, state_pool, g, w, sc)


def _gla_tables(rows, seg, score_rows):
    levels = int(np.log2(seg))
    assert 2 ** levels == seg and rows % score_rows == 0 and score_rows % seg == 0
    idx = np.arange(rows)
    tri = (idx[None, :] >= (idx[:, None] // seg) * seg) & (idx[None, :] <= idx[:, None])
    row = np.arange(score_rows)[:, None]
    col = np.arange(score_rows)[None, :]
    masks = [row == col]
    for l in range(1, levels + 1):
        s = 2 ** l
        mid = (row // s) * s + s // 2 - 1
        masks.append((row // s == col // s) & (row > mid) & (col <= mid))
    masks = np.stack(masks, axis=0).astype(np.float32)
    return jnp.asarray(tri.astype(np.float32), BF16), jnp.asarray(masks, F32), levels


def _gla_front(x, g_ref, wqkvr_ref, wa_ref, wa2_ref, ba_ref, tri_ref):
    dk = wa2_ref.shape[1]
    dkh = dk // GLA_HEADS
    h = _rmsnorm(x, g_ref[...]).astype(BF16)
    a_low = _dot(h, wa_ref[...])
    xg = _dot(a_low.astype(BF16), wa2_ref[...]) + ba_ref[...]
    la = -(jnp.maximum(-xg, 0.0) + jnp.log1p(jnp.exp(-jnp.abs(xg)))) * (1.0 / GLA_GATE_TEMP)
    hi, mid, lo = _split3(la)
    tri = tri_ref[...]
    b = _dot(tri, hi) + _dot(tri, mid) + _dot(tri, lo)
    proj = _dot(h, wqkvr_ref[...])
    q = proj[:, 0:dk] * (dkh ** -0.5)
    k = proj[:, dk:2 * dk]
    v = proj[:, 2 * dk:4 * dk]
    r = proj[:, 4 * dk:6 * dk]
    return q, k, v, r, b


def _gla_level_operands(q, k, b, levels):
    rows, dk = b.shape
    row = lax.broadcasted_iota(jnp.int32, (rows, 1), 0)
    sub = lax.broadcasted_iota(jnp.int32, (1, V7X_SUBLANES, 1), 1)
    ops = []
    for l in range(1, levels + 1):
        s = 2 ** l
        if s >= V7X_SUBLANES:
            b3 = b.reshape(rows // s, s, dk)
            m = b3[:, s // 2 - 1:s // 2, :]
        else:
            b3 = b.reshape(rows // V7X_SUBLANES, V7X_SUBLANES, dk)
            picks = [b3[:, j + s // 2 - 1:j + s // 2, :] for j in range(0, V7X_SUBLANES, s)]
            m = picks[-1]
            for idx in range(len(picks) - 2, -1, -1):
                m = jnp.where(sub < (idx + 1) * s, picks[idx], m)
        e = jnp.exp(-jnp.abs(b3 - m)).reshape(rows, dk)
        second = ((row >> (l - 1)) & 1) == 1
        ops.append((jnp.where(second, q, k) * e).astype(BF16))
    return ops


def _gla_intra(qb16_h, kb16_h, ops, masks_ref, rs, hs):
    a = _dot_nt(qb16_h, kb16_h) * masks_ref[0]
    for l, t in enumerate(ops, start=1):
        th = t[rs, hs]
        a = a + _dot_nt(th, th) * masks_ref[l]
    return a


def _lane_column(row_vec):
    return jnp.transpose(row_vec, (1, 0))


def _gla_out(x, o, r, gn_ref, wo_ref):
    dv = o.shape[-1]
    dvh = dv // GLA_HEADS
    normed = []
    for hd in range(GLA_HEADS):
        oh = o[:, hd * dvh:(hd + 1) * dvh]
        normed.append(oh * lax.rsqrt(jnp.mean(oh * oh, axis=-1, keepdims=True) + EPS))
    o = jnp.concatenate(normed, axis=-1) * gn_ref[...]
    o = (o * _silu(r)).astype(BF16)
    return x + _dot(o, wo_ref[...])


def _gla_prompt_body(x_ref, g_ref, wqkvr_ref, wa_ref, wa2_ref, ba_ref, gn_ref, wo_ref,
                     tri_ref, masks_ref, o_ref, sout_ref, s_ref, *, levels, chunk, n_tiles):
    c = pl.program_id(1)

    @pl.when(c == 0)
    def _():
        s_ref[...] = jnp.zeros(s_ref.shape, F32)

    x = x_ref[...]
    rows = x.shape[0]
    q, k, v, r, b = _gla_front(x, g_ref, wqkvr_ref, wa_ref, wa2_ref, ba_ref, tri_ref)
    dkh = q.shape[-1] // GLA_HEADS
    dvh = v.shape[-1] // GLA_HEADS
    ops = _gla_level_operands(q, k, b, levels)
    q16 = q.astype(BF16)
    k16 = k.astype(BF16)
    qb = (q * jnp.exp(b)).astype(BF16)
    vb = v.astype(BF16)
    states = [s_ref[hd] for hd in range(GLA_HEADS)]
    out_rows = []
    for ci in range(rows // chunk):
        rs = slice(ci * chunk, (ci + 1) * chunk)
        b_c = b[rs]
        b_last = b_c[chunk - 1:chunk, :]
        kend = (k[rs] * jnp.exp(b_last - b_c)).astype(BF16)
        outs = []
        for hd in range(GLA_HEADS):
            hs = slice(hd * dkh, (hd + 1) * dkh)
            vs = slice(hd * dvh, (hd + 1) * dvh)
            s_old = states[hd]
            a = _gla_intra(q16[rs, hs], k16[rs, hs], ops, masks_ref, rs, hs)
            outs.append(_dot(a.astype(BF16), vb[rs, vs]) + _dot(qb[rs, hs], s_old.astype(BF16)))
            dec = jnp.exp(_lane_column(b_last[:, hs]))
            states[hd] = dec * s_old + _dot_tn(kend[:, hs], vb[rs, vs])
        out_rows.append(jnp.concatenate(outs, axis=-1))
    for hd in range(GLA_HEADS):
        s_ref[hd] = states[hd]
    o_ref[...] = _gla_out(x, jnp.concatenate(out_rows, axis=0), r, gn_ref, wo_ref)

    @pl.when(c == n_tiles - 1)
    def _():
        sout_ref[...] = s_ref[...]


def _gla_prompt(x, g, wqkvr, wa, wa2, ba, gn, wo, *, batch, seq):
    n, d = x.shape
    rows = GLA_ROWS
    chunk = GLA_CHUNK
    n_tiles = seq // rows
    assert seq % rows == 0 and rows % chunk == 0
    dk = wa2.shape[1]
    dv = wo.shape[0]
    dkh, dvh = dk // GLA_HEADS, dv // GLA_HEADS
    tri, masks, levels = _gla_tables(rows, chunk, chunk)
    consts = (g, wqkvr, wa, wa2, ba, gn, wo, tri, masks)
    resident = sum(a.size * a.dtype.itemsize for a in consts) + 4 * rows * d * 4 + 3 * GLA_HEADS * dkh * dvh * 4
    resident += rows * (wqkvr.shape[1] * 4 + (levels + 4) * dk * 4)
    return pl.pallas_call(
        functools.partial(_gla_prompt_body, levels=levels, chunk=chunk, n_tiles=n_tiles),
        out_shape=(jax.ShapeDtypeStruct((n, d), F32),
                   jax.ShapeDtypeStruct((batch, GLA_HEADS, dkh, dvh), F32)),
        grid=(batch, n_tiles),
        in_specs=[pl.BlockSpec((rows, d), lambda b, c: (b * n_tiles + c, 0))]
        + [_const_spec(a.shape) for a in consts],
        out_specs=(pl.BlockSpec((rows, d), lambda b, c: (b * n_tiles + c, 0)),
                   pl.BlockSpec((None, GLA_HEADS, dkh, dvh), lambda b, c: (b, 0, 0, 0))),
        scratch_shapes=[pltpu.VMEM((GLA_HEADS, dkh, dvh), F32)],
        compiler_params=pltpu.CompilerParams(
            dimension_semantics=("arbitrary", "arbitrary"), vmem_limit_bytes=_vmem_limit(resident)),
        name="gla_prompt",
    )(x, *consts)


def _gla_sample_body(x_ref, s0_ref, g_ref, wqkvr_ref, wa_ref, wa2_ref, ba_ref, gn_ref, wo_ref,
                     tri_ref, masks_ref, o_ref, sout_ref, *, levels, seqs, steps):
    x = x_ref[...]
    rows = x.shape[0]
    q, k, v, r, b = _gla_front(x, g_ref, wqkvr_ref, wa_ref, wa2_ref, ba_ref, tri_ref)
    dk = q.shape[-1]
    dkh = dk // GLA_HEADS
    dvh = v.shape[-1] // GLA_HEADS
    ops = _gla_level_operands(q, k, b, levels)
    q16 = q.astype(BF16)
    k16 = k.astype(BF16)
    b3 = b.reshape(seqs, steps, dk)
    b_last = b3[:, steps - 1:steps, :]
    qb = (q * jnp.exp(b)).astype(BF16)
    kend = (k.reshape(seqs, steps, dk) * jnp.exp(b_last - b3)).reshape(rows, dk).astype(BF16)
    vb = v.astype(BF16)
    all_rows = slice(0, rows)
    outs = []
    for hd in range(GLA_HEADS):
        hs = slice(hd * dkh, (hd + 1) * dkh)
        vs = slice(hd * dvh, (hd + 1) * dvh)
        a = _gla_intra(q16[:, hs], k16[:, hs], ops, masks_ref, all_rows, hs)
        o_h = _dot(a.astype(BF16), vb[:, vs])
        o_state = []
        for s in range(seqs):
            rs = slice(s * steps, (s + 1) * steps)
            s_old = s0_ref[s, hd]
            o_state.append(_dot(qb[rs, hs], s_old.astype(BF16)))
            dec = jnp.exp(_lane_column(b[(s + 1) * steps - 1:(s + 1) * steps, hs]))
            sout_ref[s, hd] = dec * s_old + _dot_tn(kend[rs, hs], vb[rs, vs])
        outs.append(o_h + jnp.concatenate(o_state, axis=0))
    o_ref[...] = _gla_out(x, jnp.concatenate(outs, axis=-1), r, gn_ref, wo_ref)


def _gla_sample(x, state_gla, layer, g, wqkvr, wa, wa2, ba, gn, wo, *, steps):
    n, d = x.shape
    seqs = GLA_SEQS
    rows = seqs * steps
    bs = n // steps
    assert bs % seqs == 0
    dk = wa2.shape[1]
    dv = wo.shape[0]
    dkh, dvh = dk // GLA_HEADS, dv // GLA_HEADS
    tri, masks, levels = _gla_tables(rows, steps, rows)
    consts = (g, wqkvr, wa, wa2, ba, gn, wo, tri, masks)
    state_block = seqs * GLA_HEADS * dkh * dvh * 4
    resident = sum(a.size * a.dtype.itemsize for a in consts) + 4 * rows * d * 4 + 4 * state_block
    resident += rows * (wqkvr.shape[1] * 4 + (levels + 4) * dk * 4)
    return pl.pallas_call(
        functools.partial(_gla_sample_body, levels=levels, seqs=seqs, steps=steps),
        out_shape=(jax.ShapeDtypeStruct((n, d), F32),
                   jax.ShapeDtypeStruct((bs, GLA_HEADS, dkh, dvh), F32)),
        grid=(bs // seqs,),
        in_specs=[pl.BlockSpec((rows, d), lambda i: (i, 0)),
                  pl.BlockSpec((None, seqs, GLA_HEADS, dkh, dvh), lambda i: (layer, i, 0, 0, 0))]
        + [_const_spec(a.shape) for a in consts],
        out_specs=(pl.BlockSpec((rows, d), lambda i: (i, 0)),
                   pl.BlockSpec((seqs, GLA_HEADS, dkh, dvh), lambda i: (i, 0, 0, 0))),
        compiler_params=pltpu.CompilerParams(
            dimension_semantics=("arbitrary",), vmem_limit_bytes=_vmem_limit(resident)),
        name="gla_sample",
    )(x, state_gla, *consts)


def kernel(x_prompt, x_sample, state_pool, state_gla, norm_mix, norm_ffn, norm_final, pool_w, pool_scale,
           gla_w_in, gla_w_a2, gla_b_a, gla_norm, gla_w_o, ffn_w1, ffn_w3, ffn_w2):
    batch, seq, d = x_prompt.shape
    bs, steps, _ = x_sample.shape
    depth = norm_mix.shape[0]
    dk = gla_w_a2.shape[-1]
    dv = gla_w_o.shape[1]
    rank = gla_w_a2.shape[1]
    n_qkvr = 2 * dk + 2 * dv

    xp = x_prompt.reshape(batch * seq, d)
    xs = x_sample.reshape(bs * steps, d)
    row = lambda a: a.reshape(1, -1)
    gf = row(norm_final)

    new_pool_p, new_pool_s, new_gla_p, new_gla_s = [], [], [], []
    for i in range(depth):
        j = i // 2
        g = row(norm_mix[i])
        if i % 2 == 0:
            w = pool_w[j].astype(BF16)
            sc = row(pool_scale[j])
            xp, c = _pool_prompt(xp, g, w, sc, batch=batch, seq=seq)
            new_pool_p.append(c)
            xs3, c = _pool_sample(xs.reshape(bs, steps, d), state_pool, j, g, w, sc)
            xs = xs3.reshape(bs * steps, d)
            new_pool_s.append(c)
        else:
            w_in = gla_w_in[j]
            wqkvr = w_in[:, :n_qkvr].astype(BF16)
            wa = jnp.pad(w_in[:, n_qkvr:], ((0, 0), (0, V7X_LANES - rank))).astype(BF16)
            wa2 = jnp.pad(gla_w_a2[j], ((0, V7X_LANES - rank), (0, 0))).astype(BF16)
            args = (g, wqkvr, wa, wa2, row(gla_b_a[j]), row(gla_norm[j]), gla_w_o[j].astype(BF16))
            xp, s = _gla_prompt(xp, *args, batch=batch, seq=seq)
            new_gla_p.append(s)
            xs, s = _gla_sample(xs, state_gla, j, *args, steps=steps)
            new_gla_s.append(s)
        final = i == depth - 1
        ffn_args = (row(norm_ffn[i]), ffn_w1[i].astype(BF16), ffn_w3[i].astype(BF16), ffn_w2[i].astype(BF16), gf)
        xp = _ffn(xp, *ffn_args, final=final)
        xs = _ffn(xs, *ffn_args, final=final)

    return (xp.reshape(batch, seq, d), xs.reshape(bs, steps, d),
            jnp.stack(new_pool_p), jnp.stack(new_gla_p),
            jnp.stack(new_pool_s), jnp.stack(new_gla_s))
```

```python
import functools

import numpy as np
import jax
import jax.numpy as jnp
from jax import lax
from jax.experimental import pallas as pl
from jax.experimental.pallas import tpu as pltpu

F32 = jnp.float32
BF16 = jnp.bfloat16

PAST_LEN = 16384
POOL_WINDOWS = (2, 4, 8, 16)
POOL_MAXW = max(POOL_WINDOWS)
POOL_BUF = POOL_MAXW - 1
GLA_HEADS = 4
GLA_GATE_RANK = 16
GLA_GATE_TEMP = 16.0
EPS = 1e-6
LOG2_E = 1.4426950408889634

V7X_LANES = 128
V7X_SUBLANES = 8
V7X_VMEM_BYTES = 64 * 1024 * 1024

PROMPT_ROWS = 256
GLA_CHUNK = 128
FFN_ROWS = 512
POOL_SEQS = 16
GLA_SEQS = 8


def _vmem_limit(resident_bytes):
    return int(min(V7X_VMEM_BYTES - (4 << 20), 2 * resident_bytes + (16 << 20)))


def _nbytes(arrays):
    return sum(a.size * a.dtype.itemsize for a in arrays)


def _const_spec(shape):
    nd = len(shape)
    return pl.BlockSpec(shape, lambda *_: (0,) * nd, pipeline_mode=pl.Buffered(1))


def _rmsnorm(x, g):
    ms = jnp.mean(x * x, axis=-1, keepdims=True)
    return x * lax.rsqrt(ms + EPS) * g


def _silu(x):
    return x / (1.0 + jnp.exp(-x))


def _dot(a, b):
    return jnp.dot(a, b, preferred_element_type=F32)


def _dot_nt(a, b):
    return lax.dot_general(a, b, (((1,), (1,)), ((), ())), preferred_element_type=F32)


def _dot_tn(a, b):
    return lax.dot_general(a, b, (((0,), (0,)), ((), ())), preferred_element_type=F32)


def _split3(x):
    hi = x.astype(BF16)
    r1 = x - hi.astype(F32)
    mid = r1.astype(BF16)
    lo = (r1 - mid.astype(F32)).astype(BF16)
    return hi, mid, lo


def _ffn_math(x, g_ref, w1_ref, w3_ref, w2_ref, gf_ref, final):
    h = _rmsnorm(x, g_ref[...]).astype(BF16)
    u1 = _dot(h, w1_ref[...])
    u3 = _dot(h, w3_ref[...])
    act = (_silu(u1) * u3).astype(BF16)
    out = x + _dot(act, w2_ref[...])
    if final:
        out = _rmsnorm(out, gf_ref[...])
    return out


def _ffn_body(x_ref, g_ref, w1_ref, w3_ref, w2_ref, gf_ref, o_ref, *, final):
    o_ref[...] = _ffn_math(x_ref[...], g_ref, w1_ref, w3_ref, w2_ref, gf_ref, final)


def _ffn(x, ffn_consts, *, final):
    n, d = x.shape
    dff = ffn_consts[1].shape[1]
    rows = min(FFN_ROWS, n)
    assert n % rows == 0
    resident = _nbytes(ffn_consts) + 4 * rows * d * 4 + 3 * rows * dff * 4
    return pl.pallas_call(
        functools.partial(_ffn_body, final=final),
        out_shape=jax.ShapeDtypeStruct((n, d), F32),
        grid=(n // rows,),
        in_specs=[pl.BlockSpec((rows, d), lambda i: (i, 0))] + [_const_spec(a.shape) for a in ffn_consts],
        out_specs=pl.BlockSpec((rows, d), lambda i: (i, 0)),
        compiler_params=pltpu.CompilerParams(
            dimension_semantics=("arbitrary",), vmem_limit_bytes=_vmem_limit(resident)),
        name="ffn_final" if final else "ffn",
    )(x, *ffn_consts)


def _pool_groups(ext_read, h, pos, w_ref, sc):
    gw = h.shape[-1] // len(POOL_WINDOWS)
    ys = []
    for g, w in enumerate(POOL_WINDOWS):
        cs = slice(g * gw, (g + 1) * gw)
        hg = h[:, cs]
        s = hg
        for k in range(1, w):
            s = s + ext_read(k, cs)
        cnt = jnp.minimum(pos + 1, w).astype(F32)
        diff = (s / cnt - hg).astype(BF16)
        ys.append(_dot(diff, w_ref[g]))
    return jnp.concatenate(ys, axis=-1) * sc


def _pool_sample_body(x_ref, ctx_ref, g_ref, w_ref, sc_ref, o_ref, nctx_ref, ext_ref, *, seqs, steps):
    d = x_ref.shape[-1]
    x = x_ref[...]
    h = _rmsnorm(x, g_ref[...])
    ext_ref[:, 0:1, :] = jnp.zeros((seqs, 1, d), F32)
    ext_ref[:, 1:POOL_MAXW, :] = ctx_ref[...]
    ext_ref[:, POOL_MAXW:POOL_MAXW + steps, :] = h
    pos = PAST_LEN + lax.broadcasted_iota(jnp.int32, (seqs * steps, 1), 0) % steps
    h2 = h.reshape(seqs * steps, d)

    def ext_read(k, cs):
        return ext_ref[:, POOL_MAXW - k:POOL_MAXW - k + steps, cs].reshape(seqs * steps, cs.stop - cs.start)

    y = _pool_groups(ext_read, h2, pos, w_ref, sc_ref[...])
    o_ref[...] = x + y.reshape(seqs, steps, d)
    nctx_ref[...] = ext_ref[:, steps + 1:steps + POOL_MAXW, :]


def _pool_sample(x, state_pool, layer, pool_consts):
    bs, steps, d = x.shape
    seqs = POOL_SEQS
    assert bs % seqs == 0 and steps == V7X_SUBLANES
    resident = 2 * seqs * (2 * steps + 2 * POOL_MAXW) * d * 4 + seqs * (steps + POOL_MAXW) * d * 4
    return pl.pallas_call(
        functools.partial(_pool_sample_body, seqs=seqs, steps=steps),
        out_shape=(jax.ShapeDtypeStruct((bs, steps, d), F32),
                   jax.ShapeDtypeStruct((bs, POOL_BUF, d), F32)),
        grid=(bs // seqs,),
        in_specs=[pl.BlockSpec((seqs, steps, d), lambda i: (i, 0, 0)),
                  pl.BlockSpec((None, seqs, POOL_BUF, d), lambda i: (layer, i, 0, 0))]
        + [_const_spec(a.shape) for a in pool_consts],
        out_specs=(pl.BlockSpec((seqs, steps, d), lambda i: (i, 0, 0)),
                   pl.BlockSpec((seqs, POOL_BUF, d), lambda i: (i, 0, 0))),
        scratch_shapes=[pltpu.VMEM((seqs, steps + POOL_MAXW, d), F32)],
        compiler_params=pltpu.CompilerParams(
            dimension_semantics=("arbitrary",), vmem_limit_bytes=_vmem_limit(resident)),
        name="pool_sample",
    )(x, state_pool, *pool_consts)


def _gla_tables(rows, seg, score_rows):
    levels = int(np.log2(seg))
    assert 2 ** levels == seg and rows % score_rows == 0 and score_rows % seg == 0
    idx = np.arange(rows)
    tri = (idx[None, :] >= (idx[:, None] // seg) * seg) & (idx[None, :] <= idx[:, None])
    row = np.arange(score_rows)[:, None]
    col = np.arange(score_rows)[None, :]
    masks = [row == col]
    for l in range(1, levels + 1):
        s = 2 ** l
        mid = (row // s) * s + s // 2 - 1
        masks.append((row // s == col // s) & (row > mid) & (col <= mid))
    masks = np.stack(masks, axis=0).astype(np.float32)
    return jnp.asarray(tri.astype(np.float32), BF16), jnp.asarray(masks, F32), levels


def _gla_front(x, g_ref, wqkvr_ref, wa_ref, wa2_ref, ba_ref, tri_ref):
    dk = wa2_ref.shape[1]
    dkh = dk // GLA_HEADS
    h = _rmsnorm(x, g_ref[...]).astype(BF16)
    a_low = _dot(h, wa_ref[...])
    xg = _dot(a_low.astype(BF16), wa2_ref[...]) + ba_ref[...]
    la = -(jnp.maximum(-xg, 0.0) + jnp.log(1.0 + jnp.exp(-jnp.abs(xg)))) * (1.0 / GLA_GATE_TEMP)
    hi, mid, lo = _split3(la)
    tri = tri_ref[...]
    b = _dot(tri, hi) + _dot(tri, mid) + _dot(tri, lo)
    proj = _dot(h, wqkvr_ref[...])
    q = proj[:, 0:dk] * (dkh ** -0.5)
    k = proj[:, dk:2 * dk]
    v = proj[:, 2 * dk:4 * dk]
    r = proj[:, 4 * dk:6 * dk]
    return q, k, v, r, b


def _gla_level_operands(q16, k16, b, levels):
    rows, dk = b.shape
    row = lax.broadcasted_iota(jnp.int32, (rows, 1), 0)
    sub = lax.broadcasted_iota(jnp.int32, (1, V7X_SUBLANES, 1), 1)
    b = b * LOG2_E
    ops = []
    for l in range(1, levels + 1):
        s = 2 ** l
        if s >= V7X_SUBLANES:
            b3 = b.reshape(rows // s, s, dk)
            m = b3[:, s // 2 - 1:s // 2, :]
        else:
            b3 = b.reshape(rows // V7X_SUBLANES, V7X_SUBLANES, dk)
            picks = [b3[:, j + s // 2 - 1:j + s // 2, :] for j in range(0, V7X_SUBLANES, s)]
            m = picks[-1]
            for idx in range(len(picks) - 2, -1, -1):
                m = jnp.where(sub < (idx + 1) * s, picks[idx], m)
        e = jnp.exp2(-jnp.abs(b3 - m)).reshape(rows, dk).astype(BF16)
        second = ((row >> (l - 1)) & 1) == 1
        ops.append(jnp.where(second, q16, k16) * e)
    return ops


def _gla_intra(qb16_h, kb16_h, ops, masks_ref, rs, hs):
    a = _dot_nt(qb16_h, kb16_h) * masks_ref[0]
    for l, t in enumerate(ops, start=1):
        th = t[rs, hs]
        a = a + _dot_nt(th, th) * masks_ref[l]
    return a


def _lane_column(row_vec):
    return jnp.transpose(row_vec, (1, 0))


def _gla_out(x, o, r, gn_ref, wo_ref):
    dv = o.shape[-1]
    dvh = dv // GLA_HEADS
    normed = []
    for hd in range(GLA_HEADS):
        oh = o[:, hd * dvh:(hd + 1) * dvh]
        normed.append(oh * lax.rsqrt(jnp.mean(oh * oh, axis=-1, keepdims=True) + EPS))
    o = jnp.concatenate(normed, axis=-1) * gn_ref[...]
    o = (o * _silu(r)).astype(BF16)
    return x + _dot(o, wo_ref[...])


def _gla_prompt_math(x, states, g_ref, wqkvr_ref, wa_ref, wa2_ref, ba_ref, gn_ref, wo_ref,
                     tri_ref, masks_ref, *, levels, chunk):
    rows = x.shape[0]
    q, k, v, r, b = _gla_front(x, g_ref, wqkvr_ref, wa_ref, wa2_ref, ba_ref, tri_ref)
    dkh = q.shape[-1] // GLA_HEADS
    dvh = v.shape[-1] // GLA_HEADS
    q16 = q.astype(BF16)
    k16 = k.astype(BF16)
    ops = _gla_level_operands(q16, k16, b, levels)
    qb = (q * jnp.exp(b)).astype(BF16)
    vb = v.astype(BF16)
    states = list(states)
    out_rows = []
    for ci in range(rows // chunk):
        rs = slice(ci * chunk, (ci + 1) * chunk)
        b_c = b[rs]
        b_last = b_c[chunk - 1:chunk, :]
        kend = (k[rs] * jnp.exp(b_last - b_c)).astype(BF16)
        outs = []
        for hd in range(GLA_HEADS):
            hs = slice(hd * dkh, (hd + 1) * dkh)
            vs = slice(hd * dvh, (hd + 1) * dvh)
            s_old = states[hd]
            a = _gla_intra(q16[rs, hs], k16[rs, hs], ops, masks_ref, rs, hs)
            outs.append(_dot(a.astype(BF16), vb[rs, vs]) + _dot(qb[rs, hs], s_old.astype(BF16)))
            dec = jnp.exp(_lane_column(b_last[:, hs]))
            states[hd] = dec * s_old + _dot_tn(kend[:, hs], vb[rs, vs])
        out_rows.append(jnp.concatenate(outs, axis=-1))
    return _gla_out(x, jnp.concatenate(out_rows, axis=0), r, gn_ref, wo_ref), states


def _gla_sample_body(x_ref, s0_ref, g_ref, wqkvr_ref, wa_ref, wa2_ref, ba_ref, gn_ref, wo_ref,
                     tri_ref, masks_ref, o_ref, sout_ref, *, levels, seqs, steps):
    x = x_ref[...]
    rows = x.shape[0]
    q, k, v, r, b = _gla_front(x, g_ref, wqkvr_ref, wa_ref, wa2_ref, ba_ref, tri_ref)
    dk = q.shape[-1]
    dkh = dk // GLA_HEADS
    dvh = v.shape[-1] // GLA_HEADS
    q16 = q.astype(BF16)
    k16 = k.astype(BF16)
    ops = _gla_level_operands(q16, k16, b, levels)
    b3 = b.reshape(seqs, steps, dk)
    b_last = b3[:, steps - 1:steps, :]
    qb = (q * jnp.exp(b)).astype(BF16)
    kend = (k.reshape(seqs, steps, dk) * jnp.exp(b_last - b3)).reshape(rows, dk).astype(BF16)
    vb = v.astype(BF16)
    all_rows = slice(0, rows)
    outs = []
    for hd in range(GLA_HEADS):
        hs = slice(hd * dkh, (hd + 1) * dkh)
        vs = slice(hd * dvh, (hd + 1) * dvh)
        a = _gla_intra(q16[:, hs], k16[:, hs], ops, masks_ref, all_rows, hs)
        o_h = _dot(a.astype(BF16), vb[:, vs])
        o_state = []
        for s in range(seqs):
            rs = slice(s * steps, (s + 1) * steps)
            s_old = s0_ref[s, hd]
            o_state.append(_dot(qb[rs, hs], s_old.astype(BF16)))
            dec = jnp.exp(_lane_column(b[(s + 1) * steps - 1:(s + 1) * steps, hs]))
            sout_ref[s, hd] = dec * s_old + _dot_tn(kend[rs, hs], vb[rs, vs])
        outs.append(o_h + jnp.concatenate(o_state, axis=0))
    o_ref[...] = _gla_out(x, jnp.concatenate(outs, axis=-1), r, gn_ref, wo_ref)


def _gla_sample(x, state_gla, layer, gla_consts, *, steps):
    n, d = x.shape
    seqs = GLA_SEQS
    rows = seqs * steps
    bs = n // steps
    assert bs % seqs == 0
    wqkvr, wa2, wo = gla_consts[1], gla_consts[3], gla_consts[6]
    dk = wa2.shape[1]
    dkh, dvh = dk // GLA_HEADS, wo.shape[0] // GLA_HEADS
    tri, masks, levels = _gla_tables(rows, steps, rows)
    consts = tuple(gla_consts) + (tri, masks)
    state_block = seqs * GLA_HEADS * dkh * dvh * 4
    resident = _nbytes(consts) + 4 * rows * d * 4 + 4 * state_block
    resident += rows * (wqkvr.shape[1] * 4 + (levels + 4) * dk * 4)
    return pl.pallas_call(
        functools.partial(_gla_sample_body, levels=levels, seqs=seqs, steps=steps),
        out_shape=(jax.ShapeDtypeStruct((n, d), F32),
                   jax.ShapeDtypeStruct((bs, GLA_HEADS, dkh, dvh), F32)),
        grid=(bs // seqs,),
        in_specs=[pl.BlockSpec((rows, d), lambda i: (i, 0)),
                  pl.BlockSpec((None, seqs, GLA_HEADS, dkh, dvh), lambda i: (layer, i, 0, 0, 0))]
        + [_const_spec(a.shape) for a in consts],
        out_specs=(pl.BlockSpec((rows, d), lambda i: (i, 0)),
                   pl.BlockSpec((seqs, GLA_HEADS, dkh, dvh), lambda i: (i, 0, 0, 0))),
        compiler_params=pltpu.CompilerParams(
            dimension_semantics=("arbitrary",), vmem_limit_bytes=_vmem_limit(resident)),
        name="gla_sample",
    )(x, state_gla, *consts)


def _pool_ffn_body(x_ref, gm_ref, wp_ref, sc_ref, gfn_ref, w1_ref, w3_ref, w2_ref, gf_ref,
                   o_ref, ctx_ref, ext_ref, z_ref, res_ref, *, rows, n_tiles, n_total, final):
    s = pl.program_id(0)
    d = x_ref.shape[-1]
    t = jnp.minimum(s, n_total - 1) % n_tiles
    live = s < n_total

    @pl.when(s == 0)
    def _():
        z_ref[...] = jnp.zeros(z_ref.shape, F32)

    @pl.when(t == 0)
    def _():
        ext_ref[0:POOL_MAXW, :] = jnp.zeros((POOL_MAXW, d), F32)

    res_ref[...] = z_ref[...]
    o_ref[...] = _ffn_math(res_ref[...], gfn_ref, w1_ref, w3_ref, w2_ref, gf_ref, final)

    x = x_ref[...]
    h = _rmsnorm(x, gm_ref[...])
    ext_ref[POOL_MAXW:POOL_MAXW + rows, :] = h
    pos = t * rows + lax.broadcasted_iota(jnp.int32, (rows, 1), 0)

    def ext_read(k, cs):
        return ext_ref[POOL_MAXW - k:POOL_MAXW - k + rows, cs]

    z_ref[...] = x + _pool_groups(ext_read, h, pos, wp_ref, sc_ref[...])

    @pl.when(live)
    def _():
        ext_ref[0:POOL_MAXW, :] = ext_ref[rows:rows + POOL_MAXW, :]

    @pl.when(jnp.logical_and(live, t == n_tiles - 1))
    def _():
        ctx_ref[...] = ext_ref[rows + 1:rows + POOL_MAXW, :]


def _gla_ffn_body(x_ref, gm_ref, wqkvr_ref, wa_ref, wa2_ref, ba_ref, gn_ref, wo_ref, tri_ref, masks_ref,
                  gfn_ref, w1_ref, w3_ref, w2_ref, gf_ref,
                  o_ref, sout_ref, s_ref, z_ref, res_ref, *, levels, chunk, n_tiles, n_total, final):
    s = pl.program_id(0)
    t = jnp.minimum(s, n_total - 1) % n_tiles
    live = s < n_total

    @pl.when(s == 0)
    def _():
        z_ref[...] = jnp.zeros(z_ref.shape, F32)

    @pl.when(t == 0)
    def _():
        s_ref[...] = jnp.zeros(s_ref.shape, F32)

    res_ref[...] = z_ref[...]
    o_ref[...] = _ffn_math(res_ref[...], gfn_ref, w1_ref, w3_ref, w2_ref, gf_ref, final)

    states = [s_ref[hd] for hd in range(GLA_HEADS)]
    z, states = _gla_prompt_math(x_ref[...], states, gm_ref, wqkvr_ref, wa_ref, wa2_ref, ba_ref, gn_ref,
                                 wo_ref, tri_ref, masks_ref, levels=levels, chunk=chunk)
    z_ref[...] = z

    @pl.when(live)
    def _():
        for hd in range(GLA_HEADS):
            s_ref[hd] = states[hd]

    @pl.when(jnp.logical_and(live, t == n_tiles - 1))
    def _():
        for hd in range(GLA_HEADS):
            sout_ref[hd] = states[hd]


def _prompt_layer(kind, x, mixer_consts, ffn_consts, *, batch, seq, final):
    n, d = x.shape
    rows = PROMPT_ROWS
    n_tiles = seq // rows
    n_total = batch * n_tiles
    assert seq % rows == 0 and rows >= POOL_MAXW and rows % GLA_CHUNK == 0
    dff = ffn_consts[1].shape[1]
    resident = _nbytes(mixer_consts) + _nbytes(ffn_consts) + 6 * rows * d * 4 + 3 * rows * dff * 4
    x_spec = pl.BlockSpec((rows, d), lambda s: (jnp.minimum(s, n_total - 1), 0))
    o_spec = pl.BlockSpec((rows, d), lambda s: (jnp.maximum(s - 1, 0), 0))
    seq_of = lambda s: jnp.minimum(s, n_total - 1) // n_tiles
    carry = [pltpu.VMEM((rows, d), F32), pltpu.VMEM((rows, d), F32)]
    if kind == "pool":
        body = functools.partial(_pool_ffn_body, rows=rows, n_tiles=n_tiles, n_total=n_total, final=final)
        state_shape = (batch, POOL_BUF, d)
        state_spec = pl.BlockSpec((None, POOL_BUF, d), lambda s: (seq_of(s), 0, 0))
        scratch = [pltpu.VMEM((rows + POOL_MAXW, d), F32)] + carry
        resident += (rows + POOL_MAXW) * d * 4
        consts = tuple(mixer_consts) + tuple(ffn_consts)
    else:
        wqkvr, wa2, wo = mixer_consts[1], mixer_consts[3], mixer_consts[6]
        dk = wa2.shape[1]
        dkh, dvh = dk // GLA_HEADS, wo.shape[0] // GLA_HEADS
        tri, masks, levels = _gla_tables(rows, GLA_CHUNK, GLA_CHUNK)
        body = functools.partial(_gla_ffn_body, levels=levels, chunk=GLA_CHUNK, n_tiles=n_tiles,
                                 n_total=n_total, final=final)
        state_shape = (batch, GLA_HEADS, dkh, dvh)
        state_spec = pl.BlockSpec((None, GLA_HEADS, dkh, dvh), lambda s: (seq_of(s), 0, 0, 0))
        scratch = [pltpu.VMEM((GLA_HEADS, dkh, dvh), F32)] + carry
        resident += _nbytes((tri, masks)) + 3 * GLA_HEADS * dkh * dvh * 4
        resident += rows * (wqkvr.shape[1] * 4 + (levels + 4) * dk * 4)
        consts = tuple(mixer_consts) + (tri, masks) + tuple(ffn_consts)
    return pl.pallas_call(
        body,
        out_shape=(jax.ShapeDtypeStruct((n, d), F32), jax.ShapeDtypeStruct(state_shape, F32)),
        grid=(n_total + 1,),
        in_specs=[x_spec] + [_const_spec(a.shape) for a in consts],
        out_specs=(o_spec, state_spec),
        scratch_shapes=scratch,
        compiler_params=pltpu.CompilerParams(
            dimension_semantics=("arbitrary",), vmem_limit_bytes=_vmem_limit(resident)),
        name=kind + ("_ffn_final" if final else "_ffn") + "_prompt",
    )(x, *consts)


def kernel(x_prompt, x_sample, state_pool, state_gla, norm_mix, norm_ffn, norm_final, pool_w, pool_scale,
           gla_w_in, gla_w_a2, gla_b_a, gla_norm, gla_w_o, ffn_w1, ffn_w3, ffn_w2):
    batch, seq, d = x_prompt.shape
    bs, steps, _ = x_sample.shape
    depth = norm_mix.shape[0]
    dk = gla_w_a2.shape[-1]
    dv = gla_w_o.shape[1]
    rank = gla_w_a2.shape[1]
    n_qkvr = 2 * dk + 2 * dv

    xp = x_prompt.reshape(batch * seq, d)
    xs = x_sample.reshape(bs * steps, d)
    row = lambda a: a.reshape(1, -1)
    gf = row(norm_final)

    new_pool_p, new_pool_s, new_gla_p, new_gla_s = [], [], [], []
    for i in range(depth):
        j = i // 2
        final = i == depth - 1
        g = row(norm_mix[i])
        ffn_consts = (row(norm_ffn[i]), ffn_w1[i].astype(BF16), ffn_w3[i].astype(BF16),
                      ffn_w2[i].astype(BF16), gf)
        if i % 2 == 0:
            pool_consts = (g, pool_w[j].astype(BF16), row(pool_scale[j]))
            xp, c = _prompt_layer("pool", xp, pool_consts, ffn_consts, batch=batch, seq=seq, final=final)
            new_pool_p.append(c)
            xs3, c = _pool_sample(xs.reshape(bs, steps, d), state_pool, j, pool_consts)
            xs = xs3.reshape(bs * steps, d)
            new_pool_s.append(c)
        else:
            w_in = gla_w_in[j]
            wa = jnp.pad(w_in[:, n_qkvr:], ((0, 0), (0, V7X_LANES - rank))).astype(BF16)
            wa2 = jnp.pad(gla_w_a2[j], ((0, V7X_LANES - rank), (0, 0))).astype(BF16)
            gla_consts = (g, w_in[:, :n_qkvr].astype(BF16), wa, wa2, row(gla_b_a[j]), row(gla_norm[j]),
                          gla_w_o[j].astype(BF16))
            xp, s = _prompt_layer("gla", xp, gla_consts, ffn_consts, batch=batch, seq=seq, final=final)
            new_gla_p.append(s)
            xs, s = _gla_sample(xs, state_gla, j, gla_consts, steps=steps)
            new_gla_s.append(s)
        xs = _ffn(xs, ffn_consts, final=final)

    return (xp.reshape(batch, seq, d), xs.reshape(bs, steps, d),
            jnp.stack(new_pool_p), jnp.stack(new_gla_p),
            jnp.stack(new_pool_s), jnp.stack(new_gla_s))
```

```python
import functools

import numpy as np
import jax
import jax.numpy as jnp
from jax import lax
from jax.experimental import pallas as pl
from jax.experimental.pallas import tpu as pltpu

F32 = jnp.float32
BF16 = jnp.bfloat16

PAST_LEN = 16384
POOL_WINDOWS = (2, 4, 8, 16)
POOL_MAXW = max(POOL_WINDOWS)
POOL_BUF = POOL_MAXW - 1
GLA_HEADS = 4
GLA_GATE_RANK = 16
GLA_GATE_TEMP = 16.0
EPS = 1e-6
LOG2_E = 1.4426950408889634

V7X_LANES = 128
V7X_SUBLANES = 8
V7X_VMEM_BYTES = 64 * 1024 * 1024

PROMPT_ROWS = 256
GLA_CHUNK = 128
FFN_ROWS = 512
POOL_SEQS = 16
GLA_SEQS = 8
FFN_COL_CHUNKS = (1024, 1024, 768)
GLA_LEVEL_COST = 300
GLA_SCORE_COST = 300
GLA_FINISH_COST = 200
GLA_SCORE_LEAD = 2
FFN_STEP_COST = 8648
GLA_STEP_COST = 5882 + 7 * GLA_LEVEL_COST + 8 * (GLA_SCORE_COST + GLA_FINISH_COST)
POOL_STEP_COST = 2220


def _vmem_limit(resident_bytes):
    return int(min(V7X_VMEM_BYTES - (4 << 20), 2 * resident_bytes + (16 << 20)))


def _nbytes(arrays):
    return sum(a.size * a.dtype.itemsize for a in arrays)


def _const_spec(shape):
    nd = len(shape)
    return pl.BlockSpec(shape, lambda *_: (0,) * nd, pipeline_mode=pl.Buffered(1))


def _rmsnorm(x, g):
    ms = jnp.mean(x * x, axis=-1, keepdims=True)
    return x * lax.rsqrt(ms + EPS) * g


def _silu(x):
    hx = 0.5 * x
    return hx + hx * jnp.tanh(hx)


def _dot(a, b):
    return jnp.dot(a, b, preferred_element_type=F32)


def _dot_nt(a, b):
    return lax.dot_general(a, b, (((1,), (1,)), ((), ())), preferred_element_type=F32)


def _dot_tn(a, b):
    return lax.dot_general(a, b, (((0,), (0,)), ((), ())), preferred_element_type=F32)


def _split3(x):
    hi = x.astype(BF16)
    r1 = x - hi.astype(F32)
    mid = r1.astype(BF16)
    lo = (r1 - mid.astype(F32)).astype(BF16)
    return hi, mid, lo


def _ffn_math(x, g_ref, w1_ref, w3_ref, w2_ref, gf_ref, final):
    h = _rmsnorm(x, g_ref[...]).astype(BF16)
    u1 = _dot(h, w1_ref[...])
    u3 = _dot(h, w3_ref[...])
    act = (_silu(u1) * u3).astype(BF16)
    out = x + _dot(act, w2_ref[...])
    if final:
        out = _rmsnorm(out, gf_ref[...])
    return out


def _ffn_body(x_ref, g_ref, w1_ref, w3_ref, w2_ref, gf_ref, o_ref, *, final):
    o_ref[...] = _ffn_math(x_ref[...], g_ref, w1_ref, w3_ref, w2_ref, gf_ref, final)


def _ffn(x, ffn_consts, *, final):
    n, d = x.shape
    dff = ffn_consts[1].shape[1]
    rows = min(FFN_ROWS, n)
    assert n % rows == 0
    resident = _nbytes(ffn_consts) + 4 * rows * d * 4 + 3 * rows * dff * 4
    return pl.pallas_call(
        functools.partial(_ffn_body, final=final),
        out_shape=jax.ShapeDtypeStruct((n, d), F32),
        grid=(n // rows,),
        in_specs=[pl.BlockSpec((rows, d), lambda i: (i, 0))] + [_const_spec(a.shape) for a in ffn_consts],
        out_specs=pl.BlockSpec((rows, d), lambda i: (i, 0)),
        compiler_params=pltpu.CompilerParams(
            dimension_semantics=("arbitrary",), vmem_limit_bytes=_vmem_limit(resident)),
        name="ffn_final" if final else "ffn",
    )(x, *ffn_consts)


def _pool_group_steps(ext_read, h, pos, w_ref, sc, result):
    gw = h.shape[-1] // len(POOL_WINDOWS)
    ys = []
    for g, w in enumerate(POOL_WINDOWS):
        cs = slice(g * gw, (g + 1) * gw)
        hg = h[:, cs]
        s = hg
        for k in range(1, w):
            s = s + ext_read(k, cs)
        cnt = jnp.minimum(pos + 1, w).astype(F32)
        diff = (s / cnt - hg).astype(BF16)
        ys.append(_dot(diff, w_ref[g]))
        yield 100 + 70 * (w - 1)
    result["y"] = jnp.concatenate(ys, axis=-1) * sc


def _pool_groups(ext_read, h, pos, w_ref, sc):
    result = {}
    for _ in _pool_group_steps(ext_read, h, pos, w_ref, sc, result):
        pass
    return result["y"]


def _pool_sample_body(x_ref, ctx_ref, g_ref, w_ref, sc_ref, o_ref, nctx_ref, ext_ref, *, seqs, steps):
    d = x_ref.shape[-1]
    x = x_ref[...]
    h = _rmsnorm(x, g_ref[...])
    ext_ref[:, 0:1, :] = jnp.zeros((seqs, 1, d), F32)
    ext_ref[:, 1:POOL_MAXW, :] = ctx_ref[...]
    ext_ref[:, POOL_MAXW:POOL_MAXW + steps, :] = h
    pos = PAST_LEN + lax.broadcasted_iota(jnp.int32, (seqs * steps, 1), 0) % steps
    h2 = h.reshape(seqs * steps, d)

    def ext_read(k, cs):
        return ext_ref[:, POOL_MAXW - k:POOL_MAXW - k + steps, cs].reshape(seqs * steps, cs.stop - cs.start)

    y = _pool_groups(ext_read, h2, pos, w_ref, sc_ref[...])
    o_ref[...] = x + y.reshape(seqs, steps, d)
    nctx_ref[...] = ext_ref[:, steps + 1:steps + POOL_MAXW, :]


def _pool_sample(x, state_pool, layer, pool_consts):
    bs, steps, d = x.shape
    seqs = POOL_SEQS
    assert bs % seqs == 0 and steps == V7X_SUBLANES
    resident = 2 * seqs * (2 * steps + 2 * POOL_MAXW) * d * 4 + seqs * (steps + POOL_MAXW) * d * 4
    return pl.pallas_call(
        functools.partial(_pool_sample_body, seqs=seqs, steps=steps),
        out_shape=(jax.ShapeDtypeStruct((bs, steps, d), F32),
                   jax.ShapeDtypeStruct((bs, POOL_BUF, d), F32)),
        grid=(bs // seqs,),
        in_specs=[pl.BlockSpec((seqs, steps, d), lambda i: (i, 0, 0)),
                  pl.BlockSpec((None, seqs, POOL_BUF, d), lambda i: (layer, i, 0, 0))]
        + [_const_spec(a.shape) for a in pool_consts],
        out_specs=(pl.BlockSpec((seqs, steps, d), lambda i: (i, 0, 0)),
                   pl.BlockSpec((seqs, POOL_BUF, d), lambda i: (i, 0, 0))),
        scratch_shapes=[pltpu.VMEM((seqs, steps + POOL_MAXW, d), F32)],
        compiler_params=pltpu.CompilerParams(
            dimension_semantics=("arbitrary",), vmem_limit_bytes=_vmem_limit(resident)),
        name="pool_sample",
    )(x, state_pool, *pool_consts)


def _gla_tables(rows, seg, score_rows):
    levels = int(np.log2(seg))
    assert 2 ** levels == seg and rows % score_rows == 0 and score_rows % seg == 0
    idx = np.arange(rows)
    tri = (idx[None, :] >= (idx[:, None] // seg) * seg) & (idx[None, :] <= idx[:, None])
    row = np.arange(score_rows)[:, None]
    col = np.arange(score_rows)[None, :]
    masks = [row == col]
    for l in range(1, levels + 1):
        s = 2 ** l
        mid = (row // s) * s + s // 2 - 1
        masks.append((row // s == col // s) & (row > mid) & (col <= mid))
    masks = np.stack(masks, axis=0).astype(np.float32)
    return jnp.asarray(tri.astype(np.float32), BF16), jnp.asarray(masks, F32), levels


def _gla_front(x, g_ref, wqkvr_ref, wa_ref, wa2_ref, ba_ref, tri_ref):
    dk = wa2_ref.shape[1]
    dkh = dk // GLA_HEADS
    h = _rmsnorm(x, g_ref[...]).astype(BF16)
    a_low = _dot(h, wa_ref[...])
    xg = _dot(a_low.astype(BF16), wa2_ref[...]) + ba_ref[...]
    la = -(jnp.maximum(-xg, 0.0) + jnp.log(1.0 + jnp.exp(-jnp.abs(xg)))) * (1.0 / GLA_GATE_TEMP)
    hi, mid, lo = _split3(la)
    tri = tri_ref[...]
    b = _dot(tri, hi) + _dot(tri, mid) + _dot(tri, lo)
    proj = _dot(h, wqkvr_ref[...])
    q = proj[:, 0:dk] * (dkh ** -0.5)
    k = proj[:, dk:2 * dk]
    v = proj[:, 2 * dk:4 * dk]
    r = proj[:, 4 * dk:6 * dk]
    return q, k, v, r, b


def _gla_level_operand_steps(q16, k16, b, levels, ops):
    rows, dk = b.shape
    row = lax.broadcasted_iota(jnp.int32, (rows, 1), 0)
    sub = lax.broadcasted_iota(jnp.int32, (1, V7X_SUBLANES, 1), 1)
    b = b * LOG2_E
    for l in range(1, levels + 1):
        s = 2 ** l
        if s >= V7X_SUBLANES:
            b3 = b.reshape(rows // s, s, dk)
            m = b3[:, s // 2 - 1:s // 2, :]
        else:
            b3 = b.reshape(rows // V7X_SUBLANES, V7X_SUBLANES, dk)
            picks = [b3[:, j + s // 2 - 1:j + s // 2, :] for j in range(0, V7X_SUBLANES, s)]
            m = picks[-1]
            for idx in range(len(picks) - 2, -1, -1):
                m = jnp.where(sub < (idx + 1) * s, picks[idx], m)
        e = jnp.exp2(-jnp.abs(b3 - m)).reshape(rows, dk).astype(BF16)
        second = ((row >> (l - 1)) & 1) == 1
        ops.append(jnp.where(second, q16, k16) * e)
        yield GLA_LEVEL_COST


def _gla_level_operands(q16, k16, b, levels):
    ops = []
    for _ in _gla_level_operand_steps(q16, k16, b, levels, ops):
        pass
    return ops


def _gla_intra(qb16_h, kb16_h, ops, masks_ref, rs, hs):
    a = _dot_nt(qb16_h, kb16_h) * masks_ref[0]
    for l, t in enumerate(ops, start=1):
        th = t[rs, hs]
        a = a + _dot_nt(th, th) * masks_ref[l]
    return a


def _lane_column(row_vec):
    return jnp.transpose(row_vec, (1, 0))


def _gla_out(x, o, r, gn_ref, wo_ref):
    dv = o.shape[-1]
    dvh = dv // GLA_HEADS
    normed = []
    for hd in range(GLA_HEADS):
        oh = o[:, hd * dvh:(hd + 1) * dvh]
        normed.append(oh * lax.rsqrt(jnp.mean(oh * oh, axis=-1, keepdims=True) + EPS))
    o = jnp.concatenate(normed, axis=-1) * gn_ref[...]
    o = (o * _silu(r)).astype(BF16)
    return x + _dot(o, wo_ref[...])


def _gla_prompt_steps(x, states, g_ref, wqkvr_ref, wa_ref, wa2_ref, ba_ref, gn_ref, wo_ref,
                      tri_ref, masks_ref, result, *, levels, chunk):
    rows = x.shape[0]
    dk = wa2_ref.shape[1]
    dkh = dk // GLA_HEADS
    h = _rmsnorm(x, g_ref[...]).astype(BF16)
    a_low = _dot(h, wa_ref[...])
    yield 150
    q = _dot(h, wqkvr_ref[:, 0:dk]) * (dkh ** -0.5)
    q16 = q.astype(BF16)
    yield 512
    xg = _dot(a_low.astype(BF16), wa2_ref[...]) + ba_ref[...]
    yield 150
    k = _dot(h, wqkvr_ref[:, dk:2 * dk])
    k16 = k.astype(BF16)
    yield 512
    la = -(jnp.maximum(-xg, 0.0) + jnp.log(1.0 + jnp.exp(-jnp.abs(xg)))) * (1.0 / GLA_GATE_TEMP)
    hi, mid, lo = _split3(la)
    yield 400
    vb = _dot(h, wqkvr_ref[:, 2 * dk:4 * dk]).astype(BF16)
    yield 1024
    dvh = vb.shape[-1] // GLA_HEADS
    tri = tri_ref[...]
    b = _dot(tri, hi) + _dot(tri, mid) + _dot(tri, lo)
    yield 350
    r = _dot(h, wqkvr_ref[:, 4 * dk:6 * dk])
    yield 1024
    ops = []
    yield from _gla_level_operand_steps(q16, k16, b, levels, ops)
    qb = (q * jnp.exp(b)).astype(BF16)
    yield 60

    states = list(states)
    pieces = [(ci, hd) for ci in range(rows // chunk) for hd in range(GLA_HEADS)]
    kends = {}
    outs = {}

    def scores(ci, hd):
        rs = slice(ci * chunk, (ci + 1) * chunk)
        hs = slice(hd * dkh, (hd + 1) * dkh)
        return _gla_intra(q16[rs, hs], k16[rs, hs], ops, masks_ref, rs, hs).astype(BF16)

    def finish(ci, hd, a16):
        rs = slice(ci * chunk, (ci + 1) * chunk)
        hs = slice(hd * dkh, (hd + 1) * dkh)
        vs = slice(hd * dvh, (hd + 1) * dvh)
        b_c = b[rs]
        b_last = b_c[chunk - 1:chunk, :]
        if ci not in kends:
            kends[ci] = (k[rs] * jnp.exp(b_last - b_c)).astype(BF16)
        s_old = states[hd]
        outs[ci, hd] = _dot(a16, vb[rs, vs]) + _dot(qb[rs, hs], s_old.astype(BF16))
        dec = jnp.exp(_lane_column(b_last[:, hs]))
        states[hd] = dec * s_old + _dot_tn(kends[ci][:, hs], vb[rs, vs])

    pending = []
    for piece in pieces:
        pending.append((*piece, scores(*piece)))
        yield GLA_SCORE_COST
        if len(pending) > GLA_SCORE_LEAD:
            finish(*pending.pop(0))
            yield GLA_FINISH_COST
    while pending:
        finish(*pending.pop(0))
        yield GLA_FINISH_COST
    o = jnp.concatenate(
        [jnp.concatenate([outs[ci, hd] for hd in range(GLA_HEADS)], axis=-1) for ci in range(rows // chunk)],
        axis=0)
    result["z"] = _gla_out(x, o, r, gn_ref, wo_ref)
    result["states"] = states
    yield 1700


def _ffn_steps(x, g_ref, w1_ref, w3_ref, w2_ref, gf_ref, result, *, final, col_chunks):
    rows = x.shape[0]
    h = _rmsnorm(x, g_ref[...]).astype(BF16)
    yield 200
    acc = x
    lo = 0
    pending = None
    for cols in col_chunks:
        cs = slice(lo, lo + cols)
        lo += cols
        u1 = _dot(h, w1_ref[:, cs])
        yield rows * cols // 256
        u3 = _dot(h, w3_ref[:, cs])
        yield rows * cols // 256
        if pending is not None:
            acc = acc + _dot(pending[0], w2_ref[pending[1], :])
            yield rows * (pending[1].stop - pending[1].start) // 256
        pending = ((_silu(u1) * u3).astype(BF16), cs)
    acc = acc + _dot(pending[0], w2_ref[pending[1], :])
    yield rows * (pending[1].stop - pending[1].start) // 256
    if final:
        acc = _rmsnorm(acc, gf_ref[...])
    result["out"] = acc


def _interleave(*pairs):
    gens = [p[0] for p in pairs]
    totals = [float(p[1]) for p in pairs]
    done = [0.0] * len(gens)
    live = list(range(len(gens)))
    while live:
        i = min(live, key=lambda j: done[j] / totals[j])
        try:
            done[i] += next(gens[i])
        except StopIteration:
            live.remove(i)


def _gla_sample_body(x_ref, s0_ref, g_ref, wqkvr_ref, wa_ref, wa2_ref, ba_ref, gn_ref, wo_ref,
                     tri_ref, masks_ref, o_ref, sout_ref, *, levels, seqs, steps):
    x = x_ref[...]
    rows = x.shape[0]
    q, k, v, r, b = _gla_front(x, g_ref, wqkvr_ref, wa_ref, wa2_ref, ba_ref, tri_ref)
    dk = q.shape[-1]
    dkh = dk // GLA_HEADS
    dvh = v.shape[-1] // GLA_HEADS
    q16 = q.astype(BF16)
    k16 = k.astype(BF16)
    ops = _gla_level_operands(q16, k16, b, levels)
    b3 = b.reshape(seqs, steps, dk)
    b_last = b3[:, steps - 1:steps, :]
    qb = (q * jnp.exp(b)).astype(BF16)
    kend = (k.reshape(seqs, steps, dk) * jnp.exp(b_last - b3)).reshape(rows, dk).astype(BF16)
    vb = v.astype(BF16)
    all_rows = slice(0, rows)
    outs = []
    for hd in range(GLA_HEADS):
        hs = slice(hd * dkh, (hd + 1) * dkh)
        vs = slice(hd * dvh, (hd + 1) * dvh)
        a = _gla_intra(q16[:, hs], k16[:, hs], ops, masks_ref, all_rows, hs)
        o_h = _dot(a.astype(BF16), vb[:, vs])
        o_state = []
        for s in range(seqs):
            rs = slice(s * steps, (s + 1) * steps)
            s_old = s0_ref[s, hd]
            o_state.append(_dot(qb[rs, hs], s_old.astype(BF16)))
            dec = jnp.exp(_lane_column(b[(s + 1) * steps - 1:(s + 1) * steps, hs]))
            sout_ref[s, hd] = dec * s_old + _dot_tn(kend[rs, hs], vb[rs, vs])
        outs.append(o_h + jnp.concatenate(o_state, axis=0))
    o_ref[...] = _gla_out(x, jnp.concatenate(outs, axis=-1), r, gn_ref, wo_ref)


def _gla_sample(x, state_gla, layer, gla_consts, *, steps):
    n, d = x.shape
    seqs = GLA_SEQS
    rows = seqs * steps
    bs = n // steps
    assert bs % seqs == 0
    wqkvr, wa2, wo = gla_consts[1], gla_consts[3], gla_consts[6]
    dk = wa2.shape[1]
    dkh, dvh = dk // GLA_HEADS, wo.shape[0] // GLA_HEADS
    tri, masks, levels = _gla_tables(rows, steps, rows)
    consts = tuple(gla_consts) + (tri, masks)
    state_block = seqs * GLA_HEADS * dkh * dvh * 4
    resident = _nbytes(consts) + 4 * rows * d * 4 + 4 * state_block
    resident += rows * (wqkvr.shape[1] * 4 + (levels + 4) * dk * 4)
    return pl.pallas_call(
        functools.partial(_gla_sample_body, levels=levels, seqs=seqs, steps=steps),
        out_shape=(jax.ShapeDtypeStruct((n, d), F32),
                   jax.ShapeDtypeStruct((bs, GLA_HEADS, dkh, dvh), F32)),
        grid=(bs // seqs,),
        in_specs=[pl.BlockSpec((rows, d), lambda i: (i, 0)),
                  pl.BlockSpec((None, seqs, GLA_HEADS, dkh, dvh), lambda i: (layer, i, 0, 0, 0))]
        + [_const_spec(a.shape) for a in consts],
        out_specs=(pl.BlockSpec((rows, d), lambda i: (i, 0)),
                   pl.BlockSpec((seqs, GLA_HEADS, dkh, dvh), lambda i: (i, 0, 0, 0))),
        compiler_params=pltpu.CompilerParams(
            dimension_semantics=("arbitrary",), vmem_limit_bytes=_vmem_limit(resident)),
        name="gla_sample",
    )(x, state_gla, *consts)


def _pool_ffn_body(x_ref, gm_ref, wp_ref, sc_ref, gfn_ref, w1_ref, w3_ref, w2_ref, gf_ref,
                   o_ref, ctx_ref, ext_ref, z_ref, res_ref, *, rows, n_tiles, n_total, final):
    s = pl.program_id(0)
    d = x_ref.shape[-1]
    t = jnp.minimum(s, n_total - 1) % n_tiles
    live = s < n_total

    @pl.when(s == 0)
    def _():
        z_ref[...] = jnp.zeros(z_ref.shape, F32)

    @pl.when(t == 0)
    def _():
        ext_ref[0:POOL_MAXW, :] = jnp.zeros((POOL_MAXW, d), F32)

    res_ref[...] = z_ref[...]

    x = x_ref[...]
    h = _rmsnorm(x, gm_ref[...])
    ext_ref[POOL_MAXW:POOL_MAXW + rows, :] = h
    pos = t * rows + lax.broadcasted_iota(jnp.int32, (rows, 1), 0)

    def ext_read(k, cs):
        return ext_ref[POOL_MAXW - k:POOL_MAXW - k + rows, cs]

    mix, ffn = {}, {}
    _interleave(
        (_pool_group_steps(ext_read, h, pos, wp_ref, sc_ref[...], mix), POOL_STEP_COST),
        (_ffn_steps(res_ref[...], gfn_ref, w1_ref, w3_ref, w2_ref, gf_ref, ffn,
                    final=final, col_chunks=FFN_COL_CHUNKS), FFN_STEP_COST))
    z_ref[...] = x + mix["y"]
    o_ref[...] = ffn["out"]

    @pl.when(live)
    def _():
        ext_ref[0:POOL_MAXW, :] = ext_ref[rows:rows + POOL_MAXW, :]

    @pl.when(jnp.logical_and(live, t == n_tiles - 1))
    def _():
        ctx_ref[...] = ext_ref[rows + 1:rows + POOL_MAXW, :]


def _gla_ffn_body(x_ref, gm_ref, wqkvr_ref, wa_ref, wa2_ref, ba_ref, gn_ref, wo_ref, tri_ref, masks_ref,
                  gfn_ref, w1_ref, w3_ref, w2_ref, gf_ref,
                  o_ref, sout_ref, s_ref, z_ref, res_ref, *, levels, chunk, n_tiles, n_total, final):
    s = pl.program_id(0)
    t = jnp.minimum(s, n_total - 1) % n_tiles
    live = s < n_total

    @pl.when(s == 0)
    def _():
        z_ref[...] = jnp.zeros(z_ref.shape, F32)

    @pl.when(t == 0)
    def _():
        s_ref[...] = jnp.zeros(s_ref.shape, F32)

    res_ref[...] = z_ref[...]

    states = [s_ref[hd] for hd in range(GLA_HEADS)]
    mix, ffn = {}, {}
    _interleave(
        (_gla_prompt_steps(x_ref[...], states, gm_ref, wqkvr_ref, wa_ref, wa2_ref, ba_ref, gn_ref,
                           wo_ref, tri_ref, masks_ref, mix, levels=levels, chunk=chunk), GLA_STEP_COST),
        (_ffn_steps(res_ref[...], gfn_ref, w1_ref, w3_ref, w2_ref, gf_ref, ffn,
                    final=final, col_chunks=FFN_COL_CHUNKS), FFN_STEP_COST))
    states = mix["states"]
    z_ref[...] = mix["z"]
    o_ref[...] = ffn["out"]

    @pl.when(live)
    def _():
        for hd in range(GLA_HEADS):
            s_ref[hd] = states[hd]

    @pl.when(jnp.logical_and(live, t == n_tiles - 1))
    def _():
        for hd in range(GLA_HEADS):
            sout_ref[hd] = states[hd]


def _prompt_layer(kind, x, mixer_consts, ffn_consts, *, batch, seq, final):
    n, d = x.shape
    rows = PROMPT_ROWS
    n_tiles = seq // rows
    n_total = batch * n_tiles
    assert seq % rows == 0 and rows >= POOL_MAXW and rows % GLA_CHUNK == 0
    dff = ffn_consts[1].shape[1]
    resident = _nbytes(mixer_consts) + _nbytes(ffn_consts) + 6 * rows * d * 4 + 3 * rows * dff * 4
    x_spec = pl.BlockSpec((rows, d), lambda s: (jnp.minimum(s, n_total - 1), 0))
    o_spec = pl.BlockSpec((rows, d), lambda s: (jnp.maximum(s - 1, 0), 0))
    seq_of = lambda s: jnp.minimum(s, n_total - 1) // n_tiles
    carry = [pltpu.VMEM((rows, d), F32), pltpu.VMEM((rows, d), F32)]
    if kind == "pool":
        body = functools.partial(_pool_ffn_body, rows=rows, n_tiles=n_tiles, n_total=n_total, final=final)
        state_shape = (batch, POOL_BUF, d)
        state_spec = pl.BlockSpec((None, POOL_BUF, d), lambda s: (seq_of(s), 0, 0))
        scratch = [pltpu.VMEM((rows + POOL_MAXW, d), F32)] + carry
        resident += (rows + POOL_MAXW) * d * 4
        consts = tuple(mixer_consts) + tuple(ffn_consts)
    else:
        wqkvr, wa2, wo = mixer_consts[1], mixer_consts[3], mixer_consts[6]
        dk = wa2.shape[1]
        dkh, dvh = dk // GLA_HEADS, wo.shape[0] // GLA_HEADS
        tri, masks, levels = _gla_tables(rows, GLA_CHUNK, GLA_CHUNK)
        body = functools.partial(_gla_ffn_body, levels=levels, chunk=GLA_CHUNK, n_tiles=n_tiles,
                                 n_total=n_total, final=final)
        state_shape = (batch, GLA_HEADS, dkh, dvh)
        state_spec = pl.BlockSpec((None, GLA_HEADS, dkh, dvh), lambda s: (seq_of(s), 0, 0, 0))
        scratch = [pltpu.VMEM((GLA_HEADS, dkh, dvh), F32)] + carry
        resident += _nbytes((tri, masks)) + 3 * GLA_HEADS * dkh * dvh * 4
        resident += rows * (wqkvr.shape[1] * 4 + (levels + 4) * dk * 4)
        consts = tuple(mixer_consts) + (tri, masks) + tuple(ffn_consts)
    return pl.pallas_call(
        body,
        out_shape=(jax.ShapeDtypeStruct((n, d), F32), jax.ShapeDtypeStruct(state_shape, F32)),
        grid=(n_total + 1,),
        in_specs=[x_spec] + [_const_spec(a.shape) for a in consts],
        out_specs=(o_spec, state_spec),
        scratch_shapes=scratch,
        compiler_params=pltpu.CompilerParams(
            dimension_semantics=("arbitrary",), vmem_limit_bytes=_vmem_limit(resident)),
        name=kind + ("_ffn_final" if final else "_ffn") + "_prompt",
    )(x, *consts)


def kernel(x_prompt, x_sample, state_pool, state_gla, norm_mix, norm_ffn, norm_final, pool_w, pool_scale,
           gla_w_in, gla_w_a2, gla_b_a, gla_norm, gla_w_o, ffn_w1, ffn_w3, ffn_w2):
    batch, seq, d = x_prompt.shape
    bs, steps, _ = x_sample.shape
    depth = norm_mix.shape[0]
    dk = gla_w_a2.shape[-1]
    dv = gla_w_o.shape[1]
    rank = gla_w_a2.shape[1]
    n_qkvr = 2 * dk + 2 * dv

    xp = x_prompt.reshape(batch * seq, d)
    xs = x_sample.reshape(bs * steps, d)
    row = lambda a: a.reshape(1, -1)
    gf = row(norm_final)

    new_pool_p, new_pool_s, new_gla_p, new_gla_s = [], [], [], []
    for i in range(depth):
        j = i // 2
        final = i == depth - 1
        g = row(norm_mix[i])
        ffn_consts = (row(norm_ffn[i]), ffn_w1[i].astype(BF16), ffn_w3[i].astype(BF16),
                      ffn_w2[i].astype(BF16), gf)
        if i % 2 == 0:
            pool_consts = (g, pool_w[j].astype(BF16), row(pool_scale[j]))
            xp, c = _prompt_layer("pool", xp, pool_consts, ffn_consts, batch=batch, seq=seq, final=final)
            new_pool_p.append(c)
            xs3, c = _pool_sample(xs.reshape(bs, steps, d), state_pool, j, pool_consts)
            xs = xs3.reshape(bs * steps, d)
            new_pool_s.append(c)
        else:
            w_in = gla_w_in[j]
            wa = jnp.pad(w_in[:, n_qkvr:], ((0, 0), (0, V7X_LANES - rank))).astype(BF16)
            wa2 = jnp.pad(gla_w_a2[j], ((0, V7X_LANES - rank), (0, 0))).astype(BF16)
            gla_consts = (g, w_in[:, :n_qkvr].astype(BF16), wa, wa2, row(gla_b_a[j]), row(gla_norm[j]),
                          gla_w_o[j].astype(BF16))
            xp, s = _prompt_layer("gla", xp, gla_consts, ffn_consts, batch=batch, seq=seq, final=final)
            new_gla_p.append(s)
            xs, s = _gla_sample(xs, state_gla, j, gla_consts, steps=steps)
            new_gla_s.append(s)
        xs = _ffn(xs, ffn_consts, final=final)

    return (xp.reshape(batch, seq, d), xs.reshape(bs, steps, d),
            jnp.stack(new_pool_p), jnp.stack(new_gla_p),
            jnp.stack(new_pool_s), jnp.stack(new_gla_s))
```

```python
import functools

import numpy as np
import jax
import jax.numpy as jnp
from jax import lax
from jax.experimental import pallas as pl
from jax.experimental.pallas import tpu as pltpu

F32 = jnp.float32
BF16 = jnp.bfloat16

PAST_LEN = 16384
POOL_WINDOWS = (2, 4, 8, 16)
POOL_MAXW = max(POOL_WINDOWS)
POOL_BUF = POOL_MAXW - 1
GLA_HEADS = 4
GLA_GATE_RANK = 16
GLA_GATE_TEMP = 16.0
EPS = 1e-6
LOG2_E = 1.4426950408889634

V7X_LANES = 128
V7X_SUBLANES = 8
V7X_VMEM_BYTES = 64 * 1024 * 1024

PROMPT_ROWS = 256
GLA_CHUNK = 128
FFN_ROWS = 512
POOL_SEQS = 16
GLA_SEQS = 8
FFN_COL_CHUNKS = (1024, 1024, 768)
GLA_LEVEL_COST = 300
GLA_SCORE_COST = 300
GLA_FINISH_COST = 200
GLA_SCORE_LEAD = 2
FFN_STEP_COST = 8648
GLA_STEP_COST = 5882 + 7 * GLA_LEVEL_COST + 8 * (GLA_SCORE_COST + GLA_FINISH_COST)
POOL_STEP_COST = 2220


def _vmem_limit(resident_bytes):
    return int(min(V7X_VMEM_BYTES - (4 << 20), 2 * resident_bytes + (16 << 20)))


def _nbytes(arrays):
    return sum(a.size * a.dtype.itemsize for a in arrays)


def _const_spec(shape):
    nd = len(shape)
    return pl.BlockSpec(shape, lambda *_: (0,) * nd, pipeline_mode=pl.Buffered(1))


def _call_with_state_slot(body, args, in_specs, *, grid, out_shape, out_spec, state_shape, state_block,
                          state_index, slot, n_slots, prev, scratch_shapes, resident, name):
    n_args = len(args)
    inner = body
    aliases = {}
    if prev is None:
        state_spec = pl.BlockSpec((n_slots,) + tuple(state_block), lambda *g: (0,) + tuple(state_index(*g)))
        resident += 2 * (n_slots - 1) * int(np.prod([b for b in state_block if b is not None])) * 4

        def body(*refs):
            refs = list(refs)
            full = refs[n_args + 1]
            for other in range(n_slots):
                if other != slot:
                    full[other] = jnp.zeros(full.shape[1:], F32)
            refs[n_args + 1] = full.at[slot]
            return inner(*refs)
    else:
        state_spec = pl.BlockSpec((None,) + tuple(state_block), lambda *g: (slot,) + tuple(state_index(*g)))

        def body(*refs):
            return inner(*refs[:n_args], *refs[n_args + 1:])

        args = tuple(args) + (prev,)
        in_specs = list(in_specs) + [pl.BlockSpec(memory_space=pl.ANY)]
        aliases = {n_args: 1}
    return pl.pallas_call(
        body,
        out_shape=(out_shape, jax.ShapeDtypeStruct((n_slots,) + tuple(state_shape), F32)),
        grid=grid,
        in_specs=in_specs,
        out_specs=(out_spec, state_spec),
        scratch_shapes=scratch_shapes,
        input_output_aliases=aliases,
        compiler_params=pltpu.CompilerParams(
            dimension_semantics=("arbitrary",) * len(grid), vmem_limit_bytes=_vmem_limit(resident)),
        name=name,
    )(*args)


def _rmsnorm(x, g):
    ms = jnp.mean(x * x, axis=-1, keepdims=True)
    return x * lax.rsqrt(ms + EPS) * g


def _silu(x):
    hx = 0.5 * x
    return hx + hx * jnp.tanh(hx)


def _dot(a, b):
    return jnp.dot(a, b, preferred_element_type=F32)


def _dot_nt(a, b):
    return lax.dot_general(a, b, (((1,), (1,)), ((), ())), preferred_element_type=F32)


def _dot_tn(a, b):
    return lax.dot_general(a, b, (((0,), (0,)), ((), ())), preferred_element_type=F32)


def _split3(x):
    hi = x.astype(BF16)
    r1 = x - hi.astype(F32)
    mid = r1.astype(BF16)
    lo = (r1 - mid.astype(F32)).astype(BF16)
    return hi, mid, lo


def _ffn_math(x, g_ref, w1_ref, w3_ref, w2_ref, gf_ref, final):
    h = _rmsnorm(x, g_ref[...]).astype(BF16)
    u1 = _dot(h, w1_ref[...])
    u3 = _dot(h, w3_ref[...])
    act = (_silu(u1) * u3).astype(BF16)
    out = x + _dot(act, w2_ref[...])
    if final:
        out = _rmsnorm(out, gf_ref[...])
    return out


def _ffn_body(x_ref, g_ref, w1_ref, w3_ref, w2_ref, gf_ref, o_ref, *, final):
    o_ref[...] = _ffn_math(x_ref[...], g_ref, w1_ref, w3_ref, w2_ref, gf_ref, final)


def _ffn(x, ffn_consts, *, final):
    n, d = x.shape
    dff = ffn_consts[1].shape[1]
    rows = min(FFN_ROWS, n)
    assert n % rows == 0
    resident = _nbytes(ffn_consts) + 4 * rows * d * 4 + 3 * rows * dff * 4
    return pl.pallas_call(
        functools.partial(_ffn_body, final=final),
        out_shape=jax.ShapeDtypeStruct((n, d), F32),
        grid=(n // rows,),
        in_specs=[pl.BlockSpec((rows, d), lambda i: (i, 0))] + [_const_spec(a.shape) for a in ffn_consts],
        out_specs=pl.BlockSpec((rows, d), lambda i: (i, 0)),
        compiler_params=pltpu.CompilerParams(
            dimension_semantics=("arbitrary",), vmem_limit_bytes=_vmem_limit(resident)),
        name="ffn_final" if final else "ffn",
    )(x, *ffn_consts)


def _pool_group_steps(ext_read, h, pos, w_ref, sc, result):
    gw = h.shape[-1] // len(POOL_WINDOWS)
    ys = []
    for g, w in enumerate(POOL_WINDOWS):
        cs = slice(g * gw, (g + 1) * gw)
        hg = h[:, cs]
        s = hg
        for k in range(1, w):
            s = s + ext_read(k, cs)
        cnt = jnp.minimum(pos + 1, w).astype(F32)
        diff = (s / cnt - hg).astype(BF16)
        ys.append(_dot(diff, w_ref[g]))
        yield 100 + 70 * (w - 1)
    result["y"] = jnp.concatenate(ys, axis=-1) * sc


def _pool_groups(ext_read, h, pos, w_ref, sc):
    result = {}
    for _ in _pool_group_steps(ext_read, h, pos, w_ref, sc, result):
        pass
    return result["y"]


def _pool_sample_body(x_ref, ctx_ref, g_ref, w_ref, sc_ref, o_ref, nctx_ref, ext_ref, *, seqs, steps):
    d = x_ref.shape[-1]
    x = x_ref[...]
    h = _rmsnorm(x, g_ref[...])
    ext_ref[:, 0:1, :] = jnp.zeros((seqs, 1, d), F32)
    ext_ref[:, 1:POOL_MAXW, :] = ctx_ref[...]
    ext_ref[:, POOL_MAXW:POOL_MAXW + steps, :] = h
    pos = PAST_LEN + lax.broadcasted_iota(jnp.int32, (seqs * steps, 1), 0) % steps
    h2 = h.reshape(seqs * steps, d)

    def ext_read(k, cs):
        return ext_ref[:, POOL_MAXW - k:POOL_MAXW - k + steps, cs].reshape(seqs * steps, cs.stop - cs.start)

    y = _pool_groups(ext_read, h2, pos, w_ref, sc_ref[...])
    o_ref[...] = x + y.reshape(seqs, steps, d)
    nctx_ref[...] = ext_ref[:, steps + 1:steps + POOL_MAXW, :]


def _pool_sample(x, state_pool, layer, pool_consts, prev):
    bs, steps, d = x.shape
    seqs = POOL_SEQS
    assert bs % seqs == 0 and steps == V7X_SUBLANES
    resident = 2 * seqs * (2 * steps + 2 * POOL_MAXW) * d * 4 + seqs * (steps + POOL_MAXW) * d * 4
    return _call_with_state_slot(
        functools.partial(_pool_sample_body, seqs=seqs, steps=steps),
        (x, state_pool) + tuple(pool_consts),
        [pl.BlockSpec((seqs, steps, d), lambda i: (i, 0, 0)),
         pl.BlockSpec((None, seqs, POOL_BUF, d), lambda i: (layer, i, 0, 0))]
        + [_const_spec(a.shape) for a in pool_consts],
        grid=(bs // seqs,),
        out_shape=jax.ShapeDtypeStruct((bs, steps, d), F32),
        out_spec=pl.BlockSpec((seqs, steps, d), lambda i: (i, 0, 0)),
        state_shape=(bs, POOL_BUF, d), state_block=(seqs, POOL_BUF, d), state_index=lambda i: (i, 0, 0),
        slot=layer, n_slots=state_pool.shape[0], prev=prev,
        scratch_shapes=[pltpu.VMEM((seqs, steps + POOL_MAXW, d), F32)],
        resident=resident, name="pool_sample")


def _gla_tables(rows, seg, score_rows):
    levels = int(np.log2(seg))
    assert 2 ** levels == seg and rows % score_rows == 0 and score_rows % seg == 0
    idx = np.arange(rows)
    tri = (idx[None, :] >= (idx[:, None] // seg) * seg) & (idx[None, :] <= idx[:, None])
    row = np.arange(score_rows)[:, None]
    col = np.arange(score_rows)[None, :]
    masks = [row == col]
    for l in range(1, levels + 1):
        s = 2 ** l
        mid = (row // s) * s + s // 2 - 1
        masks.append((row // s == col // s) & (row > mid) & (col <= mid))
    masks = np.stack(masks, axis=0).astype(np.float32)
    return jnp.asarray(tri.astype(np.float32), BF16), jnp.asarray(masks, F32), levels


def _gla_front(x, g_ref, win_ref, wa2_ref, ba_ref, tri_ref):
    rank, dk = wa2_ref.shape
    dkh = dk // GLA_HEADS
    h = _rmsnorm(x, g_ref[...]).astype(BF16)
    a_low = _dot(h, win_ref[:, 6 * dk:6 * dk + rank])
    xg = _dot(a_low.astype(BF16), wa2_ref[...]) + ba_ref[...]
    la = -(jnp.maximum(-xg, 0.0) + jnp.log(1.0 + jnp.exp(-jnp.abs(xg)))) * (1.0 / GLA_GATE_TEMP)
    hi, mid, lo = _split3(la)
    tri = tri_ref[...]
    b = _dot(tri, hi) + _dot(tri, mid) + _dot(tri, lo)
    proj = _dot(h, win_ref[:, 0:6 * dk])
    q = proj[:, 0:dk] * (dkh ** -0.5)
    k = proj[:, dk:2 * dk]
    v = proj[:, 2 * dk:4 * dk]
    r = proj[:, 4 * dk:6 * dk]
    return q, k, v, r, b


def _gla_level_operand_steps(q16, k16, b, levels, ops):
    rows, dk = b.shape
    row = lax.broadcasted_iota(jnp.int32, (rows, 1), 0)
    sub = lax.broadcasted_iota(jnp.int32, (1, V7X_SUBLANES, 1), 1)
    b = b * LOG2_E
    for l in range(1, levels + 1):
        s = 2 ** l
        if s >= V7X_SUBLANES:
            b3 = b.reshape(rows // s, s, dk)
            m = b3[:, s // 2 - 1:s // 2, :]
        else:
            b3 = b.reshape(rows // V7X_SUBLANES, V7X_SUBLANES, dk)
            picks = [b3[:, j + s // 2 - 1:j + s // 2, :] for j in range(0, V7X_SUBLANES, s)]
            m = picks[-1]
            for idx in range(len(picks) - 2, -1, -1):
                m = jnp.where(sub < (idx + 1) * s, picks[idx], m)
        e = jnp.exp2(-jnp.abs(b3 - m)).reshape(rows, dk).astype(BF16)
        second = ((row >> (l - 1)) & 1) == 1
        ops.append(jnp.where(second, q16, k16) * e)
        yield GLA_LEVEL_COST


def _gla_level_operands(q16, k16, b, levels):
    ops = []
    for _ in _gla_level_operand_steps(q16, k16, b, levels, ops):
        pass
    return ops


def _gla_intra(qb16_h, kb16_h, ops, masks_ref, rs, hs):
    a = _dot_nt(qb16_h, kb16_h) * masks_ref[0]
    for l, t in enumerate(ops, start=1):
        th = t[rs, hs]
        a = a + _dot_nt(th, th) * masks_ref[l]
    return a


def _lane_column(row_vec):
    return jnp.transpose(row_vec, (1, 0))


def _gla_out(x, o, r, gn_ref, wo_ref):
    dv = o.shape[-1]
    dvh = dv // GLA_HEADS
    normed = []
    for hd in range(GLA_HEADS):
        oh = o[:, hd * dvh:(hd + 1) * dvh]
        normed.append(oh * lax.rsqrt(jnp.mean(oh * oh, axis=-1, keepdims=True) + EPS))
    o = jnp.concatenate(normed, axis=-1) * gn_ref[...]
    o = (o * _silu(r)).astype(BF16)
    return x + _dot(o, wo_ref[...])


def _gla_prompt_steps(x, states, g_ref, win_ref, wa2_ref, ba_ref, gn_ref, wo_ref,
                      tri_ref, masks_ref, result, *, levels, chunk):
    rows = x.shape[0]
    rank, dk = wa2_ref.shape
    dkh = dk // GLA_HEADS
    h = _rmsnorm(x, g_ref[...]).astype(BF16)
    a_low = _dot(h, win_ref[:, 6 * dk:6 * dk + rank])
    yield 150
    q = _dot(h, win_ref[:, 0:dk]) * (dkh ** -0.5)
    q16 = q.astype(BF16)
    yield 512
    xg = _dot(a_low.astype(BF16), wa2_ref[...]) + ba_ref[...]
    yield 150
    k = _dot(h, win_ref[:, dk:2 * dk])
    k16 = k.astype(BF16)
    yield 512
    la = -(jnp.maximum(-xg, 0.0) + jnp.log(1.0 + jnp.exp(-jnp.abs(xg)))) * (1.0 / GLA_GATE_TEMP)
    hi, mid, lo = _split3(la)
    yield 400
    vb = _dot(h, win_ref[:, 2 * dk:4 * dk]).astype(BF16)
    yield 1024
    dvh = vb.shape[-1] // GLA_HEADS
    tri = tri_ref[...]
    b = _dot(tri, hi) + _dot(tri, mid) + _dot(tri, lo)
    yield 350
    r = _dot(h, win_ref[:, 4 * dk:6 * dk])
    yield 1024
    ops = []
    yield from _gla_level_operand_steps(q16, k16, b, levels, ops)
    qb = (q * jnp.exp(b)).astype(BF16)
    yield 60

    states = list(states)
    pieces = [(ci, hd) for ci in range(rows // chunk) for hd in range(GLA_HEADS)]
    kends = {}
    outs = {}

    def scores(ci, hd):
        rs = slice(ci * chunk, (ci + 1) * chunk)
        hs = slice(hd * dkh, (hd + 1) * dkh)
        return _gla_intra(q16[rs, hs], k16[rs, hs], ops, masks_ref, rs, hs).astype(BF16)

    def finish(ci, hd, a16):
        rs = slice(ci * chunk, (ci + 1) * chunk)
        hs = slice(hd * dkh, (hd + 1) * dkh)
        vs = slice(hd * dvh, (hd + 1) * dvh)
        b_c = b[rs]
        b_last = b_c[chunk - 1:chunk, :]
        if ci not in kends:
            kends[ci] = (k[rs] * jnp.exp(b_last - b_c)).astype(BF16)
        s_old = states[hd]
        outs[ci, hd] = _dot(a16, vb[rs, vs]) + _dot(qb[rs, hs], s_old.astype(BF16))
        dec = jnp.exp(_lane_column(b_last[:, hs]))
        states[hd] = dec * s_old + _dot_tn(kends[ci][:, hs], vb[rs, vs])

    pending = []
    for piece in pieces:
        pending.append((*piece, scores(*piece)))
        yield GLA_SCORE_COST
        if len(pending) > GLA_SCORE_LEAD:
            finish(*pending.pop(0))
            yield GLA_FINISH_COST
    while pending:
        finish(*pending.pop(0))
        yield GLA_FINISH_COST
    o = jnp.concatenate(
        [jnp.concatenate([outs[ci, hd] for hd in range(GLA_HEADS)], axis=-1) for ci in range(rows // chunk)],
        axis=0)
    result["z"] = _gla_out(x, o, r, gn_ref, wo_ref)
    result["states"] = states
    yield 1700


def _ffn_steps(x, g_ref, w1_ref, w3_ref, w2_ref, gf_ref, result, *, final, col_chunks):
    rows = x.shape[0]
    h = _rmsnorm(x, g_ref[...]).astype(BF16)
    yield 200
    acc = x
    lo = 0
    pending = None
    for cols in col_chunks:
        cs = slice(lo, lo + cols)
        lo += cols
        u1 = _dot(h, w1_ref[:, cs])
        yield rows * cols // 256
        u3 = _dot(h, w3_ref[:, cs])
        yield rows * cols // 256
        if pending is not None:
            acc = acc + _dot(pending[0], w2_ref[pending[1], :])
            yield rows * (pending[1].stop - pending[1].start) // 256
        pending = ((_silu(u1) * u3).astype(BF16), cs)
    acc = acc + _dot(pending[0], w2_ref[pending[1], :])
    yield rows * (pending[1].stop - pending[1].start) // 256
    if final:
        acc = _rmsnorm(acc, gf_ref[...])
    result["out"] = acc


def _interleave(*pairs):
    gens = [p[0] for p in pairs]
    totals = [float(p[1]) for p in pairs]
    done = [0.0] * len(gens)
    live = list(range(len(gens)))
    while live:
        i = min(live, key=lambda j: done[j] / totals[j])
        try:
            done[i] += next(gens[i])
        except StopIteration:
            live.remove(i)


def _gla_sample_body(x_ref, s0_ref, g_ref, win_ref, wa2_ref, ba_ref, gn_ref, wo_ref,
                     tri_ref, masks_ref, o_ref, sout_ref, *, levels, seqs, steps):
    x = x_ref[...]
    rows = x.shape[0]
    q, k, v, r, b = _gla_front(x, g_ref, win_ref, wa2_ref, ba_ref, tri_ref)
    dk = q.shape[-1]
    dkh = dk // GLA_HEADS
    dvh = v.shape[-1] // GLA_HEADS
    q16 = q.astype(BF16)
    k16 = k.astype(BF16)
    ops = _gla_level_operands(q16, k16, b, levels)
    b3 = b.reshape(seqs, steps, dk)
    b_last = b3[:, steps - 1:steps, :]
    qb = (q * jnp.exp(b)).astype(BF16)
    kend = (k.reshape(seqs, steps, dk) * jnp.exp(b_last - b3)).reshape(rows, dk).astype(BF16)
    vb = v.astype(BF16)
    all_rows = slice(0, rows)
    outs = []
    for hd in range(GLA_HEADS):
        hs = slice(hd * dkh, (hd + 1) * dkh)
        vs = slice(hd * dvh, (hd + 1) * dvh)
        a = _gla_intra(q16[:, hs], k16[:, hs], ops, masks_ref, all_rows, hs)
        o_h = _dot(a.astype(BF16), vb[:, vs])
        o_state = []
        for s in range(seqs):
            rs = slice(s * steps, (s + 1) * steps)
            s_old = s0_ref[s, hd]
            o_state.append(_dot(qb[rs, hs], s_old.astype(BF16)))
            dec = jnp.exp(_lane_column(b[(s + 1) * steps - 1:(s + 1) * steps, hs]))
            sout_ref[s, hd] = dec * s_old + _dot_tn(kend[rs, hs], vb[rs, vs])
        outs.append(o_h + jnp.concatenate(o_state, axis=0))
    o_ref[...] = _gla_out(x, jnp.concatenate(outs, axis=-1), r, gn_ref, wo_ref)


def _gla_sample(x, state_gla, layer, gla_consts, prev, *, steps):
    n, d = x.shape
    seqs = GLA_SEQS
    rows = seqs * steps
    bs = n // steps
    assert bs % seqs == 0
    win, wa2, wo = gla_consts[1], gla_consts[2], gla_consts[5]
    dk = wa2.shape[1]
    dkh, dvh = dk // GLA_HEADS, wo.shape[0] // GLA_HEADS
    tri, masks, levels = _gla_tables(rows, steps, rows)
    consts = tuple(gla_consts) + (tri, masks)
    state_block = seqs * GLA_HEADS * dkh * dvh * 4
    resident = _nbytes(consts) + 4 * rows * d * 4 + 4 * state_block
    resident += rows * (win.shape[1] * 4 + (levels + 4) * dk * 4)
    return _call_with_state_slot(
        functools.partial(_gla_sample_body, levels=levels, seqs=seqs, steps=steps),
        (x, state_gla) + consts,
        [pl.BlockSpec((rows, d), lambda i: (i, 0)),
         pl.BlockSpec((None, seqs, GLA_HEADS, dkh, dvh), lambda i: (layer, i, 0, 0, 0))]
        + [_const_spec(a.shape) for a in consts],
        grid=(bs // seqs,),
        out_shape=jax.ShapeDtypeStruct((n, d), F32),
        out_spec=pl.BlockSpec((rows, d), lambda i: (i, 0)),
        state_shape=(bs, GLA_HEADS, dkh, dvh), state_block=(seqs, GLA_HEADS, dkh, dvh),
        state_index=lambda i: (i, 0, 0, 0),
        slot=layer, n_slots=state_gla.shape[0], prev=prev,
        scratch_shapes=[], resident=resident, name="gla_sample")


def _pool_ffn_body(x_ref, gm_ref, wp_ref, sc_ref, gfn_ref, w1_ref, w3_ref, w2_ref, gf_ref,
                   o_ref, ctx_ref, ext_ref, z_ref, res_ref, *, rows, n_tiles, n_total, final):
    s = pl.program_id(0)
    d = x_ref.shape[-1]
    t = jnp.minimum(s, n_total - 1) % n_tiles
    live = s < n_total

    @pl.when(s == 0)
    def _():
        z_ref[...] = jnp.zeros(z_ref.shape, F32)

    @pl.when(t == 0)
    def _():
        ext_ref[0:POOL_MAXW, :] = jnp.zeros((POOL_MAXW, d), F32)

    res_ref[...] = z_ref[...]

    x = x_ref[...]
    h = _rmsnorm(x, gm_ref[...])
    ext_ref[POOL_MAXW:POOL_MAXW + rows, :] = h
    pos = t * rows + lax.broadcasted_iota(jnp.int32, (rows, 1), 0)

    def ext_read(k, cs):
        return ext_ref[POOL_MAXW - k:POOL_MAXW - k + rows, cs]

    mix, ffn = {}, {}
    _interleave(
        (_pool_group_steps(ext_read, h, pos, wp_ref, sc_ref[...], mix), POOL_STEP_COST),
        (_ffn_steps(res_ref[...], gfn_ref, w1_ref, w3_ref, w2_ref, gf_ref, ffn,
                    final=final, col_chunks=FFN_COL_CHUNKS), FFN_STEP_COST))
    z_ref[...] = x + mix["y"]
    o_ref[...] = ffn["out"]

    @pl.when(live)
    def _():
        ext_ref[0:POOL_MAXW, :] = ext_ref[rows:rows + POOL_MAXW, :]

    @pl.when(jnp.logical_and(live, t == n_tiles - 1))
    def _():
        ctx_ref[...] = ext_ref[rows + 1:rows + POOL_MAXW, :]


def _gla_ffn_body(x_ref, gm_ref, win_ref, wa2_ref, ba_ref, gn_ref, wo_ref, tri_ref, masks_ref,
                  gfn_ref, w1_ref, w3_ref, w2_ref, gf_ref,
                  o_ref, sout_ref, s_ref, z_ref, res_ref, *, levels, chunk, n_tiles, n_total, final):
    s = pl.program_id(0)
    t = jnp.minimum(s, n_total - 1) % n_tiles
    live = s < n_total

    @pl.when(s == 0)
    def _():
        z_ref[...] = jnp.zeros(z_ref.shape, F32)

    @pl.when(t == 0)
    def _():
        s_ref[...] = jnp.zeros(s_ref.shape, F32)

    res_ref[...] = z_ref[...]

    states = [s_ref[hd] for hd in range(GLA_HEADS)]
    mix, ffn = {}, {}
    _interleave(
        (_gla_prompt_steps(x_ref[...], states, gm_ref, win_ref, wa2_ref, ba_ref, gn_ref,
                           wo_ref, tri_ref, masks_ref, mix, levels=levels, chunk=chunk), GLA_STEP_COST),
        (_ffn_steps(res_ref[...], gfn_ref, w1_ref, w3_ref, w2_ref, gf_ref, ffn,
                    final=final, col_chunks=FFN_COL_CHUNKS), FFN_STEP_COST))
    states = mix["states"]
    z_ref[...] = mix["z"]
    o_ref[...] = ffn["out"]

    @pl.when(live)
    def _():
        for hd in range(GLA_HEADS):
            s_ref[hd] = states[hd]

    @pl.when(jnp.logical_and(live, t == n_tiles - 1))
    def _():
        for hd in range(GLA_HEADS):
            sout_ref[hd] = states[hd]


def _prompt_layer(kind, x, mixer_consts, ffn_consts, *, batch, seq, final, slot, n_slots, prev):
    n, d = x.shape
    rows = PROMPT_ROWS
    n_tiles = seq // rows
    n_total = batch * n_tiles
    assert seq % rows == 0 and rows >= POOL_MAXW and rows % GLA_CHUNK == 0
    dff = ffn_consts[1].shape[1]
    resident = _nbytes(mixer_consts) + _nbytes(ffn_consts) + 6 * rows * d * 4 + 3 * rows * dff * 4
    x_spec = pl.BlockSpec((rows, d), lambda s: (jnp.minimum(s, n_total - 1), 0))
    o_spec = pl.BlockSpec((rows, d), lambda s: (jnp.maximum(s - 1, 0), 0))
    seq_of = lambda s: jnp.minimum(s, n_total - 1) // n_tiles
    carry = [pltpu.VMEM((rows, d), F32), pltpu.VMEM((rows, d), F32)]
    if kind == "pool":
        body = functools.partial(_pool_ffn_body, rows=rows, n_tiles=n_tiles, n_total=n_total, final=final)
        state_shape = (batch, POOL_BUF, d)
        state_block = (None, POOL_BUF, d)
        state_index = lambda s: (seq_of(s), 0, 0)
        scratch = [pltpu.VMEM((rows + POOL_MAXW, d), F32)] + carry
        resident += (rows + POOL_MAXW) * d * 4
        consts = tuple(mixer_consts) + tuple(ffn_consts)
    else:
        win, wa2, wo = mixer_consts[1], mixer_consts[2], mixer_consts[5]
        dk = wa2.shape[1]
        dkh, dvh = dk // GLA_HEADS, wo.shape[0] // GLA_HEADS
        tri, masks, levels = _gla_tables(rows, GLA_CHUNK, GLA_CHUNK)
        body = functools.partial(_gla_ffn_body, levels=levels, chunk=GLA_CHUNK, n_tiles=n_tiles,
                                 n_total=n_total, final=final)
        state_shape = (batch, GLA_HEADS, dkh, dvh)
        state_block = (None, GLA_HEADS, dkh, dvh)
        state_index = lambda s: (seq_of(s), 0, 0, 0)
        scratch = [pltpu.VMEM((GLA_HEADS, dkh, dvh), F32)] + carry
        resident += _nbytes((tri, masks)) + 3 * GLA_HEADS * dkh * dvh * 4
        resident += rows * (win.shape[1] * 4 + (levels + 4) * dk * 4)
        consts = tuple(mixer_consts) + (tri, masks) + tuple(ffn_consts)
    return _call_with_state_slot(
        body, (x,) + consts, [x_spec] + [_const_spec(a.shape) for a in consts],
        grid=(n_total + 1,),
        out_shape=jax.ShapeDtypeStruct((n, d), F32), out_spec=o_spec,
        state_shape=state_shape, state_block=state_block, state_index=state_index,
        slot=slot, n_slots=n_slots, prev=prev,
        scratch_shapes=scratch, resident=resident,
        name=kind + ("_ffn_final" if final else "_ffn") + "_prompt")


def kernel(x_prompt, x_sample, state_pool, state_gla, norm_mix, norm_ffn, norm_final, pool_w, pool_scale,
           gla_w_in, gla_w_a2, gla_b_a, gla_norm, gla_w_o, ffn_w1, ffn_w3, ffn_w2):
    batch, seq, d = x_prompt.shape
    bs, steps, _ = x_sample.shape
    depth = norm_mix.shape[0]
    w_in = gla_w_in.astype(BF16)

    xp = x_prompt.reshape(batch * seq, d)
    xs = x_sample.reshape(bs * steps, d)
    row = lambda a: a.reshape(1, -1)
    gf = row(norm_final)

    n_pool, n_gla = state_pool.shape[0], state_gla.shape[0]
    new_pool_p = new_pool_s = new_gla_p = new_gla_s = None
    for i in range(depth):
        j = i // 2
        final = i == depth - 1
        g = row(norm_mix[i])
        ffn_consts = (row(norm_ffn[i]), ffn_w1[i].astype(BF16), ffn_w3[i].astype(BF16),
                      ffn_w2[i].astype(BF16), gf)
        if i % 2 == 0:
            pool_consts = (g, pool_w[j].astype(BF16), row(pool_scale[j]))
            xp, new_pool_p = _prompt_layer("pool", xp, pool_consts, ffn_consts, batch=batch, seq=seq,
                                           final=final, slot=j, n_slots=n_pool, prev=new_pool_p)
            xs3, new_pool_s = _pool_sample(xs.reshape(bs, steps, d), state_pool, j, pool_consts, new_pool_s)
            xs = xs3.reshape(bs * steps, d)
        else:
            gla_consts = (g, w_in[j], gla_w_a2[j].astype(BF16), row(gla_b_a[j]), row(gla_norm[j]),
                          gla_w_o[j].astype(BF16))
            xp, new_gla_p = _prompt_layer("gla", xp, gla_consts, ffn_consts, batch=batch, seq=seq,
                                          final=final, slot=j, n_slots=n_gla, prev=new_gla_p)
            xs, new_gla_s = _gla_sample(xs, state_gla, j, gla_consts, new_gla_s, steps=steps)
        xs = _ffn(xs, ffn_consts, final=final)

    return (xp.reshape(batch, seq, d), xs.reshape(bs, steps, d),
            new_pool_p, new_gla_p, new_pool_s, new_gla_s)
```

```python
import functools

import numpy as np
import jax
import jax.numpy as jnp
from jax import lax
from jax.experimental import pallas as pl
from jax.experimental.pallas import tpu as pltpu

F32 = jnp.float32
BF16 = jnp.bfloat16

PAST_LEN = 16384
POOL_WINDOWS = (2, 4, 8, 16)
POOL_MAXW = max(POOL_WINDOWS)
POOL_BUF = POOL_MAXW - 1
GLA_HEADS = 4
GLA_GATE_RANK = 16
GLA_GATE_TEMP = 16.0
EPS = 1e-6
LOG2_E = 1.4426950408889634

V7X_LANES = 128
V7X_SUBLANES = 8
V7X_VMEM_BYTES = 64 * 1024 * 1024

PROMPT_ROWS = 256
GLA_CHUNK = 128
FFN_ROWS = 512
POOL_SEQS = 16
GLA_SEQS = 8
FFN_COL_CHUNKS = (1024, 1024, 768)
GLA_LEVEL_COST = 300
GLA_SCORE_COST = 300
GLA_FINISH_COST = 200
GLA_SCORE_LEAD = 2
FFN_STEP_COST = 8648
GLA_STEP_COST = 5882 + 7 * GLA_LEVEL_COST + 8 * (GLA_SCORE_COST + GLA_FINISH_COST)
POOL_STEP_COST = 2220


def _vmem_limit(resident_bytes):
    return int(min(V7X_VMEM_BYTES - (4 << 20), 2 * resident_bytes + (16 << 20)))


class _Const:
    def __init__(self, array, layer=None):
        self.array = array
        self.layer = layer
        self.shape = tuple(array.shape if layer is None else array.shape[1:])
        self.nbytes = int(np.prod(self.shape)) * array.dtype.itemsize

    def spec(self):
        nd = len(self.shape)
        if self.layer is None:
            return pl.BlockSpec(self.shape, lambda *_: (0,) * nd, pipeline_mode=pl.Buffered(1))
        layer = self.layer
        return pl.BlockSpec((None,) + self.shape, lambda *_: (layer,) + (0,) * nd, pipeline_mode=pl.Buffered(1))


def _nbytes(consts):
    return sum(c.nbytes for c in consts)


def _arrays(consts):
    return tuple(c.array for c in consts)


def _specs(consts):
    return [c.spec() for c in consts]


def _call_with_state_slot(body, args, in_specs, *, grid, out_shape, out_spec, state_shape, state_block,
                          state_index, slot, n_slots, prev, scratch_shapes, resident, name):
    n_args = len(args)
    inner = body
    aliases = {}
    if prev is None:
        state_spec = pl.BlockSpec((n_slots,) + tuple(state_block), lambda *g: (0,) + tuple(state_index(*g)))
        resident += 2 * (n_slots - 1) * int(np.prod([b for b in state_block if b is not None])) * 4

        def body(*refs):
            refs = list(refs)
            full = refs[n_args + 1]
            for other in range(n_slots):
                if other != slot:
                    full[other] = jnp.zeros(full.shape[1:], F32)
            refs[n_args + 1] = full.at[slot]
            return inner(*refs)
    else:
        state_spec = pl.BlockSpec((None,) + tuple(state_block), lambda *g: (slot,) + tuple(state_index(*g)))

        def body(*refs):
            return inner(*refs[:n_args], *refs[n_args + 1:])

        args = tuple(args) + (prev,)
        in_specs = list(in_specs) + [pl.BlockSpec(memory_space=pl.ANY)]
        aliases = {n_args: 1}
    return pl.pallas_call(
        body,
        out_shape=(out_shape, jax.ShapeDtypeStruct((n_slots,) + tuple(state_shape), F32)),
        grid=grid,
        in_specs=in_specs,
        out_specs=(out_spec, state_spec),
        scratch_shapes=scratch_shapes,
        input_output_aliases=aliases,
        compiler_params=pltpu.CompilerParams(
            dimension_semantics=("arbitrary",) * len(grid), vmem_limit_bytes=_vmem_limit(resident)),
        name=name,
    )(*args)


def _rmsnorm(x, g):
    ms = jnp.mean(x * x, axis=-1, keepdims=True)
    return x * lax.rsqrt(ms + EPS) * g


def _silu(x):
    hx = 0.5 * x
    return hx + hx * jnp.tanh(hx)


def _dot(a, b):
    return jnp.dot(a, b, preferred_element_type=F32)


def _dot_nt(a, b):
    return lax.dot_general(a, b, (((1,), (1,)), ((), ())), preferred_element_type=F32)


def _dot_tn(a, b):
    return lax.dot_general(a, b, (((0,), (0,)), ((), ())), preferred_element_type=F32)


def _split3(x):
    hi = x.astype(BF16)
    r1 = x - hi.astype(F32)
    mid = r1.astype(BF16)
    lo = (r1 - mid.astype(F32)).astype(BF16)
    return hi, mid, lo


def _ffn_math(x, g_ref, w1_ref, w3_ref, w2_ref, gf_ref, final):
    h = _rmsnorm(x, g_ref[...]).astype(BF16)
    u1 = _dot(h, w1_ref[...])
    u3 = _dot(h, w3_ref[...])
    act = (_silu(u1) * u3).astype(BF16)
    out = x + _dot(act, w2_ref[...])
    if final:
        out = _rmsnorm(out, gf_ref[...])
    return out


def _ffn_body(x_ref, g_ref, w1_ref, w3_ref, w2_ref, gf_ref, o_ref, *, final):
    o_ref[...] = _ffn_math(x_ref[...], g_ref, w1_ref, w3_ref, w2_ref, gf_ref, final)


def _ffn(x, ffn_consts, *, final):
    n, d = x.shape
    dff = ffn_consts[1].shape[1]
    rows = min(FFN_ROWS, n)
    assert n % rows == 0
    resident = _nbytes(ffn_consts) + 4 * rows * d * 4 + 3 * rows * dff * 4
    return pl.pallas_call(
        functools.partial(_ffn_body, final=final),
        out_shape=jax.ShapeDtypeStruct((n, d), F32),
        grid=(n // rows,),
        in_specs=[pl.BlockSpec((rows, d), lambda i: (i, 0))] + _specs(ffn_consts),
        out_specs=pl.BlockSpec((rows, d), lambda i: (i, 0)),
        compiler_params=pltpu.CompilerParams(
            dimension_semantics=("arbitrary",), vmem_limit_bytes=_vmem_limit(resident)),
        name="ffn_final" if final else "ffn",
    )(x, *_arrays(ffn_consts))


def _pool_group_steps(ext_read, h, pos, w_ref, sc, result):
    gw = h.shape[-1] // len(POOL_WINDOWS)
    ys = []
    for g, w in enumerate(POOL_WINDOWS):
        cs = slice(g * gw, (g + 1) * gw)
        hg = h[:, cs]
        s = hg
        for k in range(1, w):
            s = s + ext_read(k, cs)
        cnt = jnp.minimum(pos + 1, w).astype(F32)
        diff = (s / cnt - hg).astype(BF16)
        ys.append(_dot(diff, w_ref[g]))
        yield 100 + 70 * (w - 1)
    result["y"] = jnp.concatenate(ys, axis=-1) * sc


def _pool_groups(ext_read, h, pos, w_ref, sc):
    result = {}
    for _ in _pool_group_steps(ext_read, h, pos, w_ref, sc, result):
        pass
    return result["y"]


def _pool_sample_body(x_ref, ctx_ref, g_ref, w_ref, sc_ref, o_ref, nctx_ref, ext_ref, *, seqs, steps):
    d = x_ref.shape[-1]
    x = x_ref[...]
    h = _rmsnorm(x, g_ref[...])
    ext_ref[:, 0:1, :] = jnp.zeros((seqs, 1, d), F32)
    ext_ref[:, 1:POOL_MAXW, :] = ctx_ref[...]
    ext_ref[:, POOL_MAXW:POOL_MAXW + steps, :] = h
    pos = PAST_LEN + lax.broadcasted_iota(jnp.int32, (seqs * steps, 1), 0) % steps
    h2 = h.reshape(seqs * steps, d)

    def ext_read(k, cs):
        return ext_ref[:, POOL_MAXW - k:POOL_MAXW - k + steps, cs].reshape(seqs * steps, cs.stop - cs.start)

    y = _pool_groups(ext_read, h2, pos, w_ref, sc_ref[...])
    o_ref[...] = x + y.reshape(seqs, steps, d)
    nctx_ref[...] = ext_ref[:, steps + 1:steps + POOL_MAXW, :]


def _pool_sample(x, state_pool, layer, pool_consts, prev):
    bs, steps, d = x.shape
    seqs = POOL_SEQS
    assert bs % seqs == 0 and steps == V7X_SUBLANES
    resident = 2 * seqs * (2 * steps + 2 * POOL_MAXW) * d * 4 + seqs * (steps + POOL_MAXW) * d * 4
    return _call_with_state_slot(
        functools.partial(_pool_sample_body, seqs=seqs, steps=steps),
        (x, state_pool) + _arrays(pool_consts),
        [pl.BlockSpec((seqs, steps, d), lambda i: (i, 0, 0)),
         pl.BlockSpec((None, seqs, POOL_BUF, d), lambda i: (layer, i, 0, 0))]
        + _specs(pool_consts),
        grid=(bs // seqs,),
        out_shape=jax.ShapeDtypeStruct((bs, steps, d), F32),
        out_spec=pl.BlockSpec((seqs, steps, d), lambda i: (i, 0, 0)),
        state_shape=(bs, POOL_BUF, d), state_block=(seqs, POOL_BUF, d), state_index=lambda i: (i, 0, 0),
        slot=layer, n_slots=state_pool.shape[0], prev=prev,
        scratch_shapes=[pltpu.VMEM((seqs, steps + POOL_MAXW, d), F32)],
        resident=resident, name="pool_sample")


def _gla_tables(rows, seg, score_rows):
    levels = int(np.log2(seg))
    assert 2 ** levels == seg and rows % score_rows == 0 and score_rows % seg == 0
    idx = np.arange(rows)
    tri = (idx[None, :] >= (idx[:, None] // seg) * seg) & (idx[None, :] <= idx[:, None])
    row = np.arange(score_rows)[:, None]
    col = np.arange(score_rows)[None, :]
    masks = [row == col]
    for l in range(1, levels + 1):
        s = 2 ** l
        mid = (row // s) * s + s // 2 - 1
        masks.append((row // s == col // s) & (row > mid) & (col <= mid))
    masks = np.stack(masks, axis=0).astype(np.float32)
    return jnp.asarray(tri.astype(np.float32), BF16), jnp.asarray(masks, F32), levels


def _gla_front(x, g_ref, win_ref, wa2_ref, ba_ref, tri_ref):
    rank, dk = wa2_ref.shape
    dkh = dk // GLA_HEADS
    h = _rmsnorm(x, g_ref[...]).astype(BF16)
    a_low = _dot(h, win_ref[:, 6 * dk:6 * dk + rank])
    xg = _dot(a_low.astype(BF16), wa2_ref[...]) + ba_ref[...]
    la = -(jnp.maximum(-xg, 0.0) + jnp.log(1.0 + jnp.exp(-jnp.abs(xg)))) * (1.0 / GLA_GATE_TEMP)
    hi, mid, lo = _split3(la)
    tri = tri_ref[...]
    b = _dot(tri, hi) + _dot(tri, mid) + _dot(tri, lo)
    proj = _dot(h, win_ref[:, 0:6 * dk])
    q = proj[:, 0:dk] * (dkh ** -0.5)
    k = proj[:, dk:2 * dk]
    v = proj[:, 2 * dk:4 * dk]
    r = proj[:, 4 * dk:6 * dk]
    return q, k, v, r, b


def _gla_level_operand_steps(q16, k16, b, levels, ops):
    rows, dk = b.shape
    row = lax.broadcasted_iota(jnp.int32, (rows, 1), 0)
    sub = lax.broadcasted_iota(jnp.int32, (1, V7X_SUBLANES, 1), 1)
    b = b * LOG2_E
    for l in range(1, levels + 1):
        s = 2 ** l
        if s >= V7X_SUBLANES:
            b3 = b.reshape(rows // s, s, dk)
            m = b3[:, s // 2 - 1:s // 2, :]
        else:
            b3 = b.reshape(rows // V7X_SUBLANES, V7X_SUBLANES, dk)
            picks = [b3[:, j + s // 2 - 1:j + s // 2, :] for j in range(0, V7X_SUBLANES, s)]
            m = picks[-1]
            for idx in range(len(picks) - 2, -1, -1):
                m = jnp.where(sub < (idx + 1) * s, picks[idx], m)
        e = jnp.exp2(-jnp.abs(b3 - m)).reshape(rows, dk).astype(BF16)
        second = ((row >> (l - 1)) & 1) == 1
        ops.append(jnp.where(second, q16, k16) * e)
        yield GLA_LEVEL_COST


def _gla_level_operands(q16, k16, b, levels):
    ops = []
    for _ in _gla_level_operand_steps(q16, k16, b, levels, ops):
        pass
    return ops


def _gla_intra(qb16_h, kb16_h, ops, masks_ref, rs, hs):
    a = _dot_nt(qb16_h, kb16_h) * masks_ref[0]
    for l, t in enumerate(ops, start=1):
        th = t[rs, hs]
        a = a + _dot_nt(th, th) * masks_ref[l]
    return a


def _lane_column(row_vec):
    return jnp.transpose(row_vec, (1, 0))


def _gla_out(x, o, r, gn_ref, wo_ref):
    dv = o.shape[-1]
    dvh = dv // GLA_HEADS
    normed = []
    for hd in range(GLA_HEADS):
        oh = o[:, hd * dvh:(hd + 1) * dvh]
        normed.append(oh * lax.rsqrt(jnp.mean(oh * oh, axis=-1, keepdims=True) + EPS))
    o = jnp.concatenate(normed, axis=-1) * gn_ref[...]
    o = (o * _silu(r)).astype(BF16)
    return x + _dot(o, wo_ref[...])


def _gla_prompt_steps(x, states, g_ref, win_ref, wa2_ref, ba_ref, gn_ref, wo_ref,
                      tri_ref, masks_ref, result, *, levels, chunk):
    rows = x.shape[0]
    rank, dk = wa2_ref.shape
    dkh = dk // GLA_HEADS
    h = _rmsnorm(x, g_ref[...]).astype(BF16)
    a_low = _dot(h, win_ref[:, 6 * dk:6 * dk + rank])
    yield 150
    q = _dot(h, win_ref[:, 0:dk]) * (dkh ** -0.5)
    q16 = q.astype(BF16)
    yield 512
    xg = _dot(a_low.astype(BF16), wa2_ref[...]) + ba_ref[...]
    yield 150
    k = _dot(h, win_ref[:, dk:2 * dk])
    k16 = k.astype(BF16)
    yield 512
    la = -(jnp.maximum(-xg, 0.0) + jnp.log(1.0 + jnp.exp(-jnp.abs(xg)))) * (1.0 / GLA_GATE_TEMP)
    hi, mid, lo = _split3(la)
    yield 400
    vb = _dot(h, win_ref[:, 2 * dk:4 * dk]).astype(BF16)
    yield 1024
    dvh = vb.shape[-1] // GLA_HEADS
    tri = tri_ref[...]
    b = _dot(tri, hi) + _dot(tri, mid) + _dot(tri, lo)
    yield 350
    r = _dot(h, win_ref[:, 4 * dk:6 * dk])
    yield 1024
    ops = []
    yield from _gla_level_operand_steps(q16, k16, b, levels, ops)
    qb = (q * jnp.exp(b)).astype(BF16)
    yield 60

    states = list(states)
    pieces = [(ci, hd) for ci in range(rows // chunk) for hd in range(GLA_HEADS)]
    kends = {}
    outs = {}

    def scores(ci, hd):
        rs = slice(ci * chunk, (ci + 1) * chunk)
        hs = slice(hd * dkh, (hd + 1) * dkh)
        return _gla_intra(q16[rs, hs], k16[rs, hs], ops, masks_ref, rs, hs).astype(BF16)

    def finish(ci, hd, a16):
        rs = slice(ci * chunk, (ci + 1) * chunk)
        hs = slice(hd * dkh, (hd + 1) * dkh)
        vs = slice(hd * dvh, (hd + 1) * dvh)
        b_c = b[rs]
        b_last = b_c[chunk - 1:chunk, :]
        if ci not in kends:
            kends[ci] = (k[rs] * jnp.exp(b_last - b_c)).astype(BF16)
        s_old = states[hd]
        outs[ci, hd] = _dot(a16, vb[rs, vs]) + _dot(qb[rs, hs], s_old.astype(BF16))
        dec = jnp.exp(_lane_column(b_last[:, hs]))
        states[hd] = dec * s_old + _dot_tn(kends[ci][:, hs], vb[rs, vs])

    pending = []
    for piece in pieces:
        pending.append((*piece, scores(*piece)))
        yield GLA_SCORE_COST
        if len(pending) > GLA_SCORE_LEAD:
            finish(*pending.pop(0))
            yield GLA_FINISH_COST
    while pending:
        finish(*pending.pop(0))
        yield GLA_FINISH_COST
    o = jnp.concatenate(
        [jnp.concatenate([outs[ci, hd] for hd in range(GLA_HEADS)], axis=-1) for ci in range(rows // chunk)],
        axis=0)
    result["z"] = _gla_out(x, o, r, gn_ref, wo_ref)
    result["states"] = states
    yield 1700


def _ffn_steps(x, g_ref, w1_ref, w3_ref, w2_ref, gf_ref, result, *, final, col_chunks):
    rows = x.shape[0]
    h = _rmsnorm(x, g_ref[...]).astype(BF16)
    yield 200
    acc = x
    lo = 0
    pending = None
    for cols in col_chunks:
        cs = slice(lo, lo + cols)
        lo += cols
        u1 = _dot(h, w1_ref[:, cs])
        yield rows * cols // 256
        u3 = _dot(h, w3_ref[:, cs])
        yield rows * cols // 256
        if pending is not None:
            acc = acc + _dot(pending[0], w2_ref[pending[1], :])
            yield rows * (pending[1].stop - pending[1].start) // 256
        pending = ((_silu(u1) * u3).astype(BF16), cs)
    acc = acc + _dot(pending[0], w2_ref[pending[1], :])
    yield rows * (pending[1].stop - pending[1].start) // 256
    if final:
        acc = _rmsnorm(acc, gf_ref[...])
    result["out"] = acc


def _interleave(*pairs):
    gens = [p[0] for p in pairs]
    totals = [float(p[1]) for p in pairs]
    done = [0.0] * len(gens)
    live = list(range(len(gens)))
    while live:
        i = min(live, key=lambda j: done[j] / totals[j])
        try:
            done[i] += next(gens[i])
        except StopIteration:
            live.remove(i)


def _gla_sample_body(x_ref, s0_ref, g_ref, win_ref, wa2_ref, ba_ref, gn_ref, wo_ref,
                     tri_ref, masks_ref, o_ref, sout_ref, *, levels, seqs, steps):
    x = x_ref[...]
    rows = x.shape[0]
    q, k, v, r, b = _gla_front(x, g_ref, win_ref, wa2_ref, ba_ref, tri_ref)
    dk = q.shape[-1]
    dkh = dk // GLA_HEADS
    dvh = v.shape[-1] // GLA_HEADS
    q16 = q.astype(BF16)
    k16 = k.astype(BF16)
    ops = _gla_level_operands(q16, k16, b, levels)
    b3 = b.reshape(seqs, steps, dk)
    b_last = b3[:, steps - 1:steps, :]
    qb = (q * jnp.exp(b)).astype(BF16)
    kend = (k.reshape(seqs, steps, dk) * jnp.exp(b_last - b3)).reshape(rows, dk).astype(BF16)
    vb = v.astype(BF16)
    all_rows = slice(0, rows)
    outs = []
    for hd in range(GLA_HEADS):
        hs = slice(hd * dkh, (hd + 1) * dkh)
        vs = slice(hd * dvh, (hd + 1) * dvh)
        a = _gla_intra(q16[:, hs], k16[:, hs], ops, masks_ref, all_rows, hs)
        o_h = _dot(a.astype(BF16), vb[:, vs])
        o_state = []
        for s in range(seqs):
            rs = slice(s * steps, (s + 1) * steps)
            s_old = s0_ref[s, hd]
            o_state.append(_dot(qb[rs, hs], s_old.astype(BF16)))
            dec = jnp.exp(_lane_column(b[(s + 1) * steps - 1:(s + 1) * steps, hs]))
            sout_ref[s, hd] = dec * s_old + _dot_tn(kend[rs, hs], vb[rs, vs])
        outs.append(o_h + jnp.concatenate(o_state, axis=0))
    o_ref[...] = _gla_out(x, jnp.concatenate(outs, axis=-1), r, gn_ref, wo_ref)


def _gla_sample(x, state_gla, layer, gla_consts, prev, *, steps):
    n, d = x.shape
    seqs = GLA_SEQS
    rows = seqs * steps
    bs = n // steps
    assert bs % seqs == 0
    win, wa2, wo = gla_consts[1], gla_consts[2], gla_consts[5]
    dk = wa2.shape[1]
    dkh, dvh = dk // GLA_HEADS, wo.shape[0] // GLA_HEADS
    tri, masks, levels = _gla_tables(rows, steps, rows)
    consts = tuple(gla_consts) + (_Const(tri), _Const(masks))
    state_block = seqs * GLA_HEADS * dkh * dvh * 4
    resident = _nbytes(consts) + 4 * rows * d * 4 + 4 * state_block
    resident += rows * (win.shape[1] * 4 + (levels + 4) * dk * 4)
    return _call_with_state_slot(
        functools.partial(_gla_sample_body, levels=levels, seqs=seqs, steps=steps),
        (x, state_gla) + _arrays(consts),
        [pl.BlockSpec((rows, d), lambda i: (i, 0)),
         pl.BlockSpec((None, seqs, GLA_HEADS, dkh, dvh), lambda i: (layer, i, 0, 0, 0))]
        + _specs(consts),
        grid=(bs // seqs,),
        out_shape=jax.ShapeDtypeStruct((n, d), F32),
        out_spec=pl.BlockSpec((rows, d), lambda i: (i, 0)),
        state_shape=(bs, GLA_HEADS, dkh, dvh), state_block=(seqs, GLA_HEADS, dkh, dvh),
        state_index=lambda i: (i, 0, 0, 0),
        slot=layer, n_slots=state_gla.shape[0], prev=prev,
        scratch_shapes=[], resident=resident, name="gla_sample")


def _pool_ffn_body(x_ref, gm_ref, wp_ref, sc_ref, gfn_ref, w1_ref, w3_ref, w2_ref, gf_ref,
                   o_ref, ctx_ref, ext_ref, z_ref, res_ref, *, rows, n_tiles, n_total, final):
    s = pl.program_id(0)
    d = x_ref.shape[-1]
    t = jnp.minimum(s, n_total - 1) % n_tiles
    live = s < n_total

    @pl.when(s == 0)
    def _():
        z_ref[...] = jnp.zeros(z_ref.shape, F32)

    @pl.when(t == 0)
    def _():
        ext_ref[0:POOL_MAXW, :] = jnp.zeros((POOL_MAXW, d), F32)

    res_ref[...] = z_ref[...]

    x = x_ref[...]
    h = _rmsnorm(x, gm_ref[...])
    ext_ref[POOL_MAXW:POOL_MAXW + rows, :] = h
    pos = t * rows + lax.broadcasted_iota(jnp.int32, (rows, 1), 0)

    def ext_read(k, cs):
        return ext_ref[POOL_MAXW - k:POOL_MAXW - k + rows, cs]

    mix, ffn = {}, {}
    _interleave(
        (_pool_group_steps(ext_read, h, pos, wp_ref, sc_ref[...], mix), POOL_STEP_COST),
        (_ffn_steps(res_ref[...], gfn_ref, w1_ref, w3_ref, w2_ref, gf_ref, ffn,
                    final=final, col_chunks=FFN_COL_CHUNKS), FFN_STEP_COST))
    z_ref[...] = x + mix["y"]
    o_ref[...] = ffn["out"]

    @pl.when(live)
    def _():
        ext_ref[0:POOL_MAXW, :] = ext_ref[rows:rows + POOL_MAXW, :]

    @pl.when(jnp.logical_and(live, t == n_tiles - 1))
    def _():
        ctx_ref[...] = ext_ref[rows + 1:rows + POOL_MAXW, :]


def _gla_ffn_body(x_ref, gm_ref, win_ref, wa2_ref, ba_ref, gn_ref, wo_ref, tri_ref, masks_ref,
                  gfn_ref, w1_ref, w3_ref, w2_ref, gf_ref,
                  o_ref, sout_ref, s_ref, z_ref, res_ref, *, levels, chunk, n_tiles, n_total, final):
    s = pl.program_id(0)
    t = jnp.minimum(s, n_total - 1) % n_tiles
    live = s < n_total

    @pl.when(s == 0)
    def _():
        z_ref[...] = jnp.zeros(z_ref.shape, F32)

    @pl.when(t == 0)
    def _():
        s_ref[...] = jnp.zeros(s_ref.shape, F32)

    res_ref[...] = z_ref[...]

    states = [s_ref[hd] for hd in range(GLA_HEADS)]
    mix, ffn = {}, {}
    _interleave(
        (_gla_prompt_steps(x_ref[...], states, gm_ref, win_ref, wa2_ref, ba_ref, gn_ref,
                           wo_ref, tri_ref, masks_ref, mix, levels=levels, chunk=chunk), GLA_STEP_COST),
        (_ffn_steps(res_ref[...], gfn_ref, w1_ref, w3_ref, w2_ref, gf_ref, ffn,
                    final=final, col_chunks=FFN_COL_CHUNKS), FFN_STEP_COST))
    states = mix["states"]
    z_ref[...] = mix["z"]
    o_ref[...] = ffn["out"]

    @pl.when(live)
    def _():
        for hd in range(GLA_HEADS):
            s_ref[hd] = states[hd]

    @pl.when(jnp.logical_and(live, t == n_tiles - 1))
    def _():
        for hd in range(GLA_HEADS):
            sout_ref[hd] = states[hd]


def _prompt_layer(kind, x, mixer_consts, ffn_consts, *, batch, seq, final, slot, n_slots, prev):
    n, d = x.shape
    rows = PROMPT_ROWS
    n_tiles = seq // rows
    n_total = batch * n_tiles
    assert seq % rows == 0 and rows >= POOL_MAXW and rows % GLA_CHUNK == 0
    dff = ffn_consts[1].shape[1]
    resident = _nbytes(mixer_consts) + _nbytes(ffn_consts) + 6 * rows * d * 4 + 3 * rows * dff * 4
    x_spec = pl.BlockSpec((rows, d), lambda s: (jnp.minimum(s, n_total - 1), 0))
    o_spec = pl.BlockSpec((rows, d), lambda s: (jnp.maximum(s - 1, 0), 0))
    seq_of = lambda s: jnp.minimum(s, n_total - 1) // n_tiles
    carry = [pltpu.VMEM((rows, d), F32), pltpu.VMEM((rows, d), F32)]
    if kind == "pool":
        body = functools.partial(_pool_ffn_body, rows=rows, n_tiles=n_tiles, n_total=n_total, final=final)
        state_shape = (batch, POOL_BUF, d)
        state_block = (None, POOL_BUF, d)
        state_index = lambda s: (seq_of(s), 0, 0)
        scratch = [pltpu.VMEM((rows + POOL_MAXW, d), F32)] + carry
        resident += (rows + POOL_MAXW) * d * 4
        consts = tuple(mixer_consts) + tuple(ffn_consts)
    else:
        win, wa2, wo = mixer_consts[1], mixer_consts[2], mixer_consts[5]
        dk = wa2.shape[1]
        dkh, dvh = dk // GLA_HEADS, wo.shape[0] // GLA_HEADS
        tri, masks, levels = _gla_tables(rows, GLA_CHUNK, GLA_CHUNK)
        body = functools.partial(_gla_ffn_body, levels=levels, chunk=GLA_CHUNK, n_tiles=n_tiles,
                                 n_total=n_total, final=final)
        state_shape = (batch, GLA_HEADS, dkh, dvh)
        state_block = (None, GLA_HEADS, dkh, dvh)
        state_index = lambda s: (seq_of(s), 0, 0, 0)
        scratch = [pltpu.VMEM((GLA_HEADS, dkh, dvh), F32)] + carry
        tables = (_Const(tri), _Const(masks))
        resident += _nbytes(tables) + 3 * GLA_HEADS * dkh * dvh * 4
        resident += rows * (win.shape[1] * 4 + (levels + 4) * dk * 4)
        consts = tuple(mixer_consts) + tables + tuple(ffn_consts)
    return _call_with_state_slot(
        body, (x,) + _arrays(consts), [x_spec] + _specs(consts),
        grid=(n_total + 1,),
        out_shape=jax.ShapeDtypeStruct((n, d), F32), out_spec=o_spec,
        state_shape=state_shape, state_block=state_block, state_index=state_index,
        slot=slot, n_slots=n_slots, prev=prev,
        scratch_shapes=scratch, resident=resident,
        name=kind + ("_ffn_final" if final else "_ffn") + "_prompt")


def kernel(x_prompt, x_sample, state_pool, state_gla, norm_mix, norm_ffn, norm_final, pool_w, pool_scale,
           gla_w_in, gla_w_a2, gla_b_a, gla_norm, gla_w_o, ffn_w1, ffn_w3, ffn_w2):
    batch, seq, d = x_prompt.shape
    bs, steps, _ = x_sample.shape
    depth = norm_mix.shape[0]

    xp = x_prompt.reshape(batch * seq, d)
    xs = x_sample.reshape(bs * steps, d)

    rows_of = lambda a: a.reshape(a.shape[0], 1, -1)
    w1, w3, w2 = ffn_w1.astype(BF16), ffn_w3.astype(BF16), ffn_w2.astype(BF16)
    w_pool, w_in, w_a2, w_o = (a.astype(BF16) for a in (pool_w, gla_w_in, gla_w_a2, gla_w_o))
    g_mix, g_ffn, sc_pool, b_a, g_gla = (rows_of(a) for a in (norm_mix, norm_ffn, pool_scale, gla_b_a, gla_norm))
    gf = _Const(norm_final.reshape(1, -1))

    n_pool, n_gla = state_pool.shape[0], state_gla.shape[0]
    new_pool_p = new_pool_s = new_gla_p = new_gla_s = None
    for i in range(depth):
        j = i // 2
        final = i == depth - 1
        g = _Const(g_mix, i)
        ffn_consts = (_Const(g_ffn, i), _Const(w1, i), _Const(w3, i), _Const(w2, i), gf)
        if i % 2 == 0:
            pool_consts = (g, _Const(w_pool, j), _Const(sc_pool, j))
            xp, new_pool_p = _prompt_layer("pool", xp, pool_consts, ffn_consts, batch=batch, seq=seq,
                                           final=final, slot=j, n_slots=n_pool, prev=new_pool_p)
            xs3, new_pool_s = _pool_sample(xs.reshape(bs, steps, d), state_pool, j, pool_consts, new_pool_s)
            xs = xs3.reshape(bs * steps, d)
        else:
            gla_consts = (g, _Const(w_in, j), _Const(w_a2, j), _Const(b_a, j), _Const(g_gla, j), _Const(w_o, j))
            xp, new_gla_p = _prompt_layer("gla", xp, gla_consts, ffn_consts, batch=batch, seq=seq,
                                          final=final, slot=j, n_slots=n_gla, prev=new_gla_p)
            xs, new_gla_s = _gla_sample(xs, state_gla, j, gla_consts, new_gla_s, steps=steps)
        xs = _ffn(xs, ffn_consts, final=final)

    return (xp.reshape(batch, seq, d), xs.reshape(bs, steps, d),
            new_pool_p, new_gla_p, new_pool_s, new_gla_s)
```

```python
import functools

import numpy as np
import jax
import jax.numpy as jnp
from jax import lax
from jax.experimental import pallas as pl
from jax.experimental.pallas import tpu as pltpu

F32 = jnp.float32
BF16 = jnp.bfloat16

PAST_LEN = 16384
POOL_WINDOWS = (2, 4, 8, 16)
POOL_MAXW = max(POOL_WINDOWS)
POOL_BUF = POOL_MAXW - 1
GLA_HEADS = 4
GLA_GATE_RANK = 16
GLA_GATE_TEMP = 16.0
EPS = 1e-6
LOG2_E = 1.4426950408889634

V7X_LANES = 128
V7X_SUBLANES = 8
V7X_VMEM_BYTES = 64 * 1024 * 1024

POOL_ROWS = 512
GLA_ROWS = 256
GLA_CHUNK = 128
FFN_ROWS = 512
POOL_SEQS = 16
GLA_SEQS = 8
FFN_COL_CHUNKS = (1024, 1024, 768)
GLA_LEVEL_COST = 300
GLA_SCORE_COST = 300
GLA_FINISH_COST = 200
GLA_SCORE_LEAD = 2
FFN_STEP_COST = 8448
GLA_STEP_COST = 5882 + 7 * GLA_LEVEL_COST + 8 * (GLA_SCORE_COST + GLA_FINISH_COST)
POOL_STEP_COST = sum(100 + 70 * (w - 1) for w in POOL_WINDOWS)


def _vmem_limit(resident_bytes):
    return int(min(V7X_VMEM_BYTES - (4 << 20), 2 * resident_bytes + (16 << 20)))


class _Const:
    def __init__(self, array, layer=None):
        self.array = array
        self.layer = layer
        self.shape = tuple(array.shape if layer is None else array.shape[1:])
        self.nbytes = int(np.prod(self.shape)) * array.dtype.itemsize

    def spec(self):
        nd = len(self.shape)
        if self.layer is None:
            return pl.BlockSpec(self.shape, lambda *_: (0,) * nd, pipeline_mode=pl.Buffered(1))
        layer = self.layer
        return pl.BlockSpec((None,) + self.shape, lambda *_: (layer,) + (0,) * nd, pipeline_mode=pl.Buffered(1))


def _nbytes(consts):
    return sum(c.nbytes for c in consts)


def _arrays(consts):
    return tuple(c.array for c in consts)


def _specs(consts):
    return [c.spec() for c in consts]


def _call_with_state_slot(body, args, in_specs, *, grid, out_shape, out_spec, state_shape, state_block,
                          state_index, slot, n_slots, prev, scratch_shapes, resident, name):
    n_args = len(args)
    inner = body
    aliases = {}
    if prev is None:
        state_spec = pl.BlockSpec((n_slots,) + tuple(state_block), lambda *g: (0,) + tuple(state_index(*g)))
        resident += 2 * (n_slots - 1) * int(np.prod([b for b in state_block if b is not None])) * 4

        def body(*refs):
            refs = list(refs)
            full = refs[n_args + 1]
            for other in range(n_slots):
                if other != slot:
                    full[other] = jnp.zeros(full.shape[1:], F32)
            refs[n_args + 1] = full.at[slot]
            return inner(*refs)
    else:
        state_spec = pl.BlockSpec((None,) + tuple(state_block), lambda *g: (slot,) + tuple(state_index(*g)))

        def body(*refs):
            return inner(*refs[:n_args], *refs[n_args + 1:])

        args = tuple(args) + (prev,)
        in_specs = list(in_specs) + [pl.BlockSpec(memory_space=pl.ANY)]
        aliases = {n_args: 1}
    return pl.pallas_call(
        body,
        out_shape=(out_shape, jax.ShapeDtypeStruct((n_slots,) + tuple(state_shape), F32)),
        grid=grid,
        in_specs=in_specs,
        out_specs=(out_spec, state_spec),
        scratch_shapes=scratch_shapes,
        input_output_aliases=aliases,
        compiler_params=pltpu.CompilerParams(
            dimension_semantics=("arbitrary",) * len(grid), vmem_limit_bytes=_vmem_limit(resident)),
        name=name,
    )(*args)


def _rmsnorm(x, g):
    ms = jnp.mean(x * x, axis=-1, keepdims=True)
    return x * lax.rsqrt(ms + EPS) * g


def _silu(x):
    hx = 0.5 * x
    return hx + hx * jnp.tanh(hx)


def _dot(a, b):
    return jnp.dot(a, b, preferred_element_type=F32)


def _dot_nt(a, b):
    return lax.dot_general(a, b, (((1,), (1,)), ((), ())), preferred_element_type=F32)


def _dot_tn(a, b):
    return lax.dot_general(a, b, (((0,), (0,)), ((), ())), preferred_element_type=F32)


def _split3(x):
    hi = x.astype(BF16)
    r1 = x - hi.astype(F32)
    mid = r1.astype(BF16)
    lo = (r1 - mid.astype(F32)).astype(BF16)
    return hi, mid, lo


def _ffn_math(x, g_ref, w1_ref, w3_ref, w2_ref, gf_ref, final):
    h = _rmsnorm(x, g_ref[...]).astype(BF16)
    u1 = _dot(h, w1_ref[...])
    u3 = _dot(h, w3_ref[...])
    act = (_silu(u1) * u3).astype(BF16)
    out = x + _dot(act, w2_ref[...])
    if final:
        out = _rmsnorm(out, gf_ref[...])
    return out


def _ffn_body(x_ref, g_ref, w1_ref, w3_ref, w2_ref, gf_ref, o_ref, *, final):
    o_ref[...] = _ffn_math(x_ref[...], g_ref, w1_ref, w3_ref, w2_ref, gf_ref, final)


def _ffn(x, ffn_consts, *, final):
    n, d = x.shape
    dff = ffn_consts[1].shape[1]
    rows = min(FFN_ROWS, n)
    assert n % rows == 0
    resident = _nbytes(ffn_consts) + 4 * rows * d * 4 + 3 * rows * dff * 4
    return pl.pallas_call(
        functools.partial(_ffn_body, final=final),
        out_shape=jax.ShapeDtypeStruct((n, d), F32),
        grid=(n // rows,),
        in_specs=[pl.BlockSpec((rows, d), lambda i: (i, 0))] + _specs(ffn_consts),
        out_specs=pl.BlockSpec((rows, d), lambda i: (i, 0)),
        compiler_params=pltpu.CompilerParams(
            dimension_semantics=("arbitrary",), vmem_limit_bytes=_vmem_limit(resident)),
        name="ffn_final" if final else "ffn",
    )(x, *_arrays(ffn_consts))


def _pool_group_steps(ext_read, h, pos, w_ref, sc, result):
    gw = h.shape[-1] // len(POOL_WINDOWS)
    ys = []
    for g, w in enumerate(POOL_WINDOWS):
        cs = slice(g * gw, (g + 1) * gw)
        hg = h[:, cs]
        s = hg
        for k in range(1, w):
            s = s + ext_read(k, cs)
        cnt = jnp.minimum(pos + 1, w).astype(F32)
        diff = (s / cnt - hg).astype(BF16)
        ys.append(_dot(diff, w_ref[g]))
        yield 100 + 70 * (w - 1)
    result["y"] = jnp.concatenate(ys, axis=-1) * sc


def _pool_groups(ext_read, h, pos, w_ref, sc):
    result = {}
    for _ in _pool_group_steps(ext_read, h, pos, w_ref, sc, result):
        pass
    return result["y"]


def _pool_sample_body(x_ref, ctx_ref, g_ref, w_ref, sc_ref, o_ref, nctx_ref, ext_ref, *, seqs, steps):
    d = x_ref.shape[-1]
    x = x_ref[...]
    h = _rmsnorm(x, g_ref[...])
    ext_ref[:, 0:1, :] = jnp.zeros((seqs, 1, d), F32)
    ext_ref[:, 1:POOL_MAXW, :] = ctx_ref[...]
    ext_ref[:, POOL_MAXW:POOL_MAXW + steps, :] = h
    pos = PAST_LEN + lax.broadcasted_iota(jnp.int32, (seqs * steps, 1), 0) % steps
    h2 = h.reshape(seqs * steps, d)

    def ext_read(k, cs):
        return ext_ref[:, POOL_MAXW - k:POOL_MAXW - k + steps, cs].reshape(seqs * steps, cs.stop - cs.start)

    y = _pool_groups(ext_read, h2, pos, w_ref, sc_ref[...])
    o_ref[...] = x + y.reshape(seqs, steps, d)
    nctx_ref[...] = ext_ref[:, steps + 1:steps + POOL_MAXW, :]


def _pool_sample(x, state_pool, layer, pool_consts, prev):
    bs, steps, d = x.shape
    seqs = POOL_SEQS
    assert bs % seqs == 0 and steps == V7X_SUBLANES
    resident = 2 * seqs * (2 * steps + 2 * POOL_MAXW) * d * 4 + seqs * (steps + POOL_MAXW) * d * 4
    return _call_with_state_slot(
        functools.partial(_pool_sample_body, seqs=seqs, steps=steps),
        (x, state_pool) + _arrays(pool_consts),
        [pl.BlockSpec((seqs, steps, d), lambda i: (i, 0, 0)),
         pl.BlockSpec((None, seqs, POOL_BUF, d), lambda i: (layer, i, 0, 0))]
        + _specs(pool_consts),
        grid=(bs // seqs,),
        out_shape=jax.ShapeDtypeStruct((bs, steps, d), F32),
        out_spec=pl.BlockSpec((seqs, steps, d), lambda i: (i, 0, 0)),
        state_shape=(bs, POOL_BUF, d), state_block=(seqs, POOL_BUF, d), state_index=lambda i: (i, 0, 0),
        slot=layer, n_slots=state_pool.shape[0], prev=prev,
        scratch_shapes=[pltpu.VMEM((seqs, steps + POOL_MAXW, d), F32)],
        resident=resident, name="pool_sample")


def _gla_tables(rows, seg, score_rows):
    levels = int(np.log2(seg))
    assert 2 ** levels == seg and rows % score_rows == 0 and score_rows % seg == 0
    idx = np.arange(rows)
    tri = (idx[None, :] >= (idx[:, None] // seg) * seg) & (idx[None, :] <= idx[:, None])
    row = np.arange(score_rows)[:, None]
    col = np.arange(score_rows)[None, :]
    masks = [row == col]
    for l in range(1, levels + 1):
        s = 2 ** l
        mid = (row // s) * s + s // 2 - 1
        masks.append((row // s == col // s) & (row > mid) & (col <= mid))
    masks = np.stack(masks, axis=0).astype(np.float32)
    return jnp.asarray(tri.astype(np.float32), BF16), jnp.asarray(masks, BF16), levels


def _gla_front(x, g_ref, win_ref, wa2_ref, ba_ref, tri_ref):
    rank, dk = wa2_ref.shape
    dkh = dk // GLA_HEADS
    h = _rmsnorm(x, g_ref[...]).astype(BF16)
    a_low = _dot(h, win_ref[:, 6 * dk:6 * dk + rank])
    xg = _dot(a_low.astype(BF16), wa2_ref[...]) + ba_ref[...]
    la = -(jnp.maximum(-xg, 0.0) + jnp.log(1.0 + jnp.exp(-jnp.abs(xg)))) * (1.0 / GLA_GATE_TEMP)
    hi, mid, lo = _split3(la)
    tri = tri_ref[...]
    b = _dot(tri, hi) + _dot(tri, mid) + _dot(tri, lo)
    proj = _dot(h, win_ref[:, 0:6 * dk])
    q = proj[:, 0:dk] * (dkh ** -0.5)
    k = proj[:, dk:2 * dk]
    v = proj[:, 2 * dk:4 * dk]
    r = proj[:, 4 * dk:6 * dk]
    return q, k, v, r, b


def _gla_level_operand_steps(q16, k16, b, levels, ops):
    rows, dk = b.shape
    sub = lax.broadcasted_iota(jnp.int32, (1, V7X_SUBLANES, 1), 1)
    b = b * LOG2_E
    for l in range(1, levels + 1):
        s = 2 ** l
        if s >= 2 * V7X_SUBLANES:
            b4 = b.reshape(rows // s, 2, s // 2, dk)
            m = b4[:, 0:1, s // 2 - 1:s // 2, :]
            x = jnp.concatenate([m - b4[:, 0:1], b4[:, 1:2] - m], axis=1)
        else:
            b3 = b.reshape(rows // V7X_SUBLANES, V7X_SUBLANES, dk)
            picks = [b3[:, j + s // 2 - 1:j + s // 2, :] for j in range(0, V7X_SUBLANES, s)]
            m = picks[-1]
            for idx in range(len(picks) - 2, -1, -1):
                m = jnp.where(sub < (idx + 1) * s, picks[idx], m)
            sign = jnp.where(((sub >> (l - 1)) & 1) == 1, 1.0, -1.0)
            x = (b3 - m) * sign
        e = jnp.exp2(x).reshape(rows, dk).astype(BF16)
        ops.append((q16 * e, k16 * e))
        yield GLA_LEVEL_COST


def _gla_level_operands(q16, k16, b, levels):
    ops = []
    for _ in _gla_level_operand_steps(q16, k16, b, levels, ops):
        pass
    return ops


def _gla_intra(qb16_h, kb16_h, ops, masks_ref, rs, hs):
    a = _dot_nt(qb16_h, kb16_h).astype(BF16) * masks_ref[0]
    for l, (ql, kl) in enumerate(ops, start=1):
        a = a + _dot_nt(ql[rs, hs], kl[rs, hs]).astype(BF16) * masks_ref[l]
    return a


def _lane_column(row_vec):
    return jnp.transpose(row_vec, (1, 0))


def _gla_out(x, o, r, gn_ref, wo_ref):
    dv = o.shape[-1]
    dvh = dv // GLA_HEADS
    normed = []
    for hd in range(GLA_HEADS):
        oh = o[:, hd * dvh:(hd + 1) * dvh]
        normed.append(oh * lax.rsqrt(jnp.mean(oh * oh, axis=-1, keepdims=True) + EPS))
    o = jnp.concatenate(normed, axis=-1) * gn_ref[...]
    o = (o * _silu(r)).astype(BF16)
    return x + _dot(o, wo_ref[...])


def _gla_prompt_steps(x, states, g_ref, win_ref, wa2_ref, ba_ref, gn_ref, wo_ref,
                      tri_ref, masks_ref, result, *, levels, chunk):
    rows = x.shape[0]
    rank, dk = wa2_ref.shape
    dkh = dk // GLA_HEADS
    h = _rmsnorm(x, g_ref[...]).astype(BF16)
    a_low = _dot(h, win_ref[:, 6 * dk:6 * dk + rank])
    yield 150
    q = _dot(h, win_ref[:, 0:dk]) * (dkh ** -0.5)
    q16 = q.astype(BF16)
    yield 512
    xg = _dot(a_low.astype(BF16), wa2_ref[...]) + ba_ref[...]
    yield 150
    k = _dot(h, win_ref[:, dk:2 * dk])
    k16 = k.astype(BF16)
    yield 512
    la = -(jnp.maximum(-xg, 0.0) + jnp.log(1.0 + jnp.exp(-jnp.abs(xg)))) * (1.0 / GLA_GATE_TEMP)
    hi, mid, lo = _split3(la)
    yield 400
    vb = _dot(h, win_ref[:, 2 * dk:4 * dk]).astype(BF16)
    yield 1024
    dvh = vb.shape[-1] // GLA_HEADS
    tri = tri_ref[...]
    b = _dot(tri, hi) + _dot(tri, mid) + _dot(tri, lo)
    yield 350
    r = _dot(h, win_ref[:, 4 * dk:6 * dk])
    yield 1024
    ops = []
    yield from _gla_level_operand_steps(q16, k16, b, levels, ops)
    qb = (q * jnp.exp(b)).astype(BF16)
    yield 60

    states = list(states)
    pieces = [(ci, hd) for ci in range(rows // chunk) for hd in range(GLA_HEADS)]
    kends = {}
    outs = {}

    def scores(ci, hd):
        rs = slice(ci * chunk, (ci + 1) * chunk)
        hs = slice(hd * dkh, (hd + 1) * dkh)
        return _gla_intra(q16[rs, hs], k16[rs, hs], ops, masks_ref, rs, hs)

    def finish(ci, hd, a16):
        rs = slice(ci * chunk, (ci + 1) * chunk)
        hs = slice(hd * dkh, (hd + 1) * dkh)
        vs = slice(hd * dvh, (hd + 1) * dvh)
        b_c = b[rs]
        b_last = b_c[chunk - 1:chunk, :]
        if ci not in kends:
            kends[ci] = (k[rs] * jnp.exp(b_last - b_c)).astype(BF16)
        s_old = states[hd]
        outs[ci, hd] = _dot(a16, vb[rs, vs]) + _dot(qb[rs, hs], s_old.astype(BF16))
        dec = jnp.exp(_lane_column(b_last[:, hs]))
        states[hd] = dec * s_old + _dot_tn(kends[ci][:, hs], vb[rs, vs])

    pending = []
    for piece in pieces:
        pending.append((*piece, scores(*piece)))
        yield GLA_SCORE_COST
        if len(pending) > GLA_SCORE_LEAD:
            finish(*pending.pop(0))
            yield GLA_FINISH_COST
    while pending:
        finish(*pending.pop(0))
        yield GLA_FINISH_COST
    o = jnp.concatenate(
        [jnp.concatenate([outs[ci, hd] for hd in range(GLA_HEADS)], axis=-1) for ci in range(rows // chunk)],
        axis=0)
    result["z"] = _gla_out(x, o, r, gn_ref, wo_ref)
    result["states"] = states
    yield 1700


def _ffn_steps(x, h, w1_ref, w3_ref, w2_ref, gf_ref, result, *, final, col_chunks):
    acc = x
    lo = 0
    pending = None
    for cols in col_chunks:
        cs = slice(lo, lo + cols)
        lo += cols
        u1 = _dot(h, w1_ref[:, cs])
        yield cols
        u3 = _dot(h, w3_ref[:, cs])
        yield cols
        if pending is not None:
            acc = acc + _dot(pending[0], w2_ref[pending[1], :])
            yield pending[1].stop - pending[1].start
        pending = ((_silu(u1) * u3).astype(BF16), cs)
    acc = acc + _dot(pending[0], w2_ref[pending[1], :])
    yield pending[1].stop - pending[1].start
    if final:
        acc = _rmsnorm(acc, gf_ref[...])
    result["out"] = acc


def _interleave(*pairs):
    gens = [p[0] for p in pairs]
    totals = [float(p[1]) for p in pairs]
    done = [0.0] * len(gens)
    live = list(range(len(gens)))
    while live:
        i = min(live, key=lambda j: done[j] / totals[j])
        try:
            done[i] += next(gens[i])
        except StopIteration:
            live.remove(i)


def _gla_sample_body(x_ref, s0_ref, g_ref, win_ref, wa2_ref, ba_ref, gn_ref, wo_ref,
                     tri_ref, masks_ref, o_ref, sout_ref, *, levels, seqs, steps):
    x = x_ref[...]
    rows = x.shape[0]
    q, k, v, r, b = _gla_front(x, g_ref, win_ref, wa2_ref, ba_ref, tri_ref)
    dk = q.shape[-1]
    dkh = dk // GLA_HEADS
    dvh = v.shape[-1] // GLA_HEADS
    q16 = q.astype(BF16)
    k16 = k.astype(BF16)
    ops = _gla_level_operands(q16, k16, b, levels)
    b3 = b.reshape(seqs, steps, dk)
    b_last = b3[:, steps - 1:steps, :]
    qb = (q * jnp.exp(b)).astype(BF16)
    kend = (k.reshape(seqs, steps, dk) * jnp.exp(b_last - b3)).reshape(rows, dk).astype(BF16)
    vb = v.astype(BF16)
    all_rows = slice(0, rows)
    outs = []
    for hd in range(GLA_HEADS):
        hs = slice(hd * dkh, (hd + 1) * dkh)
        vs = slice(hd * dvh, (hd + 1) * dvh)
        a = _gla_intra(q16[:, hs], k16[:, hs], ops, masks_ref, all_rows, hs)
        o_h = _dot(a, vb[:, vs])
        o_state = []
        for s in range(seqs):
            rs = slice(s * steps, (s + 1) * steps)
            s_old = s0_ref[s, hd]
            o_state.append(_dot(qb[rs, hs], s_old.astype(BF16)))
            dec = jnp.exp(_lane_column(b[(s + 1) * steps - 1:(s + 1) * steps, hs]))
            sout_ref[s, hd] = dec * s_old + _dot_tn(kend[rs, hs], vb[rs, vs])
        outs.append(o_h + jnp.concatenate(o_state, axis=0))
    o_ref[...] = _gla_out(x, jnp.concatenate(outs, axis=-1), r, gn_ref, wo_ref)


def _gla_sample(x, state_gla, layer, gla_consts, prev, *, steps):
    n, d = x.shape
    seqs = GLA_SEQS
    rows = seqs * steps
    bs = n // steps
    assert bs % seqs == 0
    win, wa2, wo = gla_consts[1], gla_consts[2], gla_consts[5]
    dk = wa2.shape[1]
    dkh, dvh = dk // GLA_HEADS, wo.shape[0] // GLA_HEADS
    tri, masks, levels = _gla_tables(rows, steps, rows)
    consts = tuple(gla_consts) + (_Const(tri), _Const(masks))
    state_block = seqs * GLA_HEADS * dkh * dvh * 4
    resident = _nbytes(consts) + 4 * rows * d * 4 + 4 * state_block
    resident += rows * (win.shape[1] * 4 + (levels + 4) * dk * 4)
    return _call_with_state_slot(
        functools.partial(_gla_sample_body, levels=levels, seqs=seqs, steps=steps),
        (x, state_gla) + _arrays(consts),
        [pl.BlockSpec((rows, d), lambda i: (i, 0)),
         pl.BlockSpec((None, seqs, GLA_HEADS, dkh, dvh), lambda i: (layer, i, 0, 0, 0))]
        + _specs(consts),
        grid=(bs // seqs,),
        out_shape=jax.ShapeDtypeStruct((n, d), F32),
        out_spec=pl.BlockSpec((rows, d), lambda i: (i, 0)),
        state_shape=(bs, GLA_HEADS, dkh, dvh), state_block=(seqs, GLA_HEADS, dkh, dvh),
        state_index=lambda i: (i, 0, 0, 0),
        slot=layer, n_slots=state_gla.shape[0], prev=prev,
        scratch_shapes=[], resident=resident, name="gla_sample")


def _pool_ffn_body(x_ref, gm_ref, wp_ref, sc_ref, gfn_ref, w1_ref, w3_ref, w2_ref, gf_ref,
                   o_ref, ctx_ref, ext_ref, z_ref, hz_ref, *, rows, n_tiles, n_total, final):
    s = pl.program_id(0)
    d = x_ref.shape[-1]
    t = jnp.minimum(s, n_total - 1) % n_tiles
    live = s < n_total

    @pl.when(s == 0)
    def _():
        z_ref[...] = jnp.zeros(z_ref.shape, F32)
        hz_ref[...] = jnp.zeros(hz_ref.shape, BF16)

    @pl.when(t == 0)
    def _():
        ext_ref[0:POOL_MAXW, :] = jnp.zeros((POOL_MAXW, d), F32)


    x = x_ref[...]
    h = _rmsnorm(x, gm_ref[...])
    ext_ref[POOL_MAXW:POOL_MAXW + rows, :] = h
    pos = t * rows + lax.broadcasted_iota(jnp.int32, (rows, 1), 0)

    def ext_read(k, cs):
        return ext_ref[POOL_MAXW - k:POOL_MAXW - k + rows, cs]

    mix, ffn = {}, {}
    _interleave(
        (_pool_group_steps(ext_read, h, pos, wp_ref, sc_ref[...], mix), POOL_STEP_COST),
        (_ffn_steps(z_ref[...], hz_ref[...], w1_ref, w3_ref, w2_ref, gf_ref, ffn,
                    final=final, col_chunks=FFN_COL_CHUNKS), FFN_STEP_COST))
    z = x + mix["y"]
    z_ref[...] = z
    hz_ref[...] = _rmsnorm(z, gfn_ref[...]).astype(BF16)
    o_ref[...] = ffn["out"]

    @pl.when(live)
    def _():
        ext_ref[0:POOL_MAXW, :] = ext_ref[rows:rows + POOL_MAXW, :]

    @pl.when(jnp.logical_and(live, t == n_tiles - 1))
    def _():
        ctx_ref[...] = ext_ref[rows + 1:rows + POOL_MAXW, :]


def _gla_ffn_body(x_ref, gm_ref, win_ref, wa2_ref, ba_ref, gn_ref, wo_ref, tri_ref, masks_ref,
                  gfn_ref, w1_ref, w3_ref, w2_ref, gf_ref,
                  o_ref, sout_ref, s_ref, z_ref, hz_ref, *, levels, chunk, n_tiles, n_total, final):
    s = pl.program_id(0)
    t = jnp.minimum(s, n_total - 1) % n_tiles
    live = s < n_total

    @pl.when(s == 0)
    def _():
        z_ref[...] = jnp.zeros(z_ref.shape, F32)
        hz_ref[...] = jnp.zeros(hz_ref.shape, BF16)

    @pl.when(t == 0)
    def _():
        s_ref[...] = jnp.zeros(s_ref.shape, F32)


    states = [s_ref[hd] for hd in range(GLA_HEADS)]
    mix, ffn = {}, {}
    _interleave(
        (_ffn_steps(z_ref[...], hz_ref[...], w1_ref, w3_ref, w2_ref, gf_ref, ffn,
                    final=final, col_chunks=FFN_COL_CHUNKS), FFN_STEP_COST),
        (_gla_prompt_steps(x_ref[...], states, gm_ref, win_ref, wa2_ref, ba_ref, gn_ref,
                           wo_ref, tri_ref, masks_ref, mix, levels=levels, chunk=chunk), GLA_STEP_COST))
    states = mix["states"]
    z_ref[...] = mix["z"]
    hz_ref[...] = _rmsnorm(mix["z"], gfn_ref[...]).astype(BF16)
    o_ref[...] = ffn["out"]

    @pl.when(live)
    def _():
        for hd in range(GLA_HEADS):
            s_ref[hd] = states[hd]

    @pl.when(jnp.logical_and(live, t == n_tiles - 1))
    def _():
        for hd in range(GLA_HEADS):
            sout_ref[hd] = states[hd]


def _prompt_layer(kind, x, mixer_consts, ffn_consts, *, batch, seq, final, slot, n_slots, prev):
    n, d = x.shape
    rows = POOL_ROWS if kind == "pool" else GLA_ROWS
    n_tiles = seq // rows
    n_total = batch * n_tiles
    assert seq % rows == 0 and rows >= POOL_MAXW and rows % GLA_CHUNK == 0
    dff = ffn_consts[1].shape[1]
    resident = _nbytes(mixer_consts) + _nbytes(ffn_consts) + 6 * rows * d * 4 + 3 * rows * dff * 4
    x_spec = pl.BlockSpec((rows, d), lambda s: (jnp.minimum(s, n_total - 1), 0))
    o_spec = pl.BlockSpec((rows, d), lambda s: (jnp.maximum(s - 1, 0), 0))
    seq_of = lambda s: jnp.minimum(s, n_total - 1) // n_tiles
    carry = [pltpu.VMEM((rows, d), F32), pltpu.VMEM((rows, d), BF16)]
    if kind == "pool":
        body = functools.partial(_pool_ffn_body, rows=rows, n_tiles=n_tiles, n_total=n_total, final=final)
        state_shape = (batch, POOL_BUF, d)
        state_block = (None, POOL_BUF, d)
        state_index = lambda s: (seq_of(s), 0, 0)
        scratch = [pltpu.VMEM((rows + POOL_MAXW, d), F32)] + carry
        resident += (rows + POOL_MAXW) * d * 4
        consts = tuple(mixer_consts) + tuple(ffn_consts)
    else:
        win, wa2, wo = mixer_consts[1], mixer_consts[2], mixer_consts[5]
        dk = wa2.shape[1]
        dkh, dvh = dk // GLA_HEADS, wo.shape[0] // GLA_HEADS
        tri, masks, levels = _gla_tables(rows, GLA_CHUNK, GLA_CHUNK)
        body = functools.partial(_gla_ffn_body, levels=levels, chunk=GLA_CHUNK, n_tiles=n_tiles,
                                 n_total=n_total, final=final)
        state_shape = (batch, GLA_HEADS, dkh, dvh)
        state_block = (None, GLA_HEADS, dkh, dvh)
        state_index = lambda s: (seq_of(s), 0, 0, 0)
        scratch = [pltpu.VMEM((GLA_HEADS, dkh, dvh), F32)] + carry
        tables = (_Const(tri), _Const(masks))
        resident += _nbytes(tables) + 3 * GLA_HEADS * dkh * dvh * 4
        resident += rows * (win.shape[1] * 4 + (levels + 4) * dk * 4)
        consts = tuple(mixer_consts) + tables + tuple(ffn_consts)
    return _call_with_state_slot(
        body, (x,) + _arrays(consts), [x_spec] + _specs(consts),
        grid=(n_total + 1,),
        out_shape=jax.ShapeDtypeStruct((n, d), F32), out_spec=o_spec,
        state_shape=state_shape, state_block=state_block, state_index=state_index,
        slot=slot, n_slots=n_slots, prev=prev,
        scratch_shapes=scratch, resident=resident,
        name=kind + ("_ffn_final" if final else "_ffn") + "_prompt")


def kernel(x_prompt, x_sample, state_pool, state_gla, norm_mix, norm_ffn, norm_final, pool_w, pool_scale,
           gla_w_in, gla_w_a2, gla_b_a, gla_norm, gla_w_o, ffn_w1, ffn_w3, ffn_w2):
    batch, seq, d = x_prompt.shape
    bs, steps, _ = x_sample.shape
    depth = norm_mix.shape[0]

    xp = x_prompt.reshape(batch * seq, d)
    xs = x_sample.reshape(bs * steps, d)

    rows_of = lambda a: a.reshape(a.shape[0], 1, -1)
    w1, w3, w2 = ffn_w1.astype(BF16), ffn_w3.astype(BF16), ffn_w2.astype(BF16)
    w_pool, w_in, w_a2, w_o = (a.astype(BF16) for a in (pool_w, gla_w_in, gla_w_a2, gla_w_o))
    g_mix, g_ffn, sc_pool, b_a, g_gla = (rows_of(a) for a in (norm_mix, norm_ffn, pool_scale, gla_b_a, gla_norm))
    gf = _Const(norm_final.reshape(1, -1))

    n_pool, n_gla = state_pool.shape[0], state_gla.shape[0]
    new_pool_p = new_pool_s = new_gla_p = new_gla_s = None
    for i in range(depth):
        j = i // 2
        final = i == depth - 1
        g = _Const(g_mix, i)
        ffn_consts = (_Const(g_ffn, i), _Const(w1, i), _Const(w3, i), _Const(w2, i), gf)
        if i % 2 == 0:
            pool_consts = (g, _Const(w_pool, j), _Const(sc_pool, j))
            xp, new_pool_p = _prompt_layer("pool", xp, pool_consts, ffn_consts, batch=batch, seq=seq,
                                           final=final, slot=j, n_slots=n_pool, prev=new_pool_p)
            xs3, new_pool_s = _pool_sample(xs.reshape(bs, steps, d), state_pool, j, pool_consts, new_pool_s)
            xs = xs3.reshape(bs * steps, d)
        else:
            gla_consts = (g, _Const(w_in, j), _Const(w_a2, j), _Const(b_a, j), _Const(g_gla, j), _Const(w_o, j))
            xp, new_gla_p = _prompt_layer("gla", xp, gla_consts, ffn_consts, batch=batch, seq=seq,
                                          final=final, slot=j, n_slots=n_gla, prev=new_gla_p)
            xs, new_gla_s = _gla_sample(xs, state_gla, j, gla_consts, new_gla_s, steps=steps)
        xs = _ffn(xs, ffn_consts, final=final)

    return (xp.reshape(batch, seq, d), xs.reshape(bs, steps, d),
            new_pool_p, new_gla_p, new_pool_s, new_gla_s)
```

```python
import functools

import numpy as np
import jax
import jax.numpy as jnp
from jax import lax
from jax.experimental import pallas as pl
from jax.experimental.pallas import tpu as pltpu

F32 = jnp.float32
BF16 = jnp.bfloat16

PAST_LEN = 16384
POOL_WINDOWS = (2, 4, 8, 16)
POOL_MAXW = max(POOL_WINDOWS)
POOL_BUF = POOL_MAXW - 1
GLA_HEADS = 4
GLA_GATE_RANK = 16
GLA_GATE_TEMP = 16.0
EPS = 1e-6
LOG2_E = 1.4426950408889634

V7X_LANES = 128
V7X_SUBLANES = 8
V7X_VMEM_BYTES = 64 * 1024 * 1024

PROMPT_ROWS = 256
GLA_CHUNK = 128
FFN_ROWS = 512
POOL_SEQS = 16
GLA_SEQS = 8
FFN_COL_CHUNKS = (1024, 1024, 768)
GLA_LEVEL_COST = 300
GLA_SCORE_COST = 300
GLA_FINISH_COST = 200
GLA_SCORE_LEAD = 2
FFN_STEP_COST = 8648
GLA_STEP_COST = 5882 + 7 * GLA_LEVEL_COST + 8 * (GLA_SCORE_COST + GLA_FINISH_COST)
POOL_STEP_COST = 2220


def _vmem_limit(resident_bytes):
    return int(min(V7X_VMEM_BYTES - (4 << 20), 2 * resident_bytes + (16 << 20)))


class _Const:
    def __init__(self, array, layer=None):
        self.array = array
        self.layer = layer
        self.shape = tuple(array.shape if layer is None else array.shape[1:])
        self.nbytes = int(np.prod(self.shape)) * array.dtype.itemsize

    def spec(self):
        nd = len(self.shape)
        if self.layer is None:
            return pl.BlockSpec(self.shape, lambda *_: (0,) * nd, pipeline_mode=pl.Buffered(1))
        layer = self.layer
        return pl.BlockSpec((None,) + self.shape, lambda *_: (layer,) + (0,) * nd, pipeline_mode=pl.Buffered(1))


def _nbytes(consts):
    return sum(c.nbytes for c in consts)


def _arrays(consts):
    return tuple(c.array for c in consts)


def _specs(consts):
    return [c.spec() for c in consts]


def _call_with_state_slot(body, args, in_specs, *, grid, out_shape, out_spec, state_shape, state_block,
                          state_index, slot, n_slots, prev, scratch_shapes, resident, name):
    n_args = len(args)
    inner = body
    aliases = {}
    if prev is None:
        state_spec = pl.BlockSpec((n_slots,) + tuple(state_block), lambda *g: (0,) + tuple(state_index(*g)))
        resident += 2 * (n_slots - 1) * int(np.prod([b for b in state_block if b is not None])) * 4

        def body(*refs):
            refs = list(refs)
            full = refs[n_args + 1]
            for other in range(n_slots):
                if other != slot:
                    full[other] = jnp.zeros(full.shape[1:], F32)
            refs[n_args + 1] = full.at[slot]
            return inner(*refs)
    else:
        state_spec = pl.BlockSpec((None,) + tuple(state_block), lambda *g: (slot,) + tuple(state_index(*g)))

        def body(*refs):
            return inner(*refs[:n_args], *refs[n_args + 1:])

        args = tuple(args) + (prev,)
        in_specs = list(in_specs) + [pl.BlockSpec(memory_space=pl.ANY)]
        aliases = {n_args: 1}
    return pl.pallas_call(
        body,
        out_shape=(out_shape, jax.ShapeDtypeStruct((n_slots,) + tuple(state_shape), F32)),
        grid=grid,
        in_specs=in_specs,
        out_specs=(out_spec, state_spec),
        scratch_shapes=scratch_shapes,
        input_output_aliases=aliases,
        compiler_params=pltpu.CompilerParams(
            dimension_semantics=("arbitrary",) * len(grid), vmem_limit_bytes=_vmem_limit(resident)),
        name=name,
    )(*args)


def _rmsnorm(x, g):
    ms = jnp.mean(x * x, axis=-1, keepdims=True)
    return x * lax.rsqrt(ms + EPS) * g


def _silu(x):
    hx = 0.5 * x
    return hx + hx * jnp.tanh(hx)


def _dot(a, b):
    return jnp.dot(a, b, preferred_element_type=F32)


def _dot_nt(a, b):
    return lax.dot_general(a, b, (((1,), (1,)), ((), ())), preferred_element_type=F32)


def _dot_tn(a, b):
    return lax.dot_general(a, b, (((0,), (0,)), ((), ())), preferred_element_type=F32)


def _split3(x):
    hi = x.astype(BF16)
    r1 = x - hi.astype(F32)
    mid = r1.astype(BF16)
    lo = (r1 - mid.astype(F32)).astype(BF16)
    return hi, mid, lo


def _ffn_math(x, g_ref, w1_ref, w3_ref, w2_ref, gf_ref, final):
    h = _rmsnorm(x, g_ref[...]).astype(BF16)
    u1 = _dot(h, w1_ref[...])
    u3 = _dot(h, w3_ref[...])
    act = (_silu(u1) * u3).astype(BF16)
    out = x + _dot(act, w2_ref[...])
    if final:
        out = _rmsnorm(out, gf_ref[...])
    return out


def _ffn_body(x_ref, g_ref, w1_ref, w3_ref, w2_ref, gf_ref, o_ref, *, final):
    o_ref[...] = _ffn_math(x_ref[...], g_ref, w1_ref, w3_ref, w2_ref, gf_ref, final)


def _ffn(x, ffn_consts, *, final):
    n, d = x.shape
    dff = ffn_consts[1].shape[1]
    rows = min(FFN_ROWS, n)
    assert n % rows == 0
    resident = _nbytes(ffn_consts) + 4 * rows * d * 4 + 3 * rows * dff * 4
    return pl.pallas_call(
        functools.partial(_ffn_body, final=final),
        out_shape=jax.ShapeDtypeStruct((n, d), F32),
        grid=(n // rows,),
        in_specs=[pl.BlockSpec((rows, d), lambda i: (i, 0))] + _specs(ffn_consts),
        out_specs=pl.BlockSpec((rows, d), lambda i: (i, 0)),
        compiler_params=pltpu.CompilerParams(
            dimension_semantics=("arbitrary",), vmem_limit_bytes=_vmem_limit(resident)),
        name="ffn_final" if final else "ffn",
    )(x, *_arrays(ffn_consts))


def _pool_group_steps(ext_read, h, pos, w_ref, sc, result):
    gw = h.shape[-1] // len(POOL_WINDOWS)
    ys = []
    for g, w in enumerate(POOL_WINDOWS):
        cs = slice(g * gw, (g + 1) * gw)
        hg = h[:, cs]
        s = hg
        for k in range(1, w):
            s = s + ext_read(k, cs)
        cnt = jnp.minimum(pos + 1, w).astype(F32)
        diff = (s / cnt - hg).astype(BF16)
        ys.append(_dot(diff, w_ref[g]))
        yield 100 + 70 * (w - 1)
    result["y"] = jnp.concatenate(ys, axis=-1) * sc


def _pool_groups(ext_read, h, pos, w_ref, sc):
    result = {}
    for _ in _pool_group_steps(ext_read, h, pos, w_ref, sc, result):
        pass
    return result["y"]


def _pool_sample_body(x_ref, ctx_ref, g_ref, w_ref, sc_ref, o_ref, nctx_ref, ext_ref, *, seqs, steps):
    d = x_ref.shape[-1]
    x = x_ref[...]
    h = _rmsnorm(x, g_ref[...])
    ext_ref[:, 0:1, :] = jnp.zeros((seqs, 1, d), F32)
    ext_ref[:, 1:POOL_MAXW, :] = ctx_ref[...]
    ext_ref[:, POOL_MAXW:POOL_MAXW + steps, :] = h
    pos = PAST_LEN + lax.broadcasted_iota(jnp.int32, (seqs * steps, 1), 0) % steps
    h2 = h.reshape(seqs * steps, d)

    def ext_read(k, cs):
        return ext_ref[:, POOL_MAXW - k:POOL_MAXW - k + steps, cs].reshape(seqs * steps, cs.stop - cs.start)

    y = _pool_groups(ext_read, h2, pos, w_ref, sc_ref[...])
    o_ref[...] = x + y.reshape(seqs, steps, d)
    nctx_ref[...] = ext_ref[:, steps + 1:steps + POOL_MAXW, :]


def _pool_sample(x, state_pool, layer, pool_consts, prev):
    bs, steps, d = x.shape
    seqs = POOL_SEQS
    assert bs % seqs == 0 and steps == V7X_SUBLANES
    resident = 2 * seqs * (2 * steps + 2 * POOL_MAXW) * d * 4 + seqs * (steps + POOL_MAXW) * d * 4
    return _call_with_state_slot(
        functools.partial(_pool_sample_body, seqs=seqs, steps=steps),
        (x, state_pool) + _arrays(pool_consts),
        [pl.BlockSpec((seqs, steps, d), lambda i: (i, 0, 0)),
         pl.BlockSpec((None, seqs, POOL_BUF, d), lambda i: (layer, i, 0, 0))]
        + _specs(pool_consts),
        grid=(bs // seqs,),
        out_shape=jax.ShapeDtypeStruct((bs, steps, d), F32),
        out_spec=pl.BlockSpec((seqs, steps, d), lambda i: (i, 0, 0)),
        state_shape=(bs, POOL_BUF, d), state_block=(seqs, POOL_BUF, d), state_index=lambda i: (i, 0, 0),
        slot=layer, n_slots=state_pool.shape[0], prev=prev,
        scratch_shapes=[pltpu.VMEM((seqs, steps + POOL_MAXW, d), F32)],
        resident=resident, name="pool_sample")


def _gla_tables(rows, seg, score_rows):
    levels = int(np.log2(seg))
    assert 2 ** levels == seg and rows % score_rows == 0 and score_rows % seg == 0
    idx = np.arange(rows)
    tri = (idx[None, :] >= (idx[:, None] // seg) * seg) & (idx[None, :] <= idx[:, None])
    row = np.arange(score_rows)[:, None]
    col = np.arange(score_rows)[None, :]
    masks = [row == col]
    for l in range(1, levels + 1):
        s = 2 ** l
        mid = (row // s) * s + s // 2 - 1
        masks.append((row // s == col // s) & (row > mid) & (col <= mid))
    masks = np.stack(masks, axis=0).astype(np.float32)
    return jnp.asarray(tri.astype(np.float32), BF16), jnp.asarray(masks, F32), levels


def _gla_front(x, g_ref, win_ref, wa2_ref, ba_ref, tri_ref):
    rank, dk = wa2_ref.shape
    dkh = dk // GLA_HEADS
    h = _rmsnorm(x, g_ref[...]).astype(BF16)
    a_low = _dot(h, win_ref[:, 6 * dk:6 * dk + rank])
    xg = _dot(a_low.astype(BF16), wa2_ref[...]) + ba_ref[...]
    la = -(jnp.maximum(-xg, 0.0) + jnp.log(1.0 + jnp.exp(-jnp.abs(xg)))) * (1.0 / GLA_GATE_TEMP)
    hi, mid, lo = _split3(la)
    tri = tri_ref[...]
    b = _dot(tri, hi) + _dot(tri, mid) + _dot(tri, lo)
    proj = _dot(h, win_ref[:, 0:6 * dk])
    q = proj[:, 0:dk] * (dkh ** -0.5)
    k = proj[:, dk:2 * dk]
    v = proj[:, 2 * dk:4 * dk]
    r = proj[:, 4 * dk:6 * dk]
    return q, k, v, r, b


def _gla_level_operand_steps(q16, k16, b, levels, ops):
    rows, dk = b.shape
    row = lax.broadcasted_iota(jnp.int32, (rows, 1), 0)
    sub = lax.broadcasted_iota(jnp.int32, (1, V7X_SUBLANES, 1), 1)
    b = b * LOG2_E
    for l in range(1, levels + 1):
        s = 2 ** l
        if s >= V7X_SUBLANES:
            b3 = b.reshape(rows // s, s, dk)
            m = b3[:, s // 2 - 1:s // 2, :]
        else:
            b3 = b.reshape(rows // V7X_SUBLANES, V7X_SUBLANES, dk)
            picks = [b3[:, j + s // 2 - 1:j + s // 2, :] for j in range(0, V7X_SUBLANES, s)]
            m = picks[-1]
            for idx in range(len(picks) - 2, -1, -1):
                m = jnp.where(sub < (idx + 1) * s, picks[idx], m)
        e = jnp.exp2(-jnp.abs(b3 - m)).reshape(rows, dk).astype(BF16)
        second = ((row >> (l - 1)) & 1) == 1
        ops.append(jnp.where(second, q16, k16) * e)
        yield GLA_LEVEL_COST


def _gla_level_operands(q16, k16, b, levels):
    ops = []
    for _ in _gla_level_operand_steps(q16, k16, b, levels, ops):
        pass
    return ops


def _gla_intra(qb16_h, kb16_h, ops, masks_ref, rs, hs):
    a = _dot_nt(qb16_h, kb16_h) * masks_ref[0]
    for l, t in enumerate(ops, start=1):
        th = t[rs, hs]
        a = a + _dot_nt(th, th) * masks_ref[l]
    return a


def _lane_column(row_vec):
    return jnp.transpose(row_vec, (1, 0))


def _gla_out(x, o, r, gn_ref, wo_ref):
    dv = o.shape[-1]
    dvh = dv // GLA_HEADS
    normed = []
    for hd in range(GLA_HEADS):
        oh = o[:, hd * dvh:(hd + 1) * dvh]
        normed.append(oh * lax.rsqrt(jnp.mean(oh * oh, axis=-1, keepdims=True) + EPS))
    o = jnp.concatenate(normed, axis=-1) * gn_ref[...]
    o = (o * _silu(r)).astype(BF16)
    return x + _dot(o, wo_ref[...])


def _gla_prompt_steps(x, states, g_ref, win_ref, wa2_ref, ba_ref, gn_ref, wo_ref,
                      tri_ref, masks_ref, result, *, levels, chunk):
    rows = x.shape[0]
    rank, dk = wa2_ref.shape
    dkh = dk // GLA_HEADS
    h = _rmsnorm(x, g_ref[...]).astype(BF16)
    a_low = _dot(h, win_ref[:, 6 * dk:6 * dk + rank])
    yield 150
    q = _dot(h, win_ref[:, 0:dk]) * (dkh ** -0.5)
    q16 = q.astype(BF16)
    yield 512
    xg = _dot(a_low.astype(BF16), wa2_ref[...]) + ba_ref[...]
    yield 150
    k = _dot(h, win_ref[:, dk:2 * dk])
    k16 = k.astype(BF16)
    yield 512
    la = -(jnp.maximum(-xg, 0.0) + jnp.log(1.0 + jnp.exp(-jnp.abs(xg)))) * (1.0 / GLA_GATE_TEMP)
    hi, mid, lo = _split3(la)
    yield 400
    vb = _dot(h, win_ref[:, 2 * dk:4 * dk]).astype(BF16)
    yield 1024
    dvh = vb.shape[-1] // GLA_HEADS
    tri = tri_ref[...]
    b = _dot(tri, hi) + _dot(tri, mid) + _dot(tri, lo)
    yield 350
    r = _dot(h, win_ref[:, 4 * dk:6 * dk])
    yield 1024
    ops = []
    yield from _gla_level_operand_steps(q16, k16, b, levels, ops)
    qb = (q * jnp.exp(b)).astype(BF16)
    yield 60

    states = list(states)
    pieces = [(ci, hd) for ci in range(rows // chunk) for hd in range(GLA_HEADS)]
    kends = {}
    outs = {}

    def scores(ci, hd):
        rs = slice(ci * chunk, (ci + 1) * chunk)
        hs = slice(hd * dkh, (hd + 1) * dkh)
        return _gla_intra(q16[rs, hs], k16[rs, hs], ops, masks_ref, rs, hs).astype(BF16)

    def finish(ci, hd, a16):
        rs = slice(ci * chunk, (ci + 1) * chunk)
        hs = slice(hd * dkh, (hd + 1) * dkh)
        vs = slice(hd * dvh, (hd + 1) * dvh)
        b_c = b[rs]
        b_last = b_c[chunk - 1:chunk, :]
        if ci not in kends:
            kends[ci] = (k[rs] * jnp.exp(b_last - b_c)).astype(BF16)
        s_old = states[hd]
        outs[ci, hd] = _dot(a16, vb[rs, vs]) + _dot(qb[rs, hs], s_old.astype(BF16))
        dec = jnp.exp(_lane_column(b_last[:, hs]))
        states[hd] = dec * s_old + _dot_tn(kends[ci][:, hs], vb[rs, vs])

    pending = []
    for piece in pieces:
        pending.append((*piece, scores(*piece)))
        yield GLA_SCORE_COST
        if len(pending) > GLA_SCORE_LEAD:
            finish(*pending.pop(0))
            yield GLA_FINISH_COST
    while pending:
        finish(*pending.pop(0))
        yield GLA_FINISH_COST
    o = jnp.concatenate(
        [jnp.concatenate([outs[ci, hd] for hd in range(GLA_HEADS)], axis=-1) for ci in range(rows // chunk)],
        axis=0)
    result["z"] = _gla_out(x, o, r, gn_ref, wo_ref)
    result["states"] = states
    yield 1700


def _ffn_steps(x, g_ref, w1_ref, w3_ref, w2_ref, gf_ref, result, *, final, col_chunks):
    rows = x.shape[0]
    h = _rmsnorm(x, g_ref[...]).astype(BF16)
    yield 200
    acc = x
    lo = 0
    pending = None
    for cols in col_chunks:
        cs = slice(lo, lo + cols)
        lo += cols
        u1 = _dot(h, w1_ref[:, cs])
        yield rows * cols // 256
        u3 = _dot(h, w3_ref[:, cs])
        yield rows * cols // 256
        if pending is not None:
            acc = acc + _dot(pending[0], w2_ref[pending[1], :])
            yield rows * (pending[1].stop - pending[1].start) // 256
        pending = ((_silu(u1) * u3).astype(BF16), cs)
    acc = acc + _dot(pending[0], w2_ref[pending[1], :])
    yield rows * (pending[1].stop - pending[1].start) // 256
    if final:
        acc = _rmsnorm(acc, gf_ref[...])
    result["out"] = acc


def _interleave(*pairs):
    gens = [p[0] for p in pairs]
    totals = [float(p[1]) for p in pairs]
    done = [0.0] * len(gens)
    live = list(range(len(gens)))
    while live:
        i = min(live, key=lambda j: done[j] / totals[j])
        try:
            done[i] += next(gens[i])
        except StopIteration:
            live.remove(i)


def _gla_sample_body(x_ref, s0_ref, g_ref, win_ref, wa2_ref, ba_ref, gn_ref, wo_ref,
                     tri_ref, masks_ref, o_ref, sout_ref, *, levels, seqs, steps):
    x = x_ref[...]
    rows = x.shape[0]
    q, k, v, r, b = _gla_front(x, g_ref, win_ref, wa2_ref, ba_ref, tri_ref)
    dk = q.shape[-1]
    dkh = dk // GLA_HEADS
    dvh = v.shape[-1] // GLA_HEADS
    q16 = q.astype(BF16)
    k16 = k.astype(BF16)
    ops = _gla_level_operands(q16, k16, b, levels)
    b3 = b.reshape(seqs, steps, dk)
    b_last = b3[:, steps - 1:steps, :]
    qb = (q * jnp.exp(b)).astype(BF16)
    kend = (k.reshape(seqs, steps, dk) * jnp.exp(b_last - b3)).reshape(rows, dk).astype(BF16)
    vb = v.astype(BF16)
    all_rows = slice(0, rows)
    outs = []
    for hd in range(GLA_HEADS):
        hs = slice(hd * dkh, (hd + 1) * dkh)
        vs = slice(hd * dvh, (hd + 1) * dvh)
        a = _gla_intra(q16[:, hs], k16[:, hs], ops, masks_ref, all_rows, hs)
        o_h = _dot(a.astype(BF16), vb[:, vs])
        o_state = []
        for s in range(seqs):
            rs = slice(s * steps, (s + 1) * steps)
            s_old = s0_ref[s, hd]
            o_state.append(_dot(qb[rs, hs], s_old.astype(BF16)))
            dec = jnp.exp(_lane_column(b[(s + 1) * steps - 1:(s + 1) * steps, hs]))
            sout_ref[s, hd] = dec * s_old + _dot_tn(kend[rs, hs], vb[rs, vs])
        outs.append(o_h + jnp.concatenate(o_state, axis=0))
    o_ref[...] = _gla_out(x, jnp.concatenate(outs, axis=-1), r, gn_ref, wo_ref)


def _gla_sample(x, state_gla, layer, gla_consts, prev, *, steps):
    n, d = x.shape
    seqs = GLA_SEQS
    rows = seqs * steps
    bs = n // steps
    assert bs % seqs == 0
    win, wa2, wo = gla_consts[1], gla_consts[2], gla_consts[5]
    dk = wa2.shape[1]
    dkh, dvh = dk // GLA_HEADS, wo.shape[0] // GLA_HEADS
    tri, masks, levels = _gla_tables(rows, steps, rows)
    consts = tuple(gla_consts) + (_Const(tri), _Const(masks))
    state_block = seqs * GLA_HEADS * dkh * dvh * 4
    resident = _nbytes(consts) + 4 * rows * d * 4 + 4 * state_block
    resident += rows * (win.shape[1] * 4 + (levels + 4) * dk * 4)
    return _call_with_state_slot(
        functools.partial(_gla_sample_body, levels=levels, seqs=seqs, steps=steps),
        (x, state_gla) + _arrays(consts),
        [pl.BlockSpec((rows, d), lambda i: (i, 0)),
         pl.BlockSpec((None, seqs, GLA_HEADS, dkh, dvh), lambda i: (layer, i, 0, 0, 0))]
        + _specs(consts),
        grid=(bs // seqs,),
        out_shape=jax.ShapeDtypeStruct((n, d), F32),
        out_spec=pl.BlockSpec((rows, d), lambda i: (i, 0)),
        state_shape=(bs, GLA_HEADS, dkh, dvh), state_block=(seqs, GLA_HEADS, dkh, dvh),
        state_index=lambda i: (i, 0, 0, 0),
        slot=layer, n_slots=state_gla.shape[0], prev=prev,
        scratch_shapes=[], resident=resident, name="gla_sample")


def _pool_ffn_body(x_ref, gm_ref, wp_ref, sc_ref, gfn_ref, w1_ref, w3_ref, w2_ref, gf_ref,
                   o_ref, ctx_ref, ext_ref, z_ref, *, rows, n_tiles, n_total, final):
    s = pl.program_id(0)
    d = x_ref.shape[-1]
    t = jnp.minimum(s, n_total - 1) % n_tiles

    def ffn_steps(ffn):
        return (_ffn_steps(z_ref[...], gfn_ref, w1_ref, w3_ref, w2_ref, gf_ref, ffn,
                           final=final, col_chunks=FFN_COL_CHUNKS), FFN_STEP_COST)

    def mixer_step(with_ffn):
        @pl.when(t == 0)
        def _():
            ext_ref[0:POOL_MAXW, :] = jnp.zeros((POOL_MAXW, d), F32)

        x = x_ref[...]
        h = _rmsnorm(x, gm_ref[...])
        ext_ref[POOL_MAXW:POOL_MAXW + rows, :] = h
        pos = t * rows + lax.broadcasted_iota(jnp.int32, (rows, 1), 0)

        def ext_read(k, cs):
            return ext_ref[POOL_MAXW - k:POOL_MAXW - k + rows, cs]

        mix, ffn = {}, {}
        streams = [(_pool_group_steps(ext_read, h, pos, wp_ref, sc_ref[...], mix), POOL_STEP_COST)]
        if with_ffn:
            streams.append(ffn_steps(ffn))
        _interleave(*streams)
        z_ref[...] = x + mix["y"]
        if with_ffn:
            o_ref[...] = ffn["out"]
        ext_ref[0:POOL_MAXW, :] = ext_ref[rows:rows + POOL_MAXW, :]

        @pl.when(t == n_tiles - 1)
        def _():
            ctx_ref[...] = ext_ref[rows + 1:rows + POOL_MAXW, :]

    @pl.when(s == 0)
    def _():
        mixer_step(False)

    @pl.when(jnp.logical_and(s > 0, s < n_total))
    def _():
        mixer_step(True)

    @pl.when(s == n_total)
    def _():
        ffn = {}
        _interleave(ffn_steps(ffn))
        o_ref[...] = ffn["out"]


def _gla_ffn_body(x_ref, gm_ref, win_ref, wa2_ref, ba_ref, gn_ref, wo_ref, tri_ref, masks_ref,
                  gfn_ref, w1_ref, w3_ref, w2_ref, gf_ref,
                  o_ref, sout_ref, s_ref, z_ref, *, levels, chunk, n_tiles, n_total, final):
    s = pl.program_id(0)
    t = jnp.minimum(s, n_total - 1) % n_tiles

    def ffn_steps(ffn):
        return (_ffn_steps(z_ref[...], gfn_ref, w1_ref, w3_ref, w2_ref, gf_ref, ffn,
                           final=final, col_chunks=FFN_COL_CHUNKS), FFN_STEP_COST)

    def mixer_step(with_ffn):
        @pl.when(t == 0)
        def _():
            s_ref[...] = jnp.zeros(s_ref.shape, F32)

        states = [s_ref[hd] for hd in range(GLA_HEADS)]
        mix, ffn = {}, {}
        streams = [(_gla_prompt_steps(x_ref[...], states, gm_ref, win_ref, wa2_ref, ba_ref, gn_ref,
                                      wo_ref, tri_ref, masks_ref, mix, levels=levels, chunk=chunk), GLA_STEP_COST)]
        if with_ffn:
            streams.append(ffn_steps(ffn))
        _interleave(*streams)
        states = mix["states"]
        z_ref[...] = mix["z"]
        if with_ffn:
            o_ref[...] = ffn["out"]
        for hd in range(GLA_HEADS):
            s_ref[hd] = states[hd]

        @pl.when(t == n_tiles - 1)
        def _():
            for hd in range(GLA_HEADS):
                sout_ref[hd] = states[hd]

    @pl.when(s == 0)
    def _():
        mixer_step(False)

    @pl.when(jnp.logical_and(s > 0, s < n_total))
    def _():
        mixer_step(True)

    @pl.when(s == n_total)
    def _():
        ffn = {}
        _interleave(ffn_steps(ffn))
        o_ref[...] = ffn["out"]


def _prompt_layer(kind, x, mixer_consts, ffn_consts, *, batch, seq, final, slot, n_slots, prev):
    n, d = x.shape
    rows = PROMPT_ROWS
    n_tiles = seq // rows
    n_total = batch * n_tiles
    assert seq % rows == 0 and rows >= POOL_MAXW and rows % GLA_CHUNK == 0
    dff = ffn_consts[1].shape[1]
    resident = _nbytes(mixer_consts) + _nbytes(ffn_consts) + 6 * rows * d * 4 + 3 * rows * dff * 4
    x_spec = pl.BlockSpec((rows, d), lambda s: (jnp.minimum(s, n_total - 1), 0))
    o_spec = pl.BlockSpec((rows, d), lambda s: (jnp.maximum(s - 1, 0), 0))
    seq_of = lambda s: jnp.minimum(s, n_total - 1) // n_tiles
    carry = [pltpu.VMEM((rows, d), F32)]
    if kind == "pool":
        body = functools.partial(_pool_ffn_body, rows=rows, n_tiles=n_tiles, n_total=n_total, final=final)
        state_shape = (batch, POOL_BUF, d)
        state_block = (None, POOL_BUF, d)
        state_index = lambda s: (seq_of(s), 0, 0)
        scratch = [pltpu.VMEM((rows + POOL_MAXW, d), F32)] + carry
        resident += (rows + POOL_MAXW) * d * 4
        consts = tuple(mixer_consts) + tuple(ffn_consts)
    else:
        win, wa2, wo = mixer_consts[1], mixer_consts[2], mixer_consts[5]
        dk = wa2.shape[1]
        dkh, dvh = dk // GLA_HEADS, wo.shape[0] // GLA_HEADS
        tri, masks, levels = _gla_tables(rows, GLA_CHUNK, GLA_CHUNK)
        body = functools.partial(_gla_ffn_body, levels=levels, chunk=GLA_CHUNK, n_tiles=n_tiles,
                                 n_total=n_total, final=final)
        state_shape = (batch, GLA_HEADS, dkh, dvh)
        state_block = (None, GLA_HEADS, dkh, dvh)
        state_index = lambda s: (seq_of(s), 0, 0, 0)
        scratch = [pltpu.VMEM((GLA_HEADS, dkh, dvh), F32)] + carry
        tables = (_Const(tri), _Const(masks))
        resident += _nbytes(tables) + 3 * GLA_HEADS * dkh * dvh * 4
        resident += rows * (win.shape[1] * 4 + (levels + 4) * dk * 4)
        consts = tuple(mixer_consts) + tables + tuple(ffn_consts)
    return _call_with_state_slot(
        body, (x,) + _arrays(consts), [x_spec] + _specs(consts),
        grid=(n_total + 1,),
        out_shape=jax.ShapeDtypeStruct((n, d), F32), out_spec=o_spec,
        state_shape=state_shape, state_block=state_block, state_index=state_index,
        slot=slot, n_slots=n_slots, prev=prev,
        scratch_shapes=scratch, resident=resident,
        name=kind + ("_ffn_final" if final else "_ffn") + "_prompt")


def kernel(x_prompt, x_sample, state_pool, state_gla, norm_mix, norm_ffn, norm_final, pool_w, pool_scale,
           gla_w_in, gla_w_a2, gla_b_a, gla_norm, gla_w_o, ffn_w1, ffn_w3, ffn_w2):
    batch, seq, d = x_prompt.shape
    bs, steps, _ = x_sample.shape
    depth = norm_mix.shape[0]

    xp = x_prompt.reshape(batch * seq, d)
    xs = x_sample.reshape(bs * steps, d)

    rows_of = lambda a: a.reshape(a.shape[0], 1, -1)
    w1, w3, w2 = ffn_w1.astype(BF16), ffn_w3.astype(BF16), ffn_w2.astype(BF16)
    w_pool, w_in, w_a2, w_o = (a.astype(BF16) for a in (pool_w, gla_w_in, gla_w_a2, gla_w_o))
    g_mix, g_ffn, sc_pool, b_a, g_gla = (rows_of(a) for a in (norm_mix, norm_ffn, pool_scale, gla_b_a, gla_norm))
    gf = _Const(norm_final.reshape(1, -1))

    n_pool, n_gla = state_pool.shape[0], state_gla.shape[0]
    new_pool_p = new_pool_s = new_gla_p = new_gla_s = None
    for i in range(depth):
        j = i // 2
        final = i == depth - 1
        g = _Const(g_mix, i)
        ffn_consts = (_Const(g_ffn, i), _Const(w1, i), _Const(w3, i), _Const(w2, i), gf)
        if i % 2 == 0:
            pool_consts = (g, _Const(w_pool, j), _Const(sc_pool, j))
            xp, new_pool_p = _prompt_layer("pool", xp, pool_consts, ffn_consts, batch=batch, seq=seq,
                                           final=final, slot=j, n_slots=n_pool, prev=new_pool_p)
            xs3, new_pool_s = _pool_sample(xs.reshape(bs, steps, d), state_pool, j, pool_consts, new_pool_s)
            xs = xs3.reshape(bs * steps, d)
        else:
            gla_consts = (g, _Const(w_in, j), _Const(w_a2, j), _Const(b_a, j), _Const(g_gla, j), _Const(w_o, j))
            xp, new_gla_p = _prompt_layer("gla", xp, gla_consts, ffn_consts, batch=batch, seq=seq,
                                          final=final, slot=j, n_slots=n_gla, prev=new_gla_p)
            xs, new_gla_s = _gla_sample(xs, state_gla, j, gla_consts, new_gla_s, steps=steps)
        xs = _ffn(xs, ffn_consts, final=final)

    return (xp.reshape(batch, seq, d), xs.reshape(bs, steps, d),
            new_pool_p, new_gla_p, new_pool_s, new_gla_s)
```

```python
import functools

import numpy as np
import jax
import jax.numpy as jnp
from jax import lax
from jax.experimental import pallas as pl
from jax.experimental.pallas import tpu as pltpu

F32 = jnp.float32
BF16 = jnp.bfloat16

PAST_LEN = 16384
POOL_WINDOWS = (2, 4, 8, 16)
POOL_MAXW = max(POOL_WINDOWS)
POOL_BUF = POOL_MAXW - 1
GLA_HEADS = 4
GLA_GATE_RANK = 16
GLA_GATE_TEMP = 16.0
EPS = 1e-6
LOG2_E = 1.4426950408889634

V7X_LANES = 128
V7X_SUBLANES = 8
V7X_VMEM_BYTES = 64 * 1024 * 1024

PROMPT_ROWS = 256
GLA_CHUNK = 128
FFN_STREAM_COLS = 256
POOL_SEQS = 16
GLA_SEQS = 8
FFN_COL_CHUNKS = (1024, 1024, 768)
GLA_LEVEL_COST = 300
GLA_SCORE_COST = 300
GLA_FINISH_COST = 200
GLA_SCORE_LEAD = 2
FFN_STEP_COST = 8648
GLA_STEP_COST = 5882 + 7 * GLA_LEVEL_COST + 8 * (GLA_SCORE_COST + GLA_FINISH_COST)
POOL_STEP_COST = 2220


def _vmem_limit(resident_bytes):
    return int(min(V7X_VMEM_BYTES - (4 << 20), 2 * resident_bytes + (16 << 20)))


class _Const:
    def __init__(self, array, layer=None):
        self.array = array
        self.layer = layer
        self.shape = tuple(array.shape if layer is None else array.shape[1:])
        self.nbytes = int(np.prod(self.shape)) * array.dtype.itemsize

    def spec(self):
        nd = len(self.shape)
        if self.layer is None:
            return pl.BlockSpec(self.shape, lambda *_: (0,) * nd, pipeline_mode=pl.Buffered(1))
        layer = self.layer
        return pl.BlockSpec((None,) + self.shape, lambda *_: (layer,) + (0,) * nd, pipeline_mode=pl.Buffered(1))


def _nbytes(consts):
    return sum(c.nbytes for c in consts)


def _arrays(consts):
    return tuple(c.array for c in consts)


def _specs(consts):
    return [c.spec() for c in consts]


def _call_with_state_slot(body, args, in_specs, *, grid, out_shape, out_spec, state_shape, state_block,
                          state_index, slot, n_slots, prev, scratch_shapes, resident, name):
    n_args = len(args)
    inner = body
    aliases = {}
    if prev is None:
        state_spec = pl.BlockSpec((n_slots,) + tuple(state_block), lambda *g: (0,) + tuple(state_index(*g)))
        resident += 2 * (n_slots - 1) * int(np.prod([b for b in state_block if b is not None])) * 4

        def body(*refs):
            refs = list(refs)
            full = refs[n_args + 1]
            for other in range(n_slots):
                if other != slot:
                    full[other] = jnp.zeros(full.shape[1:], F32)
            refs[n_args + 1] = full.at[slot]
            return inner(*refs)
    else:
        state_spec = pl.BlockSpec((None,) + tuple(state_block), lambda *g: (slot,) + tuple(state_index(*g)))

        def body(*refs):
            return inner(*refs[:n_args], *refs[n_args + 1:])

        args = tuple(args) + (prev,)
        in_specs = list(in_specs) + [pl.BlockSpec(memory_space=pl.ANY)]
        aliases = {n_args: 1}
    return pl.pallas_call(
        body,
        out_shape=(out_shape, jax.ShapeDtypeStruct((n_slots,) + tuple(state_shape), F32)),
        grid=grid,
        in_specs=in_specs,
        out_specs=(out_spec, state_spec),
        scratch_shapes=scratch_shapes,
        input_output_aliases=aliases,
        compiler_params=pltpu.CompilerParams(
            dimension_semantics=("arbitrary",) * len(grid), vmem_limit_bytes=_vmem_limit(resident)),
        name=name,
    )(*args)


def _rmsnorm(x, g):
    ms = jnp.mean(x * x, axis=-1, keepdims=True)
    return x * lax.rsqrt(ms + EPS) * g


def _silu(x):
    hx = 0.5 * x
    return hx + hx * jnp.tanh(hx)


def _dot(a, b):
    return jnp.dot(a, b, preferred_element_type=F32)


def _dot_nt(a, b):
    return lax.dot_general(a, b, (((1,), (1,)), ((), ())), preferred_element_type=F32)


def _dot_tn(a, b):
    return lax.dot_general(a, b, (((0,), (0,)), ((), ())), preferred_element_type=F32)


def _split3(x):
    hi = x.astype(BF16)
    r1 = x - hi.astype(F32)
    mid = r1.astype(BF16)
    lo = (r1 - mid.astype(F32)).astype(BF16)
    return hi, mid, lo


def _ffn_cols_body(x_ref, g_ref, w1_ref, w3_ref, w2_ref, gf_ref, o_ref, h_ref, *, n_chunks, final):
    c = pl.program_id(0)

    @pl.when(c == 0)
    def _():
        x = x_ref[...]
        h_ref[...] = _rmsnorm(x, g_ref[...]).astype(BF16)
        o_ref[...] = x

    h = h_ref[...]
    act = (_silu(_dot(h, w1_ref[...])) * _dot(h, w3_ref[...])).astype(BF16)
    o_ref[...] += _dot(act, w2_ref[...])
    if final:
        @pl.when(c == n_chunks - 1)
        def _():
            o_ref[...] = _rmsnorm(o_ref[...], gf_ref[...])


def _ffn(x, ffn_consts, *, final):
    n, d = x.shape
    g, w1, w3, w2, gf = ffn_consts
    dff = w1.shape[1]
    cols = FFN_STREAM_COLS
    assert dff % cols == 0
    n_chunks = dff // cols
    lw = w1.layer
    resident = 2 * 3 * d * cols * 2 + 3 * n * d * 4 + n * d * 2 + 3 * n * cols * 4
    return pl.pallas_call(
        functools.partial(_ffn_cols_body, n_chunks=n_chunks, final=final),
        out_shape=jax.ShapeDtypeStruct((n, d), F32),
        grid=(n_chunks,),
        in_specs=[pl.BlockSpec((n, d), lambda c: (0, 0), pipeline_mode=pl.Buffered(1)), g.spec(),
                  pl.BlockSpec((None, d, cols), lambda c: (lw, 0, c)),
                  pl.BlockSpec((None, d, cols), lambda c: (lw, 0, c)),
                  pl.BlockSpec((None, cols, d), lambda c: (lw, c, 0)),
                  gf.spec()],
        out_specs=pl.BlockSpec((n, d), lambda c: (0, 0)),
        scratch_shapes=[pltpu.VMEM((n, d), BF16)],
        compiler_params=pltpu.CompilerParams(
            dimension_semantics=("arbitrary",), vmem_limit_bytes=_vmem_limit(resident)),
        name="ffn_final" if final else "ffn",
    )(x, g.array, w1.array, w3.array, w2.array, gf.array)


def _pool_group_steps(ext_read, h, pos, w_ref, sc, result):
    gw = h.shape[-1] // len(POOL_WINDOWS)
    ys = []
    for g, w in enumerate(POOL_WINDOWS):
        cs = slice(g * gw, (g + 1) * gw)
        hg = h[:, cs]
        s = hg
        for k in range(1, w):
            s = s + ext_read(k, cs)
        cnt = jnp.minimum(pos + 1, w).astype(F32)
        diff = (s / cnt - hg).astype(BF16)
        ys.append(_dot(diff, w_ref[g]))
        yield 100 + 70 * (w - 1)
    result["y"] = jnp.concatenate(ys, axis=-1) * sc


def _pool_groups(ext_read, h, pos, w_ref, sc):
    result = {}
    for _ in _pool_group_steps(ext_read, h, pos, w_ref, sc, result):
        pass
    return result["y"]


def _pool_sample_body(x_ref, ctx_ref, g_ref, w_ref, sc_ref, o_ref, nctx_ref, ext_ref, *, seqs, steps):
    d = x_ref.shape[-1]
    x = x_ref[...]
    h = _rmsnorm(x, g_ref[...])
    ext_ref[:, 0:1, :] = jnp.zeros((seqs, 1, d), F32)
    ext_ref[:, 1:POOL_MAXW, :] = ctx_ref[...]
    ext_ref[:, POOL_MAXW:POOL_MAXW + steps, :] = h
    pos = PAST_LEN + lax.broadcasted_iota(jnp.int32, (seqs * steps, 1), 0) % steps
    h2 = h.reshape(seqs * steps, d)

    def ext_read(k, cs):
        return ext_ref[:, POOL_MAXW - k:POOL_MAXW - k + steps, cs].reshape(seqs * steps, cs.stop - cs.start)

    y = _pool_groups(ext_read, h2, pos, w_ref, sc_ref[...])
    o_ref[...] = x + y.reshape(seqs, steps, d)
    nctx_ref[...] = ext_ref[:, steps + 1:steps + POOL_MAXW, :]


def _pool_sample(x, state_pool, layer, pool_consts, prev):
    bs, steps, d = x.shape
    seqs = POOL_SEQS
    assert bs % seqs == 0 and steps == V7X_SUBLANES
    resident = 2 * seqs * (2 * steps + 2 * POOL_MAXW) * d * 4 + seqs * (steps + POOL_MAXW) * d * 4
    return _call_with_state_slot(
        functools.partial(_pool_sample_body, seqs=seqs, steps=steps),
        (x, state_pool) + _arrays(pool_consts),
        [pl.BlockSpec((seqs, steps, d), lambda i: (i, 0, 0)),
         pl.BlockSpec((None, seqs, POOL_BUF, d), lambda i: (layer, i, 0, 0))]
        + _specs(pool_consts),
        grid=(bs // seqs,),
        out_shape=jax.ShapeDtypeStruct((bs, steps, d), F32),
        out_spec=pl.BlockSpec((seqs, steps, d), lambda i: (i, 0, 0)),
        state_shape=(bs, POOL_BUF, d), state_block=(seqs, POOL_BUF, d), state_index=lambda i: (i, 0, 0),
        slot=layer, n_slots=state_pool.shape[0], prev=prev,
        scratch_shapes=[pltpu.VMEM((seqs, steps + POOL_MAXW, d), F32)],
        resident=resident, name="pool_sample")


def _gla_tables(rows, seg, score_rows):
    levels = int(np.log2(seg))
    assert 2 ** levels == seg and rows % score_rows == 0 and score_rows % seg == 0
    idx = np.arange(rows)
    tri = (idx[None, :] >= (idx[:, None] // seg) * seg) & (idx[None, :] <= idx[:, None])
    row = np.arange(score_rows)[:, None]
    col = np.arange(score_rows)[None, :]
    masks = [row == col]
    for l in range(1, levels + 1):
        s = 2 ** l
        mid = (row // s) * s + s // 2 - 1
        masks.append((row // s == col // s) & (row > mid) & (col <= mid))
    masks = np.stack(masks, axis=0).astype(np.float32)
    return jnp.asarray(tri.astype(np.float32), BF16), jnp.asarray(masks, F32), levels


def _gla_front(x, g_ref, win_ref, wa2_ref, ba_ref, tri_ref):
    rank, dk = wa2_ref.shape
    dkh = dk // GLA_HEADS
    h = _rmsnorm(x, g_ref[...]).astype(BF16)
    a_low = _dot(h, win_ref[:, 6 * dk:6 * dk + rank])
    xg = _dot(a_low.astype(BF16), wa2_ref[...]) + ba_ref[...]
    la = -(jnp.maximum(-xg, 0.0) + jnp.log(1.0 + jnp.exp(-jnp.abs(xg)))) * (1.0 / GLA_GATE_TEMP)
    hi, mid, lo = _split3(la)
    tri = tri_ref[...]
    b = _dot(tri, hi) + _dot(tri, mid) + _dot(tri, lo)
    proj = _dot(h, win_ref[:, 0:6 * dk])
    q = proj[:, 0:dk] * (dkh ** -0.5)
    k = proj[:, dk:2 * dk]
    v = proj[:, 2 * dk:4 * dk]
    r = proj[:, 4 * dk:6 * dk]
    return q, k, v, r, b


def _gla_level_operand_steps(q16, k16, b, levels, ops):
    rows, dk = b.shape
    row = lax.broadcasted_iota(jnp.int32, (rows, 1), 0)
    sub = lax.broadcasted_iota(jnp.int32, (1, V7X_SUBLANES, 1), 1)
    b = b * LOG2_E
    for l in range(1, levels + 1):
        s = 2 ** l
        if s >= V7X_SUBLANES:
            b3 = b.reshape(rows // s, s, dk)
            m = b3[:, s // 2 - 1:s // 2, :]
        else:
            b3 = b.reshape(rows // V7X_SUBLANES, V7X_SUBLANES, dk)
            picks = [b3[:, j + s // 2 - 1:j + s // 2, :] for j in range(0, V7X_SUBLANES, s)]
            m = picks[-1]
            for idx in range(len(picks) - 2, -1, -1):
                m = jnp.where(sub < (idx + 1) * s, picks[idx], m)
        e = jnp.exp2(-jnp.abs(b3 - m)).reshape(rows, dk).astype(BF16)
        second = ((row >> (l - 1)) & 1) == 1
        ops.append(jnp.where(second, q16, k16) * e)
        yield GLA_LEVEL_COST


def _gla_level_operands(q16, k16, b, levels):
    ops = []
    for _ in _gla_level_operand_steps(q16, k16, b, levels, ops):
        pass
    return ops


def _gla_intra(qb16_h, kb16_h, ops, masks_ref, rs, hs):
    a = _dot_nt(qb16_h, kb16_h) * masks_ref[0]
    for l, t in enumerate(ops, start=1):
        th = t[rs, hs]
        a = a + _dot_nt(th, th) * masks_ref[l]
    return a


def _lane_column(row_vec):
    return jnp.transpose(row_vec, (1, 0))


def _gla_out(x, o, r, gn_ref, wo_ref):
    dv = o.shape[-1]
    dvh = dv // GLA_HEADS
    normed = []
    for hd in range(GLA_HEADS):
        oh = o[:, hd * dvh:(hd + 1) * dvh]
        normed.append(oh * lax.rsqrt(jnp.mean(oh * oh, axis=-1, keepdims=True) + EPS))
    o = jnp.concatenate(normed, axis=-1) * gn_ref[...]
    o = (o * _silu(r)).astype(BF16)
    return x + _dot(o, wo_ref[...])


def _gla_prompt_steps(x, states, g_ref, win_ref, wa2_ref, ba_ref, gn_ref, wo_ref,
                      tri_ref, masks_ref, result, *, levels, chunk):
    rows = x.shape[0]
    rank, dk = wa2_ref.shape
    dkh = dk // GLA_HEADS
    h = _rmsnorm(x, g_ref[...]).astype(BF16)
    a_low = _dot(h, win_ref[:, 6 * dk:6 * dk + rank])
    yield 150
    q = _dot(h, win_ref[:, 0:dk]) * (dkh ** -0.5)
    q16 = q.astype(BF16)
    yield 512
    xg = _dot(a_low.astype(BF16), wa2_ref[...]) + ba_ref[...]
    yield 150
    k = _dot(h, win_ref[:, dk:2 * dk])
    k16 = k.astype(BF16)
    yield 512
    la = -(jnp.maximum(-xg, 0.0) + jnp.log(1.0 + jnp.exp(-jnp.abs(xg)))) * (1.0 / GLA_GATE_TEMP)
    hi, mid, lo = _split3(la)
    yield 400
    vb = _dot(h, win_ref[:, 2 * dk:4 * dk]).astype(BF16)
    yield 1024
    dvh = vb.shape[-1] // GLA_HEADS
    tri = tri_ref[...]
    b = _dot(tri, hi) + _dot(tri, mid) + _dot(tri, lo)
    yield 350
    r = _dot(h, win_ref[:, 4 * dk:6 * dk])
    yield 1024
    ops = []
    yield from _gla_level_operand_steps(q16, k16, b, levels, ops)
    qb = (q * jnp.exp(b)).astype(BF16)
    yield 60

    states = list(states)
    pieces = [(ci, hd) for ci in range(rows // chunk) for hd in range(GLA_HEADS)]
    kends = {}
    outs = {}

    def scores(ci, hd):
        rs = slice(ci * chunk, (ci + 1) * chunk)
        hs = slice(hd * dkh, (hd + 1) * dkh)
        return _gla_intra(q16[rs, hs], k16[rs, hs], ops, masks_ref, rs, hs).astype(BF16)

    def finish(ci, hd, a16):
        rs = slice(ci * chunk, (ci + 1) * chunk)
        hs = slice(hd * dkh, (hd + 1) * dkh)
        vs = slice(hd * dvh, (hd + 1) * dvh)
        b_c = b[rs]
        b_last = b_c[chunk - 1:chunk, :]
        if ci not in kends:
            kends[ci] = (k[rs] * jnp.exp(b_last - b_c)).astype(BF16)
        s_old = states[hd]
        outs[ci, hd] = _dot(a16, vb[rs, vs]) + _dot(qb[rs, hs], s_old.astype(BF16))
        dec = jnp.exp(_lane_column(b_last[:, hs]))
        states[hd] = dec * s_old + _dot_tn(kends[ci][:, hs], vb[rs, vs])

    pending = []
    for piece in pieces:
        pending.append((*piece, scores(*piece)))
        yield GLA_SCORE_COST
        if len(pending) > GLA_SCORE_LEAD:
            finish(*pending.pop(0))
            yield GLA_FINISH_COST
    while pending:
        finish(*pending.pop(0))
        yield GLA_FINISH_COST
    o = jnp.concatenate(
        [jnp.concatenate([outs[ci, hd] for hd in range(GLA_HEADS)], axis=-1) for ci in range(rows // chunk)],
        axis=0)
    result["z"] = _gla_out(x, o, r, gn_ref, wo_ref)
    result["states"] = states
    yield 1700


def _ffn_steps(x, g_ref, w1_ref, w3_ref, w2_ref, gf_ref, result, *, final, col_chunks):
    rows = x.shape[0]
    h = _rmsnorm(x, g_ref[...]).astype(BF16)
    yield 200
    acc = x
    lo = 0
    pending = None
    for cols in col_chunks:
        cs = slice(lo, lo + cols)
        lo += cols
        u1 = _dot(h, w1_ref[:, cs])
        yield rows * cols // 256
        u3 = _dot(h, w3_ref[:, cs])
        yield rows * cols // 256
        if pending is not None:
            acc = acc + _dot(pending[0], w2_ref[pending[1], :])
            yield rows * (pending[1].stop - pending[1].start) // 256
        pending = ((_silu(u1) * u3).astype(BF16), cs)
    acc = acc + _dot(pending[0], w2_ref[pending[1], :])
    yield rows * (pending[1].stop - pending[1].start) // 256
    if final:
        acc = _rmsnorm(acc, gf_ref[...])
    result["out"] = acc


def _interleave(*pairs):
    gens = [p[0] for p in pairs]
    totals = [float(p[1]) for p in pairs]
    done = [0.0] * len(gens)
    live = list(range(len(gens)))
    while live:
        i = min(live, key=lambda j: done[j] / totals[j])
        try:
            done[i] += next(gens[i])
        except StopIteration:
            live.remove(i)


def _gla_sample_body(x_ref, s0_ref, g_ref, win_ref, wa2_ref, ba_ref, gn_ref, wo_ref,
                     tri_ref, masks_ref, o_ref, sout_ref, *, levels, seqs, steps):
    x = x_ref[...]
    rows = x.shape[0]
    q, k, v, r, b = _gla_front(x, g_ref, win_ref, wa2_ref, ba_ref, tri_ref)
    dk = q.shape[-1]
    dkh = dk // GLA_HEADS
    dvh = v.shape[-1] // GLA_HEADS
    q16 = q.astype(BF16)
    k16 = k.astype(BF16)
    ops = _gla_level_operands(q16, k16, b, levels)
    b3 = b.reshape(seqs, steps, dk)
    b_last = b3[:, steps - 1:steps, :]
    qb = (q * jnp.exp(b)).astype(BF16)
    kend = (k.reshape(seqs, steps, dk) * jnp.exp(b_last - b3)).reshape(rows, dk).astype(BF16)
    vb = v.astype(BF16)
    all_rows = slice(0, rows)
    outs = []
    for hd in range(GLA_HEADS):
        hs = slice(hd * dkh, (hd + 1) * dkh)
        vs = slice(hd * dvh, (hd + 1) * dvh)
        a = _gla_intra(q16[:, hs], k16[:, hs], ops, masks_ref, all_rows, hs)
        o_h = _dot(a.astype(BF16), vb[:, vs])
        o_state = []
        for s in range(seqs):
            rs = slice(s * steps, (s + 1) * steps)
            s_old = s0_ref[s, hd]
            o_state.append(_dot(qb[rs, hs], s_old.astype(BF16)))
            dec = jnp.exp(_lane_column(b[(s + 1) * steps - 1:(s + 1) * steps, hs]))
            sout_ref[s, hd] = dec * s_old + _dot_tn(kend[rs, hs], vb[rs, vs])
        outs.append(o_h + jnp.concatenate(o_state, axis=0))
    o_ref[...] = _gla_out(x, jnp.concatenate(outs, axis=-1), r, gn_ref, wo_ref)


def _gla_sample(x, state_gla, layer, gla_consts, prev, *, steps):
    n, d = x.shape
    seqs = GLA_SEQS if prev is None else 2 * GLA_SEQS
    rows = seqs * steps
    bs = n // steps
    assert bs % seqs == 0
    win, wa2, wo = gla_consts[1], gla_consts[2], gla_consts[5]
    dk = wa2.shape[1]
    dkh, dvh = dk // GLA_HEADS, wo.shape[0] // GLA_HEADS
    tri, masks, levels = _gla_tables(rows, steps, rows)
    consts = tuple(gla_consts) + (_Const(tri), _Const(masks))
    state_block = seqs * GLA_HEADS * dkh * dvh * 4
    resident = _nbytes(consts) + 4 * rows * d * 4 + 4 * state_block
    resident += rows * (win.shape[1] * 4 + (levels + 4) * dk * 4)
    return _call_with_state_slot(
        functools.partial(_gla_sample_body, levels=levels, seqs=seqs, steps=steps),
        (x, state_gla) + _arrays(consts),
        [pl.BlockSpec((rows, d), lambda i: (i, 0)),
         pl.BlockSpec((None, seqs, GLA_HEADS, dkh, dvh), lambda i: (layer, i, 0, 0, 0))]
        + _specs(consts),
        grid=(bs // seqs,),
        out_shape=jax.ShapeDtypeStruct((n, d), F32),
        out_spec=pl.BlockSpec((rows, d), lambda i: (i, 0)),
        state_shape=(bs, GLA_HEADS, dkh, dvh), state_block=(seqs, GLA_HEADS, dkh, dvh),
        state_index=lambda i: (i, 0, 0, 0),
        slot=layer, n_slots=state_gla.shape[0], prev=prev,
        scratch_shapes=[], resident=resident, name="gla_sample")


def _pool_ffn_body(x_ref, gm_ref, wp_ref, sc_ref, gfn_ref, w1_ref, w3_ref, w2_ref, gf_ref,
                   o_ref, ctx_ref, ext_ref, z_ref, *, rows, n_tiles, n_total, final):
    s = pl.program_id(0)
    d = x_ref.shape[-1]
    t = jnp.minimum(s, n_total - 1) % n_tiles

    def ffn_steps(ffn):
        return (_ffn_steps(z_ref[...], gfn_ref, w1_ref, w3_ref, w2_ref, gf_ref, ffn,
                           final=final, col_chunks=FFN_COL_CHUNKS), FFN_STEP_COST)

    def mixer_step(with_ffn):
        @pl.when(t == 0)
        def _():
            ext_ref[0:POOL_MAXW, :] = jnp.zeros((POOL_MAXW, d), F32)

        x = x_ref[...]
        h = _rmsnorm(x, gm_ref[...])
        ext_ref[POOL_MAXW:POOL_MAXW + rows, :] = h
        pos = t * rows + lax.broadcasted_iota(jnp.int32, (rows, 1), 0)

        def ext_read(k, cs):
            return ext_ref[POOL_MAXW - k:POOL_MAXW - k + rows, cs]

        mix, ffn = {}, {}
        streams = [(_pool_group_steps(ext_read, h, pos, wp_ref, sc_ref[...], mix), POOL_STEP_COST)]
        if with_ffn:
            streams.append(ffn_steps(ffn))
        _interleave(*streams)
        z_ref[...] = x + mix["y"]
        if with_ffn:
            o_ref[...] = ffn["out"]
        ext_ref[0:POOL_MAXW, :] = ext_ref[rows:rows + POOL_MAXW, :]

        @pl.when(t == n_tiles - 1)
        def _():
            ctx_ref[...] = ext_ref[rows + 1:rows + POOL_MAXW, :]

    @pl.when(s == 0)
    def _():
        mixer_step(False)

    @pl.when(jnp.logical_and(s > 0, s < n_total))
    def _():
        mixer_step(True)

    @pl.when(s == n_total)
    def _():
        ffn = {}
        _interleave(ffn_steps(ffn))
        o_ref[...] = ffn["out"]


def _gla_ffn_body(x_ref, gm_ref, win_ref, wa2_ref, ba_ref, gn_ref, wo_ref, tri_ref, masks_ref,
                  gfn_ref, w1_ref, w3_ref, w2_ref, gf_ref,
                  o_ref, sout_ref, s_ref, z_ref, *, levels, chunk, n_tiles, n_total, final):
    s = pl.program_id(0)
    t = jnp.minimum(s, n_total - 1) % n_tiles

    def ffn_steps(ffn):
        return (_ffn_steps(z_ref[...], gfn_ref, w1_ref, w3_ref, w2_ref, gf_ref, ffn,
                           final=final, col_chunks=FFN_COL_CHUNKS), FFN_STEP_COST)

    def mixer_step(with_ffn):
        @pl.when(t == 0)
        def _():
            s_ref[...] = jnp.zeros(s_ref.shape, F32)

        states = [s_ref[hd] for hd in range(GLA_HEADS)]
        mix, ffn = {}, {}
        streams = [(_gla_prompt_steps(x_ref[...], states, gm_ref, win_ref, wa2_ref, ba_ref, gn_ref,
                                      wo_ref, tri_ref, masks_ref, mix, levels=levels, chunk=chunk), GLA_STEP_COST)]
        if with_ffn:
            streams.append(ffn_steps(ffn))
        _interleave(*streams)
        states = mix["states"]
        z_ref[...] = mix["z"]
        if with_ffn:
            o_ref[...] = ffn["out"]
        for hd in range(GLA_HEADS):
            s_ref[hd] = states[hd]

        @pl.when(t == n_tiles - 1)
        def _():
            for hd in range(GLA_HEADS):
                sout_ref[hd] = states[hd]

    @pl.when(s == 0)
    def _():
        mixer_step(False)

    @pl.when(jnp.logical_and(s > 0, s < n_total))
    def _():
        mixer_step(True)

    @pl.when(s == n_total)
    def _():
        ffn = {}
        _interleave(ffn_steps(ffn))
        o_ref[...] = ffn["out"]


def _prompt_layer(kind, x, mixer_consts, ffn_consts, *, batch, seq, final, slot, n_slots, prev):
    n, d = x.shape
    rows = PROMPT_ROWS
    n_tiles = seq // rows
    n_total = batch * n_tiles
    assert seq % rows == 0 and rows >= POOL_MAXW and rows % GLA_CHUNK == 0
    dff = ffn_consts[1].shape[1]
    resident = _nbytes(mixer_consts) + _nbytes(ffn_consts) + 6 * rows * d * 4 + 3 * rows * dff * 4
    x_spec = pl.BlockSpec((rows, d), lambda s: (jnp.minimum(s, n_total - 1), 0))
    o_spec = pl.BlockSpec((rows, d), lambda s: (jnp.maximum(s - 1, 0), 0))
    seq_of = lambda s: jnp.minimum(s, n_total - 1) // n_tiles
    carry = [pltpu.VMEM((rows, d), F32)]
    if kind == "pool":
        body = functools.partial(_pool_ffn_body, rows=rows, n_tiles=n_tiles, n_total=n_total, final=final)
        state_shape = (batch, POOL_BUF, d)
        state_block = (None, POOL_BUF, d)
        state_index = lambda s: (seq_of(s), 0, 0)
        scratch = [pltpu.VMEM((rows + POOL_MAXW, d), F32)] + carry
        resident += (rows + POOL_MAXW) * d * 4
        consts = tuple(mixer_consts) + tuple(ffn_consts)
    else:
        win, wa2, wo = mixer_consts[1], mixer_consts[2], mixer_consts[5]
        dk = wa2.shape[1]
        dkh, dvh = dk // GLA_HEADS, wo.shape[0] // GLA_HEADS
        tri, masks, levels = _gla_tables(rows, GLA_CHUNK, GLA_CHUNK)
        body = functools.partial(_gla_ffn_body, levels=levels, chunk=GLA_CHUNK, n_tiles=n_tiles,
                                 n_total=n_total, final=final)
        state_shape = (batch, GLA_HEADS, dkh, dvh)
        state_block = (None, GLA_HEADS, dkh, dvh)
        state_index = lambda s: (seq_of(s), 0, 0, 0)
        scratch = [pltpu.VMEM((GLA_HEADS, dkh, dvh), F32)] + carry
        tables = (_Const(tri), _Const(masks))
        resident += _nbytes(tables) + 3 * GLA_HEADS * dkh * dvh * 4
        resident += rows * (win.shape[1] * 4 + (levels + 4) * dk * 4)
        consts = tuple(mixer_consts) + tables + tuple(ffn_consts)
    return _call_with_state_slot(
        body, (x,) + _arrays(consts), [x_spec] + _specs(consts),
        grid=(n_total + 1,),
        out_shape=jax.ShapeDtypeStruct((n, d), F32), out_spec=o_spec,
        state_shape=state_shape, state_block=state_block, state_index=state_index,
        slot=slot, n_slots=n_slots, prev=prev,
        scratch_shapes=scratch, resident=resident,
        name=kind + ("_ffn_final" if final else "_ffn") + "_prompt")


def kernel(x_prompt, x_sample, state_pool, state_gla, norm_mix, norm_ffn, norm_final, pool_w, pool_scale,
           gla_w_in, gla_w_a2, gla_b_a, gla_norm, gla_w_o, ffn_w1, ffn_w3, ffn_w2):
    batch, seq, d = x_prompt.shape
    bs, steps, _ = x_sample.shape
    depth = norm_mix.shape[0]

    xp = x_prompt.reshape(batch * seq, d)
    xs = x_sample.reshape(bs * steps, d)

    rows_of = lambda a: a.reshape(a.shape[0], 1, -1)
    w1, w3, w2 = ffn_w1.astype(BF16), ffn_w3.astype(BF16), ffn_w2.astype(BF16)
    w_pool, w_in, w_a2, w_o = (a.astype(BF16) for a in (pool_w, gla_w_in, gla_w_a2, gla_w_o))
    g_mix, g_ffn, sc_pool, b_a, g_gla = (rows_of(a) for a in (norm_mix, norm_ffn, pool_scale, gla_b_a, gla_norm))
    gf = _Const(norm_final.reshape(1, -1))

    n_pool, n_gla = state_pool.shape[0], state_gla.shape[0]
    new_pool_p = new_pool_s = new_gla_p = new_gla_s = None
    for i in range(depth):
        j = i // 2
        final = i == depth - 1
        g = _Const(g_mix, i)
        ffn_consts = (_Const(g_ffn, i), _Const(w1, i), _Const(w3, i), _Const(w2, i), gf)
        if i % 2 == 0:
            pool_consts = (g, _Const(w_pool, j), _Const(sc_pool, j))
            xp, new_pool_p = _prompt_layer("pool", xp, pool_consts, ffn_consts, batch=batch, seq=seq,
                                           final=final, slot=j, n_slots=n_pool, prev=new_pool_p)
            xs3, new_pool_s = _pool_sample(xs.reshape(bs, steps, d), state_pool, j, pool_consts, new_pool_s)
            xs = xs3.reshape(bs * steps, d)
        else:
            gla_consts = (g, _Const(w_in, j), _Const(w_a2, j), _Const(b_a, j), _Const(g_gla, j), _Const(w_o, j))
            xp, new_gla_p = _prompt_layer("gla", xp, gla_consts, ffn_consts, batch=batch, seq=seq,
                                          final=final, slot=j, n_slots=n_gla, prev=new_gla_p)
            xs, new_gla_s = _gla_sample(xs, state_gla, j, gla_consts, new_gla_s, steps=steps)
        xs = _ffn(xs, ffn_consts, final=final)

    return (xp.reshape(batch, seq, d), xs.reshape(bs, steps, d),
            new_pool_p, new_gla_p, new_pool_s, new_gla_s)
```

```python
import functools

import numpy as np
import jax
import jax.numpy as jnp
from jax import lax
from jax.experimental import pallas as pl
from jax.experimental.pallas import tpu as pltpu

F32 = jnp.float32
BF16 = jnp.bfloat16

PAST_LEN = 16384
POOL_WINDOWS = (2, 4, 8, 16)
POOL_MAXW = max(POOL_WINDOWS)
POOL_BUF = POOL_MAXW - 1
GLA_HEADS = 4
GLA_GATE_RANK = 16
GLA_GATE_TEMP = 16.0
EPS = 1e-6
LOG2_E = 1.4426950408889634

V7X_LANES = 128
V7X_SUBLANES = 8
V7X_VMEM_BYTES = 64 * 1024 * 1024

PROMPT_ROWS = 256
GLA_CHUNK = 128
FFN_STREAM_COLS = 256
FFN_WEIGHT_CHUNKS = 8
POOL_SEQS = 16
GLA_SEQS = 8
FFN_COL_CHUNKS = (1024, 1024, 768)
GLA_LEVEL_COST = 300
GLA_SCORE_COST = 300
GLA_FINISH_COST = 200
GLA_SCORE_LEAD = 2
FFN_STEP_COST = 8648
GLA_STEP_COST = 5882 + 7 * GLA_LEVEL_COST + 8 * (GLA_SCORE_COST + GLA_FINISH_COST)
POOL_STEP_COST = 2220


def _vmem_limit(resident_bytes):
    return int(min(V7X_VMEM_BYTES - (4 << 20), 2 * resident_bytes + (16 << 20)))


class _Const:
    def __init__(self, array, layer=None):
        self.array = array
        self.layer = layer
        self.shape = tuple(array.shape if layer is None else array.shape[1:])
        self.nbytes = int(np.prod(self.shape)) * array.dtype.itemsize

    def spec(self):
        nd = len(self.shape)
        if self.layer is None:
            return pl.BlockSpec(self.shape, lambda *_: (0,) * nd, pipeline_mode=pl.Buffered(1))
        layer = self.layer
        return pl.BlockSpec((None,) + self.shape, lambda *_: (layer,) + (0,) * nd, pipeline_mode=pl.Buffered(1))


def _nbytes(consts):
    return sum(c.nbytes for c in consts)


def _arrays(consts):
    return tuple(c.array for c in consts)


def _specs(consts):
    return [c.spec() for c in consts]


def _call_with_state_slot(body, args, in_specs, *, grid, out_shape, out_spec, state_shape, state_block,
                          state_index, slot, n_slots, prev, scratch_shapes, resident, name):
    n_args = len(args)
    inner = body
    aliases = {}
    if prev is None:
        state_spec = pl.BlockSpec((n_slots,) + tuple(state_block), lambda *g: (0,) + tuple(state_index(*g)))
        resident += 2 * (n_slots - 1) * int(np.prod([b for b in state_block if b is not None])) * 4

        def body(*refs):
            refs = list(refs)
            full = refs[n_args + 1]
            for other in range(n_slots):
                if other != slot:
                    full[other] = jnp.zeros(full.shape[1:], F32)
            refs[n_args + 1] = full.at[slot]
            return inner(*refs)
    else:
        state_spec = pl.BlockSpec((None,) + tuple(state_block), lambda *g: (slot,) + tuple(state_index(*g)))

        def body(*refs):
            return inner(*refs[:n_args], *refs[n_args + 1:])

        args = tuple(args) + (prev,)
        in_specs = list(in_specs) + [pl.BlockSpec(memory_space=pl.ANY)]
        aliases = {n_args: 1}
    return pl.pallas_call(
        body,
        out_shape=(out_shape, jax.ShapeDtypeStruct((n_slots,) + tuple(state_shape), F32)),
        grid=grid,
        in_specs=in_specs,
        out_specs=(out_spec, state_spec),
        scratch_shapes=scratch_shapes,
        input_output_aliases=aliases,
        compiler_params=pltpu.CompilerParams(
            dimension_semantics=("arbitrary",) * len(grid), vmem_limit_bytes=_vmem_limit(resident)),
        name=name,
    )(*args)


def _rmsnorm(x, g):
    ms = jnp.mean(x * x, axis=-1, keepdims=True)
    return x * lax.rsqrt(ms + EPS) * g


def _silu(x):
    hx = 0.5 * x
    return hx + hx * jnp.tanh(hx)


def _dot(a, b):
    return jnp.dot(a, b, preferred_element_type=F32)


def _dot_nt(a, b):
    return lax.dot_general(a, b, (((1,), (1,)), ((), ())), preferred_element_type=F32)


def _dot_tn(a, b):
    return lax.dot_general(a, b, (((0,), (0,)), ((), ())), preferred_element_type=F32)


def _split3(x):
    hi = x.astype(BF16)
    r1 = x - hi.astype(F32)
    mid = r1.astype(BF16)
    lo = (r1 - mid.astype(F32)).astype(BF16)
    return hi, mid, lo


def _ffn_cols_body(x_ref, g_ref, w1_ref, w3_ref, w2_ref, gf_ref, o_ref, h_ref, *, n_chunks, final):
    c = pl.program_id(0)

    @pl.when(c == 0)
    def _():
        x = x_ref[...]
        h_ref[...] = _rmsnorm(x, g_ref[...]).astype(BF16)
        o_ref[...] = x

    h = h_ref[...]
    w1, w3, w2 = (w[...].astype(BF16) for w in (w1_ref, w3_ref, w2_ref))
    act = (_silu(_dot(h, w1)) * _dot(h, w3)).astype(BF16)
    o_ref[...] += _dot(act, w2)
    if final:
        @pl.when(c == n_chunks - 1)
        def _():
            o_ref[...] = _rmsnorm(o_ref[...], gf_ref[...])


def _ffn(x, ffn_consts, *, final):
    n, d = x.shape
    g, w1, w3, w2, gf = ffn_consts
    dff = w1.shape[1]
    cols = FFN_STREAM_COLS
    assert dff % cols == 0
    n_chunks = dff // cols
    lw = w1.layer
    resident = 2 * 3 * d * cols * 4 + 3 * n * d * 4 + n * d * 2 + 3 * n * cols * 4
    return pl.pallas_call(
        functools.partial(_ffn_cols_body, n_chunks=n_chunks, final=final),
        out_shape=jax.ShapeDtypeStruct((n, d), F32),
        grid=(n_chunks,),
        in_specs=[pl.BlockSpec((n, d), lambda c: (0, 0), pipeline_mode=pl.Buffered(1)), g.spec(),
                  pl.BlockSpec((None, d, cols), lambda c: (lw, 0, c)),
                  pl.BlockSpec((None, d, cols), lambda c: (lw, 0, c)),
                  pl.BlockSpec((None, cols, d), lambda c: (lw, c, 0)),
                  gf.spec()],
        out_specs=pl.BlockSpec((n, d), lambda c: (0, 0)),
        scratch_shapes=[pltpu.VMEM((n, d), BF16)],
        compiler_params=pltpu.CompilerParams(
            dimension_semantics=("arbitrary",), vmem_limit_bytes=_vmem_limit(resident)),
        name="ffn_final" if final else "ffn",
    )(x, g.array, w1.array, w3.array, w2.array, gf.array)


def _pool_group_steps(ext_read, h, pos, w_ref, sc, result):
    gw = h.shape[-1] // len(POOL_WINDOWS)
    ys = []
    for g, w in enumerate(POOL_WINDOWS):
        cs = slice(g * gw, (g + 1) * gw)
        hg = h[:, cs]
        s = hg
        for k in range(1, w):
            s = s + ext_read(k, cs)
        cnt = jnp.minimum(pos + 1, w).astype(F32)
        diff = (s / cnt - hg).astype(BF16)
        ys.append(_dot(diff, w_ref[g]))
        yield 100 + 70 * (w - 1)
    result["y"] = jnp.concatenate(ys, axis=-1) * sc


def _pool_groups(ext_read, h, pos, w_ref, sc):
    result = {}
    for _ in _pool_group_steps(ext_read, h, pos, w_ref, sc, result):
        pass
    return result["y"]


def _pool_sample_body(x_ref, ctx_ref, g_ref, w_ref, sc_ref, o_ref, nctx_ref, ext_ref, *, seqs, steps):
    d = x_ref.shape[-1]
    x = x_ref[...]
    h = _rmsnorm(x, g_ref[...])
    ext_ref[:, 0:1, :] = jnp.zeros((seqs, 1, d), F32)
    ext_ref[:, 1:POOL_MAXW, :] = ctx_ref[...]
    ext_ref[:, POOL_MAXW:POOL_MAXW + steps, :] = h
    pos = PAST_LEN + lax.broadcasted_iota(jnp.int32, (seqs * steps, 1), 0) % steps
    h2 = h.reshape(seqs * steps, d)

    def ext_read(k, cs):
        return ext_ref[:, POOL_MAXW - k:POOL_MAXW - k + steps, cs].reshape(seqs * steps, cs.stop - cs.start)

    y = _pool_groups(ext_read, h2, pos, w_ref, sc_ref[...])
    o_ref[...] = x + y.reshape(seqs, steps, d)
    nctx_ref[...] = ext_ref[:, steps + 1:steps + POOL_MAXW, :]


def _pool_sample(x, state_pool, layer, pool_consts, prev):
    bs, steps, d = x.shape
    seqs = POOL_SEQS
    assert bs % seqs == 0 and steps == V7X_SUBLANES
    resident = 2 * seqs * (2 * steps + 2 * POOL_MAXW) * d * 4 + seqs * (steps + POOL_MAXW) * d * 4
    return _call_with_state_slot(
        functools.partial(_pool_sample_body, seqs=seqs, steps=steps),
        (x, state_pool) + _arrays(pool_consts),
        [pl.BlockSpec((seqs, steps, d), lambda i: (i, 0, 0)),
         pl.BlockSpec((None, seqs, POOL_BUF, d), lambda i: (layer, i, 0, 0))]
        + _specs(pool_consts),
        grid=(bs // seqs,),
        out_shape=jax.ShapeDtypeStruct((bs, steps, d), F32),
        out_spec=pl.BlockSpec((seqs, steps, d), lambda i: (i, 0, 0)),
        state_shape=(bs, POOL_BUF, d), state_block=(seqs, POOL_BUF, d), state_index=lambda i: (i, 0, 0),
        slot=layer, n_slots=state_pool.shape[0], prev=prev,
        scratch_shapes=[pltpu.VMEM((seqs, steps + POOL_MAXW, d), F32)],
        resident=resident, name="pool_sample")


def _gla_tables(rows, seg, score_rows):
    levels = int(np.log2(seg))
    assert 2 ** levels == seg and rows % score_rows == 0 and score_rows % seg == 0
    idx = np.arange(rows)
    tri = (idx[None, :] >= (idx[:, None] // seg) * seg) & (idx[None, :] <= idx[:, None])
    row = np.arange(score_rows)[:, None]
    col = np.arange(score_rows)[None, :]
    masks = [row == col]
    for l in range(1, levels + 1):
        s = 2 ** l
        mid = (row // s) * s + s // 2 - 1
        masks.append((row // s == col // s) & (row > mid) & (col <= mid))
    masks = np.stack(masks, axis=0).astype(np.float32)
    return jnp.asarray(tri.astype(np.float32), BF16), jnp.asarray(masks, F32), levels


def _gla_front(x, g_ref, win_ref, wa2_ref, ba_ref, tri_ref):
    rank, dk = wa2_ref.shape
    dkh = dk // GLA_HEADS
    h = _rmsnorm(x, g_ref[...]).astype(BF16)
    a_low = _dot(h, win_ref[:, 6 * dk:6 * dk + rank])
    xg = _dot(a_low.astype(BF16), wa2_ref[...]) + ba_ref[...]
    la = -(jnp.maximum(-xg, 0.0) + jnp.log(1.0 + jnp.exp(-jnp.abs(xg)))) * (1.0 / GLA_GATE_TEMP)
    hi, mid, lo = _split3(la)
    tri = tri_ref[...]
    b = _dot(tri, hi) + _dot(tri, mid) + _dot(tri, lo)
    proj = _dot(h, win_ref[:, 0:6 * dk])
    q = proj[:, 0:dk] * (dkh ** -0.5)
    k = proj[:, dk:2 * dk]
    v = proj[:, 2 * dk:4 * dk]
    r = proj[:, 4 * dk:6 * dk]
    return q, k, v, r, b


def _gla_level_operand_steps(q16, k16, b, levels, ops):
    rows, dk = b.shape
    row = lax.broadcasted_iota(jnp.int32, (rows, 1), 0)
    sub = lax.broadcasted_iota(jnp.int32, (1, V7X_SUBLANES, 1), 1)
    b = b * LOG2_E
    for l in range(1, levels + 1):
        s = 2 ** l
        if s >= V7X_SUBLANES:
            b3 = b.reshape(rows // s, s, dk)
            m = b3[:, s // 2 - 1:s // 2, :]
        else:
            b3 = b.reshape(rows // V7X_SUBLANES, V7X_SUBLANES, dk)
            picks = [b3[:, j + s // 2 - 1:j + s // 2, :] for j in range(0, V7X_SUBLANES, s)]
            m = picks[-1]
            for idx in range(len(picks) - 2, -1, -1):
                m = jnp.where(sub < (idx + 1) * s, picks[idx], m)
        e = jnp.exp2(-jnp.abs(b3 - m)).reshape(rows, dk).astype(BF16)
        second = ((row >> (l - 1)) & 1) == 1
        ops.append(jnp.where(second, q16, k16) * e)
        yield GLA_LEVEL_COST


def _gla_level_operands(q16, k16, b, levels):
    ops = []
    for _ in _gla_level_operand_steps(q16, k16, b, levels, ops):
        pass
    return ops


def _gla_intra(qb16_h, kb16_h, ops, masks_ref, rs, hs):
    a = _dot_nt(qb16_h, kb16_h) * masks_ref[0]
    for l, t in enumerate(ops, start=1):
        th = t[rs, hs]
        a = a + _dot_nt(th, th) * masks_ref[l]
    return a


def _lane_column(row_vec):
    return jnp.transpose(row_vec, (1, 0))


def _gla_out(x, o, r, gn_ref, wo_ref):
    dv = o.shape[-1]
    dvh = dv // GLA_HEADS
    normed = []
    for hd in range(GLA_HEADS):
        oh = o[:, hd * dvh:(hd + 1) * dvh]
        normed.append(oh * lax.rsqrt(jnp.mean(oh * oh, axis=-1, keepdims=True) + EPS))
    o = jnp.concatenate(normed, axis=-1) * gn_ref[...]
    o = (o * _silu(r)).astype(BF16)
    return x + _dot(o, wo_ref[...])


def _gla_prompt_steps(x, states, g_ref, win_ref, wa2_ref, ba_ref, gn_ref, wo_ref,
                      tri_ref, masks_ref, result, *, levels, chunk):
    rows = x.shape[0]
    rank, dk = wa2_ref.shape
    dkh = dk // GLA_HEADS
    h = _rmsnorm(x, g_ref[...]).astype(BF16)
    a_low = _dot(h, win_ref[:, 6 * dk:6 * dk + rank])
    yield 150
    q = _dot(h, win_ref[:, 0:dk]) * (dkh ** -0.5)
    q16 = q.astype(BF16)
    yield 512
    xg = _dot(a_low.astype(BF16), wa2_ref[...]) + ba_ref[...]
    yield 150
    k = _dot(h, win_ref[:, dk:2 * dk])
    k16 = k.astype(BF16)
    yield 512
    la = -(jnp.maximum(-xg, 0.0) + jnp.log(1.0 + jnp.exp(-jnp.abs(xg)))) * (1.0 / GLA_GATE_TEMP)
    hi, mid, lo = _split3(la)
    yield 400
    vb = _dot(h, win_ref[:, 2 * dk:4 * dk]).astype(BF16)
    yield 1024
    dvh = vb.shape[-1] // GLA_HEADS
    tri = tri_ref[...]
    b = _dot(tri, hi) + _dot(tri, mid) + _dot(tri, lo)
    yield 350
    r = _dot(h, win_ref[:, 4 * dk:6 * dk])
    yield 1024
    ops = []
    yield from _gla_level_operand_steps(q16, k16, b, levels, ops)
    qb = (q * jnp.exp(b)).astype(BF16)
    yield 60

    states = list(states)
    pieces = [(ci, hd) for ci in range(rows // chunk) for hd in range(GLA_HEADS)]
    kends = {}
    outs = {}

    def scores(ci, hd):
        rs = slice(ci * chunk, (ci + 1) * chunk)
        hs = slice(hd * dkh, (hd + 1) * dkh)
        return _gla_intra(q16[rs, hs], k16[rs, hs], ops, masks_ref, rs, hs).astype(BF16)

    def finish(ci, hd, a16):
        rs = slice(ci * chunk, (ci + 1) * chunk)
        hs = slice(hd * dkh, (hd + 1) * dkh)
        vs = slice(hd * dvh, (hd + 1) * dvh)
        b_c = b[rs]
        b_last = b_c[chunk - 1:chunk, :]
        if ci not in kends:
            kends[ci] = (k[rs] * jnp.exp(b_last - b_c)).astype(BF16)
        s_old = states[hd]
        outs[ci, hd] = _dot(a16, vb[rs, vs]) + _dot(qb[rs, hs], s_old.astype(BF16))
        dec = jnp.exp(_lane_column(b_last[:, hs]))
        states[hd] = dec * s_old + _dot_tn(kends[ci][:, hs], vb[rs, vs])

    pending = []
    for piece in pieces:
        pending.append((*piece, scores(*piece)))
        yield GLA_SCORE_COST
        if len(pending) > GLA_SCORE_LEAD:
            finish(*pending.pop(0))
            yield GLA_FINISH_COST
    while pending:
        finish(*pending.pop(0))
        yield GLA_FINISH_COST
    o = jnp.concatenate(
        [jnp.concatenate([outs[ci, hd] for hd in range(GLA_HEADS)], axis=-1) for ci in range(rows // chunk)],
        axis=0)
    result["z"] = _gla_out(x, o, r, gn_ref, wo_ref)
    result["states"] = states
    yield 1700


def _ffn_steps(x, g_ref, w1_ref, w3_ref, w2_ref, gf_ref, result, *, final, col_chunks):
    rows = x.shape[0]
    h = _rmsnorm(x, g_ref[...]).astype(BF16)
    yield 200
    acc = x
    lo = 0
    pending = None
    for cols in col_chunks:
        cs = slice(lo, lo + cols)
        lo += cols
        u1 = _dot(h, w1_ref[:, cs])
        yield rows * cols // 256
        u3 = _dot(h, w3_ref[:, cs])
        yield rows * cols // 256
        if pending is not None:
            acc = acc + _dot(pending[0], w2_ref[pending[1], :])
            yield rows * (pending[1].stop - pending[1].start) // 256
        pending = ((_silu(u1) * u3).astype(BF16), cs)
    acc = acc + _dot(pending[0], w2_ref[pending[1], :])
    yield rows * (pending[1].stop - pending[1].start) // 256
    if final:
        acc = _rmsnorm(acc, gf_ref[...])
    result["out"] = acc


def _ffn_weight_chunks(layer, wf_refs, ws_refs, stage_refs, sem_refs):
    chunks = []
    for wf, ws, stage, sem in zip(wf_refs, ws_refs, stage_refs, sem_refs):
        n_rows = ws.shape[0] // FFN_WEIGHT_CHUNKS
        for i in range(FFN_WEIGHT_CHUNKS):
            rs = pl.ds(i * n_rows, n_rows)
            chunks.append((wf.at[layer, rs, :], stage.at[i % 2], sem.at[i % 2], ws.at[rs, :]))
    return chunks


def _fetch_and_cast(chunks):
    copies = [pltpu.make_async_copy(src, stage, sem) for src, stage, sem, _ in chunks]
    copies[0].start()
    for i, (_, stage, _, dst) in enumerate(chunks):
        if i + 1 < len(chunks):
            copies[i + 1].start()
        copies[i].wait()
        dst[...] = stage[...].astype(BF16)


def _interleave(*pairs):
    gens = [p[0] for p in pairs]
    totals = [float(p[1]) for p in pairs]
    done = [0.0] * len(gens)
    live = list(range(len(gens)))
    while live:
        i = min(live, key=lambda j: done[j] / totals[j])
        try:
            done[i] += next(gens[i])
        except StopIteration:
            live.remove(i)


def _gla_sample_body(x_ref, s0_ref, g_ref, win_ref, wa2_ref, ba_ref, gn_ref, wo_ref,
                     tri_ref, masks_ref, o_ref, sout_ref, *, levels, seqs, steps):
    x = x_ref[...]
    rows = x.shape[0]
    q, k, v, r, b = _gla_front(x, g_ref, win_ref, wa2_ref, ba_ref, tri_ref)
    dk = q.shape[-1]
    dkh = dk // GLA_HEADS
    dvh = v.shape[-1] // GLA_HEADS
    q16 = q.astype(BF16)
    k16 = k.astype(BF16)
    ops = _gla_level_operands(q16, k16, b, levels)
    b3 = b.reshape(seqs, steps, dk)
    b_last = b3[:, steps - 1:steps, :]
    qb = (q * jnp.exp(b)).astype(BF16)
    kend = (k.reshape(seqs, steps, dk) * jnp.exp(b_last - b3)).reshape(rows, dk).astype(BF16)
    vb = v.astype(BF16)
    all_rows = slice(0, rows)
    outs = []
    for hd in range(GLA_HEADS):
        hs = slice(hd * dkh, (hd + 1) * dkh)
        vs = slice(hd * dvh, (hd + 1) * dvh)
        a = _gla_intra(q16[:, hs], k16[:, hs], ops, masks_ref, all_rows, hs)
        o_h = _dot(a.astype(BF16), vb[:, vs])
        o_state = []
        for s in range(seqs):
            rs = slice(s * steps, (s + 1) * steps)
            s_old = s0_ref[s, hd]
            o_state.append(_dot(qb[rs, hs], s_old.astype(BF16)))
            dec = jnp.exp(_lane_column(b[(s + 1) * steps - 1:(s + 1) * steps, hs]))
            sout_ref[s, hd] = dec * s_old + _dot_tn(kend[rs, hs], vb[rs, vs])
        outs.append(o_h + jnp.concatenate(o_state, axis=0))
    o_ref[...] = _gla_out(x, jnp.concatenate(outs, axis=-1), r, gn_ref, wo_ref)


def _gla_sample(x, state_gla, layer, gla_consts, prev, *, steps):
    n, d = x.shape
    seqs = GLA_SEQS if prev is None else 2 * GLA_SEQS
    rows = seqs * steps
    bs = n // steps
    assert bs % seqs == 0
    win, wa2, wo = gla_consts[1], gla_consts[2], gla_consts[5]
    dk = wa2.shape[1]
    dkh, dvh = dk // GLA_HEADS, wo.shape[0] // GLA_HEADS
    tri, masks, levels = _gla_tables(rows, steps, rows)
    consts = tuple(gla_consts) + (_Const(tri), _Const(masks))
    state_block = seqs * GLA_HEADS * dkh * dvh * 4
    resident = _nbytes(consts) + 4 * rows * d * 4 + 4 * state_block
    resident += rows * (win.shape[1] * 4 + (levels + 4) * dk * 4)
    return _call_with_state_slot(
        functools.partial(_gla_sample_body, levels=levels, seqs=seqs, steps=steps),
        (x, state_gla) + _arrays(consts),
        [pl.BlockSpec((rows, d), lambda i: (i, 0)),
         pl.BlockSpec((None, seqs, GLA_HEADS, dkh, dvh), lambda i: (layer, i, 0, 0, 0))]
        + _specs(consts),
        grid=(bs // seqs,),
        out_shape=jax.ShapeDtypeStruct((n, d), F32),
        out_spec=pl.BlockSpec((rows, d), lambda i: (i, 0)),
        state_shape=(bs, GLA_HEADS, dkh, dvh), state_block=(seqs, GLA_HEADS, dkh, dvh),
        state_index=lambda i: (i, 0, 0, 0),
        slot=layer, n_slots=state_gla.shape[0], prev=prev,
        scratch_shapes=[], resident=resident, name="gla_sample")


def _pool_ffn_body(x_ref, gm_ref, wp_ref, sc_ref, gfn_ref, gf_ref, w1f_ref, w3f_ref, w2f_ref,
                   o_ref, ctx_ref, ext_ref, z_ref, w1_ref, w3_ref, w2_ref, st13_ref, st2_ref, sem13, sem2,
                   *, rows, n_tiles, n_total, final, ffn_layer):
    s = pl.program_id(0)
    d = x_ref.shape[-1]
    t = jnp.minimum(s, n_total - 1) % n_tiles

    @pl.when(s == 0)
    def _():
        _fetch_and_cast(_ffn_weight_chunks(ffn_layer, (w1f_ref, w3f_ref, w2f_ref), (w1_ref, w3_ref, w2_ref),
                                           (st13_ref, st13_ref, st2_ref), (sem13, sem13, sem2)))

    def ffn_steps(ffn):
        return (_ffn_steps(z_ref[...], gfn_ref, w1_ref, w3_ref, w2_ref, gf_ref, ffn,
                           final=final, col_chunks=FFN_COL_CHUNKS), FFN_STEP_COST)

    def mixer_step(with_ffn):
        @pl.when(t == 0)
        def _():
            ext_ref[0:POOL_MAXW, :] = jnp.zeros((POOL_MAXW, d), F32)

        x = x_ref[...]
        h = _rmsnorm(x, gm_ref[...])
        ext_ref[POOL_MAXW:POOL_MAXW + rows, :] = h
        pos = t * rows + lax.broadcasted_iota(jnp.int32, (rows, 1), 0)

        def ext_read(k, cs):
            return ext_ref[POOL_MAXW - k:POOL_MAXW - k + rows, cs]

        mix, ffn = {}, {}
        streams = [(_pool_group_steps(ext_read, h, pos, wp_ref, sc_ref[...], mix), POOL_STEP_COST)]
        if with_ffn:
            streams.append(ffn_steps(ffn))
        _interleave(*streams)
        z_ref[...] = x + mix["y"]
        if with_ffn:
            o_ref[...] = ffn["out"]
        ext_ref[0:POOL_MAXW, :] = ext_ref[rows:rows + POOL_MAXW, :]

        @pl.when(t == n_tiles - 1)
        def _():
            ctx_ref[...] = ext_ref[rows + 1:rows + POOL_MAXW, :]

    @pl.when(s == 0)
    def _():
        mixer_step(False)

    @pl.when(jnp.logical_and(s > 0, s < n_total))
    def _():
        mixer_step(True)

    @pl.when(s == n_total)
    def _():
        ffn = {}
        _interleave(ffn_steps(ffn))
        o_ref[...] = ffn["out"]


def _gla_ffn_body(x_ref, gm_ref, win_ref, wa2_ref, ba_ref, gn_ref, wo_ref, tri_ref, masks_ref,
                  gfn_ref, gf_ref, w1f_ref, w3f_ref, w2f_ref,
                  o_ref, sout_ref, s_ref, z_ref, w1_ref, w3_ref, w2_ref, st13_ref, st2_ref, sem13, sem2,
                  *, levels, chunk, n_tiles, n_total, final, ffn_layer):
    s = pl.program_id(0)
    t = jnp.minimum(s, n_total - 1) % n_tiles

    @pl.when(s == 0)
    def _():
        _fetch_and_cast(_ffn_weight_chunks(ffn_layer, (w1f_ref, w3f_ref, w2f_ref), (w1_ref, w3_ref, w2_ref),
                                           (st13_ref, st13_ref, st2_ref), (sem13, sem13, sem2)))

    def ffn_steps(ffn):
        return (_ffn_steps(z_ref[...], gfn_ref, w1_ref, w3_ref, w2_ref, gf_ref, ffn,
                           final=final, col_chunks=FFN_COL_CHUNKS), FFN_STEP_COST)

    def mixer_step(with_ffn):
        @pl.when(t == 0)
        def _():
            s_ref[...] = jnp.zeros(s_ref.shape, F32)

        states = [s_ref[hd] for hd in range(GLA_HEADS)]
        mix, ffn = {}, {}
        streams = [(_gla_prompt_steps(x_ref[...], states, gm_ref, win_ref, wa2_ref, ba_ref, gn_ref,
                                      wo_ref, tri_ref, masks_ref, mix, levels=levels, chunk=chunk), GLA_STEP_COST)]
        if with_ffn:
            streams.append(ffn_steps(ffn))
        _interleave(*streams)
        states = mix["states"]
        z_ref[...] = mix["z"]
        if with_ffn:
            o_ref[...] = ffn["out"]
        for hd in range(GLA_HEADS):
            s_ref[hd] = states[hd]

        @pl.when(t == n_tiles - 1)
        def _():
            for hd in range(GLA_HEADS):
                sout_ref[hd] = states[hd]

    @pl.when(s == 0)
    def _():
        mixer_step(False)

    @pl.when(jnp.logical_and(s > 0, s < n_total))
    def _():
        mixer_step(True)

    @pl.when(s == n_total)
    def _():
        ffn = {}
        _interleave(ffn_steps(ffn))
        o_ref[...] = ffn["out"]


def _prompt_layer(kind, x, mixer_consts, ffn_consts, *, batch, seq, final, slot, n_slots, prev):
    n, d = x.shape
    rows = PROMPT_ROWS
    n_tiles = seq // rows
    n_total = batch * n_tiles
    assert seq % rows == 0 and rows >= POOL_MAXW and rows % GLA_CHUNK == 0
    g_ffn, w1f, w3f, w2f, gf = ffn_consts
    dff = w1f.shape[1]
    ffn_vmem = (g_ffn, gf)
    stage13 = (2, d // FFN_WEIGHT_CHUNKS, dff)
    stage2 = (2, dff // FFN_WEIGHT_CHUNKS, d)
    assert d % FFN_WEIGHT_CHUNKS == 0 and (dff // FFN_WEIGHT_CHUNKS) % (2 * V7X_SUBLANES) == 0
    ffn_scratch = [pltpu.VMEM((d, dff), BF16), pltpu.VMEM((d, dff), BF16), pltpu.VMEM((dff, d), BF16),
                   pltpu.VMEM(stage13, F32), pltpu.VMEM(stage2, F32),
                   pltpu.SemaphoreType.DMA((2,)), pltpu.SemaphoreType.DMA((2,))]
    resident = _nbytes(mixer_consts) + 3 * d * dff * 2 + 4 * int(np.prod(stage13)) * 2
    resident += 6 * rows * d * 4 + 3 * rows * dff * 4
    x_spec = pl.BlockSpec((rows, d), lambda s: (jnp.minimum(s, n_total - 1), 0))
    o_spec = pl.BlockSpec((rows, d), lambda s: (jnp.maximum(s - 1, 0), 0))
    seq_of = lambda s: jnp.minimum(s, n_total - 1) // n_tiles
    carry = [pltpu.VMEM((rows, d), F32)]
    if kind == "pool":
        body = functools.partial(_pool_ffn_body, rows=rows, n_tiles=n_tiles, n_total=n_total, final=final,
                                 ffn_layer=w1f.layer)
        state_shape = (batch, POOL_BUF, d)
        state_block = (None, POOL_BUF, d)
        state_index = lambda s: (seq_of(s), 0, 0)
        scratch = [pltpu.VMEM((rows + POOL_MAXW, d), F32)] + carry
        resident += (rows + POOL_MAXW) * d * 4
        consts = tuple(mixer_consts) + ffn_vmem
    else:
        win, wa2, wo = mixer_consts[1], mixer_consts[2], mixer_consts[5]
        dk = wa2.shape[1]
        dkh, dvh = dk // GLA_HEADS, wo.shape[0] // GLA_HEADS
        tri, masks, levels = _gla_tables(rows, GLA_CHUNK, GLA_CHUNK)
        body = functools.partial(_gla_ffn_body, levels=levels, chunk=GLA_CHUNK, n_tiles=n_tiles,
                                 n_total=n_total, final=final, ffn_layer=w1f.layer)
        state_shape = (batch, GLA_HEADS, dkh, dvh)
        state_block = (None, GLA_HEADS, dkh, dvh)
        state_index = lambda s: (seq_of(s), 0, 0, 0)
        scratch = [pltpu.VMEM((GLA_HEADS, dkh, dvh), F32)] + carry
        tables = (_Const(tri), _Const(masks))
        resident += _nbytes(tables) + 3 * GLA_HEADS * dkh * dvh * 4
        resident += rows * (win.shape[1] * 4 + (levels + 4) * dk * 4)
        consts = tuple(mixer_consts) + tables + ffn_vmem
    hbm = pl.BlockSpec(memory_space=pl.ANY)
    return _call_with_state_slot(
        body, (x,) + _arrays(consts) + (w1f.array, w3f.array, w2f.array), [x_spec] + _specs(consts) + [hbm] * 3,
        grid=(n_total + 1,),
        out_shape=jax.ShapeDtypeStruct((n, d), F32), out_spec=o_spec,
        state_shape=state_shape, state_block=state_block, state_index=state_index,
        slot=slot, n_slots=n_slots, prev=prev,
        scratch_shapes=scratch + ffn_scratch, resident=resident,
        name=kind + ("_ffn_final" if final else "_ffn") + "_prompt")


def kernel(x_prompt, x_sample, state_pool, state_gla, norm_mix, norm_ffn, norm_final, pool_w, pool_scale,
           gla_w_in, gla_w_a2, gla_b_a, gla_norm, gla_w_o, ffn_w1, ffn_w3, ffn_w2):
    batch, seq, d = x_prompt.shape
    bs, steps, _ = x_sample.shape
    depth = norm_mix.shape[0]

    xp = x_prompt.reshape(batch * seq, d)
    xs = x_sample.reshape(bs * steps, d)

    rows_of = lambda a: a.reshape(a.shape[0], 1, -1)
    w_pool, w_in, w_a2, w_o = (a.astype(BF16) for a in (pool_w, gla_w_in, gla_w_a2, gla_w_o))
    g_mix, g_ffn, sc_pool, b_a, g_gla = (rows_of(a) for a in (norm_mix, norm_ffn, pool_scale, gla_b_a, gla_norm))
    gf = _Const(norm_final.reshape(1, -1))

    n_pool, n_gla = state_pool.shape[0], state_gla.shape[0]
    new_pool_p = new_pool_s = new_gla_p = new_gla_s = None
    for i in range(depth):
        j = i // 2
        final = i == depth - 1
        g = _Const(g_mix, i)
        ffn_consts = (_Const(g_ffn, i), _Const(ffn_w1, i), _Const(ffn_w3, i), _Const(ffn_w2, i), gf)
        if i % 2 == 0:
            pool_consts = (g, _Const(w_pool, j), _Const(sc_pool, j))
            xp, new_pool_p = _prompt_layer("pool", xp, pool_consts, ffn_consts, batch=batch, seq=seq,
                                           final=final, slot=j, n_slots=n_pool, prev=new_pool_p)
            xs3, new_pool_s = _pool_sample(xs.reshape(bs, steps, d), state_pool, j, pool_consts, new_pool_s)
            xs = xs3.reshape(bs * steps, d)
        else:
            gla_consts = (g, _Const(w_in, j), _Const(w_a2, j), _Const(b_a, j), _Const(g_gla, j), _Const(w_o, j))
            xp, new_gla_p = _prompt_layer("gla", xp, gla_consts, ffn_consts, batch=batch, seq=seq,
                                          final=final, slot=j, n_slots=n_gla, prev=new_gla_p)
            xs, new_gla_s = _gla_sample(xs, state_gla, j, gla_consts, new_gla_s, steps=steps)
        xs = _ffn(xs, ffn_consts, final=final)

    return (xp.reshape(batch, seq, d), xs.reshape(bs, steps, d),
            new_pool_p, new_gla_p, new_pool_s, new_gla_s)
```

```python
import functools

import numpy as np
import jax
import jax.numpy as jnp
from jax import lax
from jax.experimental import pallas as pl
from jax.experimental.pallas import tpu as pltpu

F32 = jnp.float32
BF16 = jnp.bfloat16

PAST_LEN = 16384
POOL_WINDOWS = (2, 4, 8, 16)
POOL_MAXW = max(POOL_WINDOWS)
POOL_BUF = POOL_MAXW - 1
GLA_HEADS = 4
GLA_GATE_RANK = 16
GLA_GATE_TEMP = 16.0
EPS = 1e-6
LOG2_E = 1.4426950408889634

V7X_LANES = 128
V7X_SUBLANES = 8
V7X_VMEM_BYTES = 64 * 1024 * 1024

PROMPT_ROWS = {"pool": 256, "gla": 256}
GLA_CHUNK = 128
FFN_WEIGHT_CHUNKS = 8
FFN_STAGE_SLOTS = 3
POOL_SEQS = 16
GLA_SEQS = 8
FFN_COL_CHUNKS = (1024, 1024, 768)
GLA_LEVEL_COST = 300
GLA_SCORE_COST = 300
GLA_FINISH_COST = 200
GLA_SCORE_LEAD = 2
FFN_STEP_COST = 8648
GLA_STEP_COST = 5882 + 7 * GLA_LEVEL_COST + 8 * (GLA_SCORE_COST + GLA_FINISH_COST)
POOL_STEP_COST = 2220


def _vmem_limit(resident_bytes):
    return int(min(V7X_VMEM_BYTES - (4 << 20), 2 * resident_bytes + (16 << 20)))


class _Const:
    def __init__(self, array, layer=None):
        self.array = array
        self.layer = layer
        self.shape = tuple(array.shape if layer is None else array.shape[1:])
        self.nbytes = int(np.prod(self.shape)) * array.dtype.itemsize

    def spec(self):
        nd = len(self.shape)
        if self.layer is None:
            return pl.BlockSpec(self.shape, lambda *_: (0,) * nd, pipeline_mode=pl.Buffered(1))
        layer = self.layer
        return pl.BlockSpec((None,) + self.shape, lambda *_: (layer,) + (0,) * nd, pipeline_mode=pl.Buffered(1))


def _nbytes(consts):
    return sum(c.nbytes for c in consts)


def _arrays(consts):
    return tuple(c.array for c in consts)


def _specs(consts):
    return [c.spec() for c in consts]


def _call_with_state_slot(body, args, in_specs, *, grid, out_shapes, out_specs, state_shape, state_block,
                          state_index, slot, n_slots, prev, scratch_shapes, resident, name):
    n_args = len(args)
    n_outs = len(out_shapes)
    inner = body
    aliases = {}
    if prev is None:
        state_spec = pl.BlockSpec((n_slots,) + tuple(state_block), lambda *g: (0,) + tuple(state_index(*g)))
        resident += 2 * (n_slots - 1) * int(np.prod([b for b in state_block if b is not None])) * 4

        def body(*refs):
            refs = list(refs)
            full = refs[n_args + n_outs]
            for other in range(n_slots):
                if other != slot:
                    full[other] = jnp.zeros(full.shape[1:], F32)
            refs[n_args + n_outs] = full.at[slot]
            return inner(*refs)
    else:
        state_spec = pl.BlockSpec((None,) + tuple(state_block), lambda *g: (slot,) + tuple(state_index(*g)))

        def body(*refs):
            return inner(*refs[:n_args], *refs[n_args + 1:])

        args = tuple(args) + (prev,)
        in_specs = list(in_specs) + [pl.BlockSpec(memory_space=pl.ANY)]
        aliases = {n_args: n_outs}
    return pl.pallas_call(
        body,
        out_shape=tuple(out_shapes) + (jax.ShapeDtypeStruct((n_slots,) + tuple(state_shape), F32),),
        grid=grid,
        in_specs=in_specs,
        out_specs=tuple(out_specs) + (state_spec,),
        scratch_shapes=scratch_shapes,
        input_output_aliases=aliases,
        compiler_params=pltpu.CompilerParams(
            dimension_semantics=("arbitrary",) * len(grid), vmem_limit_bytes=_vmem_limit(resident)),
        name=name,
    )(*args)


def _rmsnorm(x, g):
    ms = jnp.mean(x * x, axis=-1, keepdims=True)
    return x * lax.rsqrt(ms + EPS) * g


def _silu(x):
    hx = 0.5 * x
    return hx + hx * jnp.tanh(hx)


def _dot(a, b):
    return jnp.dot(a, b, preferred_element_type=F32)


def _dot_nt(a, b):
    return lax.dot_general(a, b, (((1,), (1,)), ((), ())), preferred_element_type=F32)


def _dot_tn(a, b):
    return lax.dot_general(a, b, (((0,), (0,)), ((), ())), preferred_element_type=F32)


def _split3(x):
    hi = x.astype(BF16)
    r1 = x - hi.astype(F32)
    mid = r1.astype(BF16)
    lo = (r1 - mid.astype(F32)).astype(BF16)
    return hi, mid, lo


def _pool_group_steps(ext_read, h, pos, w_ref, sc, result):
    gw = h.shape[-1] // len(POOL_WINDOWS)
    ys = []
    for g, w in enumerate(POOL_WINDOWS):
        cs = slice(g * gw, (g + 1) * gw)
        hg = h[:, cs]
        s = hg
        for k in range(1, w):
            s = s + ext_read(k, cs)
        cnt = jnp.minimum(pos + 1, w).astype(F32)
        diff = (s / cnt - hg).astype(BF16)
        ys.append(_dot(diff, w_ref[g]))
        yield (100 + 70 * (w - 1)) * h.shape[0] // 256
    result["y"] = jnp.concatenate(ys, axis=-1) * sc


def _pool_groups(ext_read, h, pos, w_ref, sc):
    result = {}
    for _ in _pool_group_steps(ext_read, h, pos, w_ref, sc, result):
        pass
    return result["y"]


def _pool_sample_body(x_ref, ctx_ref, g_ref, w_ref, sc_ref, o_ref, nctx_ref, ext_ref, *, seqs, steps):
    d = x_ref.shape[-1]
    x = x_ref[...]
    h = _rmsnorm(x, g_ref[...])
    ext_ref[:, 0:1, :] = jnp.zeros((seqs, 1, d), F32)
    ext_ref[:, 1:POOL_MAXW, :] = ctx_ref[...]
    ext_ref[:, POOL_MAXW:POOL_MAXW + steps, :] = h
    pos = PAST_LEN + lax.broadcasted_iota(jnp.int32, (seqs * steps, 1), 0) % steps
    h2 = h.reshape(seqs * steps, d)

    def ext_read(k, cs):
        return ext_ref[:, POOL_MAXW - k:POOL_MAXW - k + steps, cs].reshape(seqs * steps, cs.stop - cs.start)

    y = _pool_groups(ext_read, h2, pos, w_ref, sc_ref[...])
    o_ref[...] = x + y.reshape(seqs, steps, d)
    nctx_ref[...] = ext_ref[:, steps + 1:steps + POOL_MAXW, :]


def _pool_sample(x, state_pool, layer, pool_consts, prev):
    bs, steps, d = x.shape
    seqs = POOL_SEQS
    assert bs % seqs == 0 and steps == V7X_SUBLANES
    resident = 2 * seqs * (2 * steps + 2 * POOL_MAXW) * d * 4 + seqs * (steps + POOL_MAXW) * d * 4
    return _call_with_state_slot(
        functools.partial(_pool_sample_body, seqs=seqs, steps=steps),
        (x, state_pool) + _arrays(pool_consts),
        [pl.BlockSpec((seqs, steps, d), lambda i: (i, 0, 0)),
         pl.BlockSpec((None, seqs, POOL_BUF, d), lambda i: (layer, i, 0, 0))]
        + _specs(pool_consts),
        grid=(bs // seqs,),
        out_shapes=[jax.ShapeDtypeStruct((bs, steps, d), F32)],
        out_specs=[pl.BlockSpec((seqs, steps, d), lambda i: (i, 0, 0))],
        state_shape=(bs, POOL_BUF, d), state_block=(seqs, POOL_BUF, d), state_index=lambda i: (i, 0, 0),
        slot=layer, n_slots=state_pool.shape[0], prev=prev,
        scratch_shapes=[pltpu.VMEM((seqs, steps + POOL_MAXW, d), F32)],
        resident=resident, name="pool_sample")


def _gla_tables(rows, seg, score_rows):
    levels = int(np.log2(seg))
    assert 2 ** levels == seg and rows % score_rows == 0 and score_rows % seg == 0
    idx = np.arange(rows)
    tri = (idx[None, :] >= (idx[:, None] // seg) * seg) & (idx[None, :] <= idx[:, None])
    row = np.arange(score_rows)[:, None]
    col = np.arange(score_rows)[None, :]
    masks = [row == col]
    for l in range(1, levels + 1):
        s = 2 ** l
        mid = (row // s) * s + s // 2 - 1
        masks.append((row // s == col // s) & (row > mid) & (col <= mid))
    masks = np.stack(masks, axis=0).astype(np.float32)
    return jnp.asarray(tri.astype(np.float32), BF16), jnp.asarray(masks, F32), levels


def _gla_front(x, g_ref, win_ref, wa2_ref, ba_ref, tri_ref):
    rank, dk = wa2_ref.shape
    dkh = dk // GLA_HEADS
    h = _rmsnorm(x, g_ref[...]).astype(BF16)
    a_low = _dot(h, win_ref[:, 6 * dk:6 * dk + rank])
    xg = _dot(a_low.astype(BF16), wa2_ref[...]) + ba_ref[...]
    la = -(jnp.maximum(-xg, 0.0) + jnp.log(1.0 + jnp.exp(-jnp.abs(xg)))) * (1.0 / GLA_GATE_TEMP)
    hi, mid, lo = _split3(la)
    tri = tri_ref[...]
    b = _dot(tri, hi) + _dot(tri, mid) + _dot(tri, lo)
    proj = _dot(h, win_ref[:, 0:6 * dk])
    q = proj[:, 0:dk] * (dkh ** -0.5)
    k = proj[:, dk:2 * dk]
    v = proj[:, 2 * dk:4 * dk]
    r = proj[:, 4 * dk:6 * dk]
    return q, k, v, r, b


def _gla_level_operand_steps(q16, k16, b, levels, ops):
    rows, dk = b.shape
    row = lax.broadcasted_iota(jnp.int32, (rows, 1), 0)
    sub = lax.broadcasted_iota(jnp.int32, (1, V7X_SUBLANES, 1), 1)
    b = b * LOG2_E
    for l in range(1, levels + 1):
        s = 2 ** l
        if s >= V7X_SUBLANES:
            b3 = b.reshape(rows // s, s, dk)
            m = b3[:, s // 2 - 1:s // 2, :]
        else:
            b3 = b.reshape(rows // V7X_SUBLANES, V7X_SUBLANES, dk)
            picks = [b3[:, j + s // 2 - 1:j + s // 2, :] for j in range(0, V7X_SUBLANES, s)]
            m = picks[-1]
            for idx in range(len(picks) - 2, -1, -1):
                m = jnp.where(sub < (idx + 1) * s, picks[idx], m)
        e = jnp.exp2(-jnp.abs(b3 - m)).reshape(rows, dk).astype(BF16)
        second = ((row >> (l - 1)) & 1) == 1
        ops.append(jnp.where(second, q16, k16) * e)
        yield GLA_LEVEL_COST


def _gla_level_operands(q16, k16, b, levels):
    ops = []
    for _ in _gla_level_operand_steps(q16, k16, b, levels, ops):
        pass
    return ops


def _gla_intra(qb16_h, kb16_h, ops, masks_ref, rs, hs):
    a = _dot_nt(qb16_h, kb16_h) * masks_ref[0]
    for l, t in enumerate(ops, start=1):
        th = t[rs, hs]
        a = a + _dot_nt(th, th) * masks_ref[l]
    return a


def _lane_column(row_vec):
    return jnp.transpose(row_vec, (1, 0))


def _gla_out(x, o, r, gn_ref, wo_ref):
    dv = o.shape[-1]
    dvh = dv // GLA_HEADS
    normed = []
    for hd in range(GLA_HEADS):
        oh = o[:, hd * dvh:(hd + 1) * dvh]
        normed.append(oh * lax.rsqrt(jnp.mean(oh * oh, axis=-1, keepdims=True) + EPS))
    o = jnp.concatenate(normed, axis=-1) * gn_ref[...]
    o = (o * _silu(r)).astype(BF16)
    return x + _dot(o, wo_ref[...])


def _gla_prompt_steps(x, states, g_ref, win_ref, wa2_ref, ba_ref, gn_ref, wo_ref,
                      tri_ref, masks_ref, result, *, levels, chunk):
    rows = x.shape[0]
    rank, dk = wa2_ref.shape
    dkh = dk // GLA_HEADS
    h = _rmsnorm(x, g_ref[...]).astype(BF16)
    a_low = _dot(h, win_ref[:, 6 * dk:6 * dk + rank])
    yield 150
    q = _dot(h, win_ref[:, 0:dk]) * (dkh ** -0.5)
    q16 = q.astype(BF16)
    yield 512
    xg = _dot(a_low.astype(BF16), wa2_ref[...]) + ba_ref[...]
    yield 150
    k = _dot(h, win_ref[:, dk:2 * dk])
    k16 = k.astype(BF16)
    yield 512
    la = -(jnp.maximum(-xg, 0.0) + jnp.log(1.0 + jnp.exp(-jnp.abs(xg)))) * (1.0 / GLA_GATE_TEMP)
    hi, mid, lo = _split3(la)
    yield 400
    vb = _dot(h, win_ref[:, 2 * dk:4 * dk]).astype(BF16)
    yield 1024
    dvh = vb.shape[-1] // GLA_HEADS
    tri = tri_ref[...]
    b = _dot(tri, hi) + _dot(tri, mid) + _dot(tri, lo)
    yield 350
    r = _dot(h, win_ref[:, 4 * dk:6 * dk])
    yield 1024
    ops = []
    yield from _gla_level_operand_steps(q16, k16, b, levels, ops)
    qb = (q * jnp.exp(b)).astype(BF16)
    yield 60

    states = list(states)
    pieces = [(ci, hd) for ci in range(rows // chunk) for hd in range(GLA_HEADS)]
    kends = {}
    outs = {}

    def scores(ci, hd):
        rs = slice(ci * chunk, (ci + 1) * chunk)
        hs = slice(hd * dkh, (hd + 1) * dkh)
        return _gla_intra(q16[rs, hs], k16[rs, hs], ops, masks_ref, rs, hs).astype(BF16)

    def finish(ci, hd, a16):
        rs = slice(ci * chunk, (ci + 1) * chunk)
        hs = slice(hd * dkh, (hd + 1) * dkh)
        vs = slice(hd * dvh, (hd + 1) * dvh)
        b_c = b[rs]
        b_last = b_c[chunk - 1:chunk, :]
        if ci not in kends:
            kends[ci] = (k[rs] * jnp.exp(b_last - b_c)).astype(BF16)
        s_old = states[hd]
        outs[ci, hd] = _dot(a16, vb[rs, vs]) + _dot(qb[rs, hs], s_old.astype(BF16))
        dec = jnp.exp(_lane_column(b_last[:, hs]))
        states[hd] = dec * s_old + _dot_tn(kends[ci][:, hs], vb[rs, vs])

    pending = []
    for piece in pieces:
        pending.append((*piece, scores(*piece)))
        yield GLA_SCORE_COST
        if len(pending) > GLA_SCORE_LEAD:
            finish(*pending.pop(0))
            yield GLA_FINISH_COST
    while pending:
        finish(*pending.pop(0))
        yield GLA_FINISH_COST
    o = jnp.concatenate(
        [jnp.concatenate([outs[ci, hd] for hd in range(GLA_HEADS)], axis=-1) for ci in range(rows // chunk)],
        axis=0)
    result["z"] = _gla_out(x, o, r, gn_ref, wo_ref)
    result["states"] = states
    yield 1700


def _ffn_steps(x, g_ref, w1_ref, w3_ref, w2_ref, gf_ref, result, *, final, col_chunks):
    rows = x.shape[0]
    h = _rmsnorm(x, g_ref[...]).astype(BF16)
    yield 200
    acc = x
    lo = 0
    pending = None
    for cols in col_chunks:
        cs = slice(lo, lo + cols)
        lo += cols
        u1 = _dot(h, w1_ref[:, cs])
        yield rows * cols // 256
        u3 = _dot(h, w3_ref[:, cs])
        yield rows * cols // 256
        if pending is not None:
            acc = acc + _dot(pending[0], w2_ref[pending[1], :])
            yield rows * (pending[1].stop - pending[1].start) // 256
        pending = ((_silu(u1) * u3).astype(BF16), cs)
    acc = acc + _dot(pending[0], w2_ref[pending[1], :])
    yield rows * (pending[1].stop - pending[1].start) // 256
    if final:
        acc = _rmsnorm(acc, gf_ref[...])
    result["out"] = acc


def _ffn_weight_chunks(layer, wf_refs, ws_refs, stage_refs, sem_refs):
    chunks = []
    used = {}
    for wf, ws, stage, sem in zip(wf_refs, ws_refs, stage_refs, sem_refs):
        n_rows = ws.shape[0] // FFN_WEIGHT_CHUNKS
        for i in range(FFN_WEIGHT_CHUNKS):
            rs = pl.ds(i * n_rows, n_rows)
            slot = used.get(id(stage), 0) % FFN_STAGE_SLOTS
            used[id(stage)] = used.get(id(stage), 0) + 1
            chunks.append((wf.at[layer, rs, :], stage.at[slot], sem.at[slot], ws.at[rs, :]))
    return chunks


def _fetch_and_cast(chunks):
    ahead = FFN_STAGE_SLOTS - 1
    copies = [pltpu.make_async_copy(src, stage, sem) for src, stage, sem, _ in chunks]
    for i in range(min(ahead, len(chunks))):
        copies[i].start()
    for i, (_, stage, _, dst) in enumerate(chunks):
        if i + ahead < len(chunks):
            copies[i + ahead].start()
        copies[i].wait()
        dst[...] = stage[...].astype(BF16)


def _interleave(*pairs):
    gens = [p[0] for p in pairs]
    totals = [float(p[1]) for p in pairs]
    done = [0.0] * len(gens)
    live = list(range(len(gens)))
    while live:
        i = min(live, key=lambda j: done[j] / totals[j])
        try:
            done[i] += next(gens[i])
        except StopIteration:
            live.remove(i)


def _gla_sample_body(x_ref, s0_ref, g_ref, win_ref, wa2_ref, ba_ref, gn_ref, wo_ref,
                     tri_ref, masks_ref, o_ref, sout_ref, *, levels, seqs, steps):
    x = x_ref[...]
    rows = x.shape[0]
    q, k, v, r, b = _gla_front(x, g_ref, win_ref, wa2_ref, ba_ref, tri_ref)
    dk = q.shape[-1]
    dkh = dk // GLA_HEADS
    dvh = v.shape[-1] // GLA_HEADS
    q16 = q.astype(BF16)
    k16 = k.astype(BF16)
    ops = _gla_level_operands(q16, k16, b, levels)
    b3 = b.reshape(seqs, steps, dk)
    b_last = b3[:, steps - 1:steps, :]
    qb = (q * jnp.exp(b)).astype(BF16)
    kend = (k.reshape(seqs, steps, dk) * jnp.exp(b_last - b3)).reshape(rows, dk).astype(BF16)
    vb = v.astype(BF16)
    all_rows = slice(0, rows)
    outs = []
    for hd in range(GLA_HEADS):
        hs = slice(hd * dkh, (hd + 1) * dkh)
        vs = slice(hd * dvh, (hd + 1) * dvh)
        a = _gla_intra(q16[:, hs], k16[:, hs], ops, masks_ref, all_rows, hs)
        o_h = _dot(a.astype(BF16), vb[:, vs])
        o_state = []
        for s in range(seqs):
            rs = slice(s * steps, (s + 1) * steps)
            s_old = s0_ref[s, hd]
            o_state.append(_dot(qb[rs, hs], s_old.astype(BF16)))
            dec = jnp.exp(_lane_column(b[(s + 1) * steps - 1:(s + 1) * steps, hs]))
            sout_ref[s, hd] = dec * s_old + _dot_tn(kend[rs, hs], vb[rs, vs])
        outs.append(o_h + jnp.concatenate(o_state, axis=0))
    o_ref[...] = _gla_out(x, jnp.concatenate(outs, axis=-1), r, gn_ref, wo_ref)


def _gla_sample(x, state_gla, layer, gla_consts, prev, *, steps):
    n, d = x.shape
    seqs = GLA_SEQS if prev is None else 2 * GLA_SEQS
    rows = seqs * steps
    bs = n // steps
    assert bs % seqs == 0
    win, wa2, wo = gla_consts[1], gla_consts[2], gla_consts[5]
    dk = wa2.shape[1]
    dkh, dvh = dk // GLA_HEADS, wo.shape[0] // GLA_HEADS
    tri, masks, levels = _gla_tables(rows, steps, rows)
    consts = tuple(gla_consts) + (_Const(tri), _Const(masks))
    state_block = seqs * GLA_HEADS * dkh * dvh * 4
    resident = _nbytes(consts) + 4 * rows * d * 4 + 4 * state_block
    resident += rows * (win.shape[1] * 4 + (levels + 4) * dk * 4)
    return _call_with_state_slot(
        functools.partial(_gla_sample_body, levels=levels, seqs=seqs, steps=steps),
        (x, state_gla) + _arrays(consts),
        [pl.BlockSpec((rows, d), lambda i: (i, 0)),
         pl.BlockSpec((None, seqs, GLA_HEADS, dkh, dvh), lambda i: (layer, i, 0, 0, 0))]
        + _specs(consts),
        grid=(bs // seqs,),
        out_shapes=[jax.ShapeDtypeStruct((n, d), F32)],
        out_specs=[pl.BlockSpec((rows, d), lambda i: (i, 0))],
        state_shape=(bs, GLA_HEADS, dkh, dvh), state_block=(seqs, GLA_HEADS, dkh, dvh),
        state_index=lambda i: (i, 0, 0, 0),
        slot=layer, n_slots=state_gla.shape[0], prev=prev,
        scratch_shapes=[], resident=resident, name="gla_sample")


def _ffn_tail_steps(s, n_total, z_ref, xs_ref, o_ref, os_ref, ffn_steps):
    @pl.when(s >= n_total)
    def _():
        ffn = {}
        _interleave(ffn_steps(ffn, jnp.where(s == n_total, z_ref[...], xs_ref[...])))

        @pl.when(s == n_total)
        def _():
            o_ref[...] = ffn["out"]

        @pl.when(s > n_total)
        def _():
            os_ref[...] = ffn["out"]


def _pool_ffn_body(x_ref, xs_ref, gm_ref, wp_ref, sc_ref, gfn_ref, gf_ref, w1f_ref, w3f_ref, w2f_ref,
                   o_ref, os_ref, ctx_ref, ext_ref, z_ref, w1_ref, w3_ref, w2_ref, st13_ref, st2_ref, sem13, sem2,
                   *, rows, n_tiles, n_total, final, ffn_layer):
    s = pl.program_id(0)
    d = x_ref.shape[-1]
    t = jnp.minimum(s, n_total - 1) % n_tiles

    @pl.when(s == 0)
    def _():
        _fetch_and_cast(_ffn_weight_chunks(ffn_layer, (w1f_ref, w3f_ref, w2f_ref), (w1_ref, w3_ref, w2_ref),
                                           (st13_ref, st13_ref, st2_ref), (sem13, sem13, sem2)))

    def ffn_steps(ffn, src=None):
        return (_ffn_steps(z_ref[...] if src is None else src, gfn_ref, w1_ref, w3_ref, w2_ref, gf_ref, ffn,
                           final=final, col_chunks=FFN_COL_CHUNKS), FFN_STEP_COST * rows // 256)

    def mixer_step(with_ffn):
        @pl.when(t == 0)
        def _():
            ext_ref[0:POOL_MAXW, :] = jnp.zeros((POOL_MAXW, d), F32)

        x = x_ref[...]
        h = _rmsnorm(x, gm_ref[...])
        ext_ref[POOL_MAXW:POOL_MAXW + rows, :] = h
        pos = t * rows + lax.broadcasted_iota(jnp.int32, (rows, 1), 0)

        def ext_read(k, cs):
            return ext_ref[POOL_MAXW - k:POOL_MAXW - k + rows, cs]

        mix, ffn = {}, {}
        streams = [(_pool_group_steps(ext_read, h, pos, wp_ref, sc_ref[...], mix), POOL_STEP_COST * rows // 256)]
        if with_ffn:
            streams.append(ffn_steps(ffn))
        _interleave(*streams)
        z_ref[...] = x + mix["y"]
        if with_ffn:
            o_ref[...] = ffn["out"]
        ext_ref[0:POOL_MAXW, :] = ext_ref[rows:rows + POOL_MAXW, :]

        @pl.when(t == n_tiles - 1)
        def _():
            ctx_ref[...] = ext_ref[rows + 1:rows + POOL_MAXW, :]

    @pl.when(s == 0)
    def _():
        mixer_step(False)

    @pl.when(jnp.logical_and(s > 0, s < n_total))
    def _():
        mixer_step(True)

    _ffn_tail_steps(s, n_total, z_ref, xs_ref, o_ref, os_ref, ffn_steps)


def _gla_ffn_body(x_ref, xs_ref, gm_ref, win_ref, wa2_ref, ba_ref, gn_ref, wo_ref, tri_ref, masks_ref,
                  gfn_ref, gf_ref, w1f_ref, w3f_ref, w2f_ref,
                  o_ref, os_ref, sout_ref, s_ref, z_ref, w1_ref, w3_ref, w2_ref, st13_ref, st2_ref, sem13, sem2,
                  *, levels, chunk, n_tiles, n_total, final, ffn_layer):
    s = pl.program_id(0)
    t = jnp.minimum(s, n_total - 1) % n_tiles

    @pl.when(s == 0)
    def _():
        _fetch_and_cast(_ffn_weight_chunks(ffn_layer, (w1f_ref, w3f_ref, w2f_ref), (w1_ref, w3_ref, w2_ref),
                                           (st13_ref, st13_ref, st2_ref), (sem13, sem13, sem2)))

    def ffn_steps(ffn, src=None):
        return (_ffn_steps(z_ref[...] if src is None else src, gfn_ref, w1_ref, w3_ref, w2_ref, gf_ref, ffn,
                           final=final, col_chunks=FFN_COL_CHUNKS), FFN_STEP_COST)

    def mixer_step(with_ffn):
        @pl.when(t == 0)
        def _():
            s_ref[...] = jnp.zeros(s_ref.shape, F32)

        states = [s_ref[hd] for hd in range(GLA_HEADS)]
        mix, ffn = {}, {}
        streams = [(_gla_prompt_steps(x_ref[...], states, gm_ref, win_ref, wa2_ref, ba_ref, gn_ref,
                                      wo_ref, tri_ref, masks_ref, mix, levels=levels, chunk=chunk), GLA_STEP_COST)]
        if with_ffn:
            streams.append(ffn_steps(ffn))
        _interleave(*streams)
        states = mix["states"]
        z_ref[...] = mix["z"]
        if with_ffn:
            o_ref[...] = ffn["out"]
        for hd in range(GLA_HEADS):
            s_ref[hd] = states[hd]

        @pl.when(t == n_tiles - 1)
        def _():
            for hd in range(GLA_HEADS):
                sout_ref[hd] = states[hd]

    @pl.when(s == 0)
    def _():
        mixer_step(False)

    @pl.when(jnp.logical_and(s > 0, s < n_total))
    def _():
        mixer_step(True)

    _ffn_tail_steps(s, n_total, z_ref, xs_ref, o_ref, os_ref, ffn_steps)


def _prompt_layer(kind, x, xs, mixer_consts, ffn_consts, *, batch, seq, final, slot, n_slots, prev):
    n, d = x.shape
    rows = PROMPT_ROWS[kind]
    n_tiles = seq // rows
    n_total = batch * n_tiles
    assert seq % rows == 0 and rows >= POOL_MAXW and rows % GLA_CHUNK == 0
    g_ffn, w1f, w3f, w2f, gf = ffn_consts
    dff = w1f.shape[1]
    ffn_vmem = (g_ffn, gf)
    stage13 = (FFN_STAGE_SLOTS, d // FFN_WEIGHT_CHUNKS, dff)
    stage2 = (FFN_STAGE_SLOTS, dff // FFN_WEIGHT_CHUNKS, d)
    assert d % FFN_WEIGHT_CHUNKS == 0 and (dff // FFN_WEIGHT_CHUNKS) % (2 * V7X_SUBLANES) == 0
    ffn_scratch = [pltpu.VMEM((d, dff), BF16), pltpu.VMEM((d, dff), BF16), pltpu.VMEM((dff, d), BF16),
                   pltpu.VMEM(stage13, F32), pltpu.VMEM(stage2, F32),
                   pltpu.SemaphoreType.DMA((FFN_STAGE_SLOTS,)), pltpu.SemaphoreType.DMA((FFN_STAGE_SLOTS,))]
    resident = _nbytes(mixer_consts) + 3 * d * dff * 2 + 4 * int(np.prod(stage13)) * 2
    resident += 6 * rows * d * 4 + 3 * rows * dff * 4
    ns = xs.shape[0]
    assert ns % rows == 0
    n_sample = ns // rows
    x_spec = pl.BlockSpec((rows, d), lambda s: (jnp.minimum(s, n_total - 1), 0))
    o_spec = pl.BlockSpec((rows, d), lambda s: (jnp.clip(s - 1, 0, n_total - 1), 0))
    xs_spec = pl.BlockSpec((rows, d), lambda s: (jnp.clip(s - n_total - 1, 0, n_sample - 1), 0))
    seq_of = lambda s: jnp.minimum(s, n_total - 1) // n_tiles
    carry = [pltpu.VMEM((rows, d), F32)]
    if kind == "pool":
        body = functools.partial(_pool_ffn_body, rows=rows, n_tiles=n_tiles, n_total=n_total, final=final,
                                 ffn_layer=w1f.layer)
        state_shape = (batch, POOL_BUF, d)
        state_block = (None, POOL_BUF, d)
        state_index = lambda s: (seq_of(s), 0, 0)
        scratch = [pltpu.VMEM((rows + POOL_MAXW, d), F32)] + carry
        resident += (rows + POOL_MAXW) * d * 4
        consts = tuple(mixer_consts) + ffn_vmem
    else:
        win, wa2, wo = mixer_consts[1], mixer_consts[2], mixer_consts[5]
        dk = wa2.shape[1]
        dkh, dvh = dk // GLA_HEADS, wo.shape[0] // GLA_HEADS
        tri, masks, levels = _gla_tables(rows, GLA_CHUNK, GLA_CHUNK)
        body = functools.partial(_gla_ffn_body, levels=levels, chunk=GLA_CHUNK, n_tiles=n_tiles,
                                 n_total=n_total, final=final, ffn_layer=w1f.layer)
        state_shape = (batch, GLA_HEADS, dkh, dvh)
        state_block = (None, GLA_HEADS, dkh, dvh)
        state_index = lambda s: (seq_of(s), 0, 0, 0)
        scratch = [pltpu.VMEM((GLA_HEADS, dkh, dvh), F32)] + carry
        tables = (_Const(tri), _Const(masks))
        resident += _nbytes(tables) + 3 * GLA_HEADS * dkh * dvh * 4
        resident += rows * (win.shape[1] * 4 + (levels + 4) * dk * 4)
        consts = tuple(mixer_consts) + tables + ffn_vmem
    hbm = pl.BlockSpec(memory_space=pl.ANY)
    return _call_with_state_slot(
        body, (x, xs) + _arrays(consts) + (w1f.array, w3f.array, w2f.array),
        [x_spec, xs_spec] + _specs(consts) + [hbm] * 3,
        grid=(n_total + 1 + n_sample,),
        out_shapes=[jax.ShapeDtypeStruct((n, d), F32), jax.ShapeDtypeStruct((ns, d), F32)],
        out_specs=[o_spec, xs_spec],
        state_shape=state_shape, state_block=state_block, state_index=state_index,
        slot=slot, n_slots=n_slots, prev=prev,
        scratch_shapes=scratch + ffn_scratch, resident=resident,
        name=kind + ("_ffn_final" if final else "_ffn") + "_prompt")


def kernel(x_prompt, x_sample, state_pool, state_gla, norm_mix, norm_ffn, norm_final, pool_w, pool_scale,
           gla_w_in, gla_w_a2, gla_b_a, gla_norm, gla_w_o, ffn_w1, ffn_w3, ffn_w2):
    batch, seq, d = x_prompt.shape
    bs, steps, _ = x_sample.shape
    depth = norm_mix.shape[0]

    xp = x_prompt.reshape(batch * seq, d)
    xs = x_sample.reshape(bs * steps, d)

    rows_of = lambda a: a.reshape(a.shape[0], 1, -1)
    w_pool, w_in, w_a2, w_o = (a.astype(BF16) for a in (pool_w, gla_w_in, gla_w_a2, gla_w_o))
    g_mix, g_ffn, sc_pool, b_a, g_gla = (rows_of(a) for a in (norm_mix, norm_ffn, pool_scale, gla_b_a, gla_norm))
    gf = _Const(norm_final.reshape(1, -1))

    n_pool, n_gla = state_pool.shape[0], state_gla.shape[0]
    new_pool_p = new_pool_s = new_gla_p = new_gla_s = None
    for i in range(depth):
        j = i // 2
        final = i == depth - 1
        g = _Const(g_mix, i)
        ffn_consts = (_Const(g_ffn, i), _Const(ffn_w1, i), _Const(ffn_w3, i), _Const(ffn_w2, i), gf)
        if i % 2 == 0:
            pool_consts = (g, _Const(w_pool, j), _Const(sc_pool, j))
            xs3, new_pool_s = _pool_sample(xs.reshape(bs, steps, d), state_pool, j, pool_consts, new_pool_s)
            xp, xs, new_pool_p = _prompt_layer("pool", xp, xs3.reshape(bs * steps, d), pool_consts, ffn_consts,
                                               batch=batch, seq=seq, final=final, slot=j, n_slots=n_pool,
                                               prev=new_pool_p)
        else:
            gla_consts = (g, _Const(w_in, j), _Const(w_a2, j), _Const(b_a, j), _Const(g_gla, j), _Const(w_o, j))
            xs, new_gla_s = _gla_sample(xs, state_gla, j, gla_consts, new_gla_s, steps=steps)
            xp, xs, new_gla_p = _prompt_layer("gla", xp, xs, gla_consts, ffn_consts, batch=batch, seq=seq,
                                              final=final, slot=j, n_slots=n_gla, prev=new_gla_p)

    return (xp.reshape(batch, seq, d), xs.reshape(bs, steps, d),
            new_pool_p, new_gla_p, new_pool_s, new_gla_s)
```

```python
import functools

import numpy as np
import jax
import jax.numpy as jnp
from jax import lax
from jax.experimental import pallas as pl
from jax.experimental.pallas import tpu as pltpu

F32 = jnp.float32
BF16 = jnp.bfloat16

PAST_LEN = 16384
POOL_WINDOWS = (2, 4, 8, 16)
POOL_MAXW = max(POOL_WINDOWS)
POOL_BUF = POOL_MAXW - 1
GLA_HEADS = 4
GLA_GATE_RANK = 16
GLA_GATE_TEMP = 16.0
EPS = 1e-6
LOG2_E = 1.4426950408889634

V7X_LANES = 128
V7X_SUBLANES = 8
V7X_VMEM_BYTES = 64 * 1024 * 1024

PROMPT_ROWS = {"pool": 256, "gla": 256}
GLA_CHUNK = 128
FFN_WEIGHT_CHUNKS = 8
FFN_STAGE_SLOTS = 3
POOL_SEQS = 16
GLA_SEQS = 8
FFN_COL_CHUNKS = (1024, 1024, 768)
GLA_LEVEL_COST = 300
GLA_SCORE_COST = 300
GLA_FINISH_COST = 200
GLA_SCORE_LEAD = 2
FFN_STEP_COST = 8648
GLA_STEP_COST = 5882 + 7 * GLA_LEVEL_COST + 8 * (GLA_SCORE_COST + GLA_FINISH_COST)
POOL_STEP_COST = 2220


def _vmem_limit(resident_bytes):
    return int(min(V7X_VMEM_BYTES - (4 << 20), 2 * resident_bytes + (16 << 20)))


class _Const:
    def __init__(self, array, layer=None):
        self.array = array
        self.layer = layer
        self.shape = tuple(array.shape if layer is None else array.shape[1:])
        self.nbytes = int(np.prod(self.shape)) * array.dtype.itemsize

    def spec(self):
        nd = len(self.shape)
        if self.layer is None:
            return pl.BlockSpec(self.shape, lambda *_: (0,) * nd, pipeline_mode=pl.Buffered(1))
        layer = self.layer
        return pl.BlockSpec((None,) + self.shape, lambda *_: (layer,) + (0,) * nd, pipeline_mode=pl.Buffered(1))


def _nbytes(consts):
    return sum(c.nbytes for c in consts)


def _arrays(consts):
    return tuple(c.array for c in consts)


def _specs(consts):
    return [c.spec() for c in consts]


def _call_with_state_slot(body, args, in_specs, *, grid, out_shapes, out_specs, state_shape, state_block,
                          state_index, slot, n_slots, prev, scratch_shapes, resident, name):
    n_args = len(args)
    n_outs = len(out_shapes)
    inner = body
    aliases = {}
    if prev is None:
        state_spec = pl.BlockSpec((n_slots,) + tuple(state_block), lambda *g: (0,) + tuple(state_index(*g)))
        resident += 2 * (n_slots - 1) * int(np.prod([b for b in state_block if b is not None])) * 4

        def body(*refs):
            refs = list(refs)
            full = refs[n_args + n_outs]
            for other in range(n_slots):
                if other != slot:
                    full[other] = jnp.zeros(full.shape[1:], F32)
            refs[n_args + n_outs] = full.at[slot]
            return inner(*refs)
    else:
        state_spec = pl.BlockSpec((None,) + tuple(state_block), lambda *g: (slot,) + tuple(state_index(*g)))

        def body(*refs):
            return inner(*refs[:n_args], *refs[n_args + 1:])

        args = tuple(args) + (prev,)
        in_specs = list(in_specs) + [pl.BlockSpec(memory_space=pl.ANY)]
        aliases = {n_args: n_outs}
    return pl.pallas_call(
        body,
        out_shape=tuple(out_shapes) + (jax.ShapeDtypeStruct((n_slots,) + tuple(state_shape), F32),),
        grid=grid,
        in_specs=in_specs,
        out_specs=tuple(out_specs) + (state_spec,),
        scratch_shapes=scratch_shapes,
        input_output_aliases=aliases,
        compiler_params=pltpu.CompilerParams(
            dimension_semantics=("arbitrary",) * len(grid), vmem_limit_bytes=_vmem_limit(resident)),
        name=name,
    )(*args)


def _rmsnorm(x, g):
    ms = jnp.mean(x * x, axis=-1, keepdims=True)
    return x * lax.rsqrt(ms + EPS) * g


def _silu(x):
    hx = 0.5 * x
    return hx + hx * jnp.tanh(hx)


def _dot(a, b):
    return jnp.dot(a, b, preferred_element_type=F32)


def _dot_nt(a, b):
    return lax.dot_general(a, b, (((1,), (1,)), ((), ())), preferred_element_type=F32)


def _dot_tn(a, b):
    return lax.dot_general(a, b, (((0,), (0,)), ((), ())), preferred_element_type=F32)


def _split3(x):
    hi = x.astype(BF16)
    r1 = x - hi.astype(F32)
    mid = r1.astype(BF16)
    lo = (r1 - mid.astype(F32)).astype(BF16)
    return hi, mid, lo


def _pool_group_steps(ext_read, h, pos, w_ref, sc, result):
    gw = h.shape[-1] // len(POOL_WINDOWS)
    ys = []
    for g, w in enumerate(POOL_WINDOWS):
        cs = slice(g * gw, (g + 1) * gw)
        hg = h[:, cs]
        s = hg
        for k in range(1, w):
            s = s + ext_read(k, cs)
        cnt = jnp.minimum(pos + 1, w).astype(F32)
        diff = (s / cnt - hg).astype(BF16)
        ys.append(_dot(diff, w_ref[g]))
        yield (100 + 70 * (w - 1)) * h.shape[0] // 256
    result["y"] = jnp.concatenate(ys, axis=-1) * sc


def _pool_groups(ext_read, h, pos, w_ref, sc):
    result = {}
    for _ in _pool_group_steps(ext_read, h, pos, w_ref, sc, result):
        pass
    return result["y"]


def _pool_sample_body(x_ref, ctx_ref, g_ref, w_ref, sc_ref, o_ref, nctx_ref, ext_ref, *, seqs, steps):
    d = x_ref.shape[-1]
    ext_ref[0] = jnp.zeros((seqs, d), F32)
    ext_ref[1:POOL_MAXW] = ctx_ref[...]
    xs = [x_ref[:, t, :] for t in range(steps)]
    for t in range(steps):
        ext_ref[POOL_MAXW + t] = _rmsnorm(xs[t], g_ref[...])
    h = ext_ref[POOL_MAXW:POOL_MAXW + steps].reshape(steps * seqs, d)
    pos = PAST_LEN + lax.broadcasted_iota(jnp.int32, (steps * seqs, 1), 0) // seqs

    def ext_read(k, cs):
        return ext_ref[POOL_MAXW - k:POOL_MAXW - k + steps, :, cs].reshape(steps * seqs, cs.stop - cs.start)

    y = _pool_groups(ext_read, h, pos, w_ref, sc_ref[...])
    for t in range(steps):
        o_ref[:, t, :] = xs[t] + y[t * seqs:(t + 1) * seqs]
    nctx_ref[...] = ext_ref[steps + 1:steps + POOL_MAXW]


def _pool_sample(x, state_pool_t, layer, pool_consts, prev):
    bs, steps, d = x.shape
    seqs = POOL_SEQS
    assert bs % seqs == 0 and steps == V7X_SUBLANES
    resident = 2 * seqs * (2 * steps + 2 * POOL_MAXW) * d * 4 + seqs * (steps + POOL_MAXW) * d * 4
    return _call_with_state_slot(
        functools.partial(_pool_sample_body, seqs=seqs, steps=steps),
        (x, state_pool_t) + _arrays(pool_consts),
        [pl.BlockSpec((seqs, steps, d), lambda i: (i, 0, 0)),
         pl.BlockSpec((None, POOL_BUF, seqs, d), lambda i: (layer, 0, i, 0))]
        + _specs(pool_consts),
        grid=(bs // seqs,),
        out_shapes=[jax.ShapeDtypeStruct((bs, steps, d), F32)],
        out_specs=[pl.BlockSpec((seqs, steps, d), lambda i: (i, 0, 0))],
        state_shape=(POOL_BUF, bs, d), state_block=(POOL_BUF, seqs, d), state_index=lambda i: (0, i, 0),
        slot=layer, n_slots=state_pool_t.shape[0], prev=prev,
        scratch_shapes=[pltpu.VMEM((steps + POOL_MAXW, seqs, d), F32)],
        resident=resident, name="pool_sample")


def _gla_tables(rows, seg, score_rows):
    levels = int(np.log2(seg))
    assert 2 ** levels == seg and rows % score_rows == 0 and score_rows % seg == 0
    idx = np.arange(rows)
    tri = (idx[None, :] >= (idx[:, None] // seg) * seg) & (idx[None, :] <= idx[:, None])
    row = np.arange(score_rows)[:, None]
    col = np.arange(score_rows)[None, :]
    masks = [row == col]
    for l in range(1, levels + 1):
        s = 2 ** l
        mid = (row // s) * s + s // 2 - 1
        masks.append((row // s == col // s) & (row > mid) & (col <= mid))
    masks = np.stack(masks, axis=0).astype(np.float32)
    return jnp.asarray(tri.astype(np.float32), BF16), jnp.asarray(masks, F32), levels


def _gla_front(x, g_ref, win_ref, wa2_ref, ba_ref, tri_ref):
    rank, dk = wa2_ref.shape
    dkh = dk // GLA_HEADS
    h = _rmsnorm(x, g_ref[...]).astype(BF16)
    a_low = _dot(h, win_ref[:, 6 * dk:6 * dk + rank])
    xg = _dot(a_low.astype(BF16), wa2_ref[...]) + ba_ref[...]
    la = -(jnp.maximum(-xg, 0.0) + jnp.log(1.0 + jnp.exp(-jnp.abs(xg)))) * (1.0 / GLA_GATE_TEMP)
    hi, mid, lo = _split3(la)
    tri = tri_ref[...]
    b = _dot(tri, hi) + _dot(tri, mid) + _dot(tri, lo)
    proj = _dot(h, win_ref[:, 0:6 * dk])
    q = proj[:, 0:dk] * (dkh ** -0.5)
    k = proj[:, dk:2 * dk]
    v = proj[:, 2 * dk:4 * dk]
    r = proj[:, 4 * dk:6 * dk]
    return q, k, v, r, b


def _gla_level_operand_steps(q16, k16, b, levels, ops):
    rows, dk = b.shape
    row = lax.broadcasted_iota(jnp.int32, (rows, 1), 0)
    sub = lax.broadcasted_iota(jnp.int32, (1, V7X_SUBLANES, 1), 1)
    b = b * LOG2_E
    for l in range(1, levels + 1):
        s = 2 ** l
        if s >= V7X_SUBLANES:
            b3 = b.reshape(rows // s, s, dk)
            m = b3[:, s // 2 - 1:s // 2, :]
        else:
            b3 = b.reshape(rows // V7X_SUBLANES, V7X_SUBLANES, dk)
            picks = [b3[:, j + s // 2 - 1:j + s // 2, :] for j in range(0, V7X_SUBLANES, s)]
            m = picks[-1]
            for idx in range(len(picks) - 2, -1, -1):
                m = jnp.where(sub < (idx + 1) * s, picks[idx], m)
        e = jnp.exp2(-jnp.abs(b3 - m)).reshape(rows, dk).astype(BF16)
        second = ((row >> (l - 1)) & 1) == 1
        ops.append(jnp.where(second, q16, k16) * e)
        yield GLA_LEVEL_COST


def _gla_level_operands(q16, k16, b, levels):
    ops = []
    for _ in _gla_level_operand_steps(q16, k16, b, levels, ops):
        pass
    return ops


def _gla_intra(qb16_h, kb16_h, ops, masks_ref, rs, hs):
    a = _dot_nt(qb16_h, kb16_h) * masks_ref[0]
    for l, t in enumerate(ops, start=1):
        th = t[rs, hs]
        a = a + _dot_nt(th, th) * masks_ref[l]
    return a


def _lane_column(row_vec):
    return jnp.transpose(row_vec, (1, 0))


def _gla_out(x, o, r, gn_ref, wo_ref):
    dv = o.shape[-1]
    dvh = dv // GLA_HEADS
    normed = []
    for hd in range(GLA_HEADS):
        oh = o[:, hd * dvh:(hd + 1) * dvh]
        normed.append(oh * lax.rsqrt(jnp.mean(oh * oh, axis=-1, keepdims=True) + EPS))
    o = jnp.concatenate(normed, axis=-1) * gn_ref[...]
    o = (o * _silu(r)).astype(BF16)
    return x + _dot(o, wo_ref[...])


def _gla_prompt_steps(x, states, g_ref, win_ref, wa2_ref, ba_ref, gn_ref, wo_ref,
                      tri_ref, masks_ref, result, *, levels, chunk):
    rows = x.shape[0]
    rank, dk = wa2_ref.shape
    dkh = dk // GLA_HEADS
    h = _rmsnorm(x, g_ref[...]).astype(BF16)
    a_low = _dot(h, win_ref[:, 6 * dk:6 * dk + rank])
    yield 150
    q = _dot(h, win_ref[:, 0:dk]) * (dkh ** -0.5)
    q16 = q.astype(BF16)
    yield 512
    xg = _dot(a_low.astype(BF16), wa2_ref[...]) + ba_ref[...]
    yield 150
    k = _dot(h, win_ref[:, dk:2 * dk])
    k16 = k.astype(BF16)
    yield 512
    la = -(jnp.maximum(-xg, 0.0) + jnp.log(1.0 + jnp.exp(-jnp.abs(xg)))) * (1.0 / GLA_GATE_TEMP)
    hi, mid, lo = _split3(la)
    yield 400
    vb = _dot(h, win_ref[:, 2 * dk:4 * dk]).astype(BF16)
    yield 1024
    dvh = vb.shape[-1] // GLA_HEADS
    tri = tri_ref[...]
    b = _dot(tri, hi) + _dot(tri, mid) + _dot(tri, lo)
    yield 350
    r = _dot(h, win_ref[:, 4 * dk:6 * dk])
    yield 1024
    ops = []
    yield from _gla_level_operand_steps(q16, k16, b, levels, ops)
    qb = (q * jnp.exp(b)).astype(BF16)
    yield 60

    states = list(states)
    pieces = [(ci, hd) for ci in range(rows // chunk) for hd in range(GLA_HEADS)]
    kends = {}
    outs = {}

    def scores(ci, hd):
        rs = slice(ci * chunk, (ci + 1) * chunk)
        hs = slice(hd * dkh, (hd + 1) * dkh)
        return _gla_intra(q16[rs, hs], k16[rs, hs], ops, masks_ref, rs, hs).astype(BF16)

    def finish(ci, hd, a16):
        rs = slice(ci * chunk, (ci + 1) * chunk)
        hs = slice(hd * dkh, (hd + 1) * dkh)
        vs = slice(hd * dvh, (hd + 1) * dvh)
        b_c = b[rs]
        b_last = b_c[chunk - 1:chunk, :]
        if ci not in kends:
            kends[ci] = (k[rs] * jnp.exp(b_last - b_c)).astype(BF16)
        s_old = states[hd]
        outs[ci, hd] = _dot(a16, vb[rs, vs]) + _dot(qb[rs, hs], s_old.astype(BF16))
        dec = jnp.exp(_lane_column(b_last[:, hs]))
        states[hd] = dec * s_old + _dot_tn(kends[ci][:, hs], vb[rs, vs])

    pending = []
    for piece in pieces:
        pending.append((*piece, scores(*piece)))
        yield GLA_SCORE_COST
        if len(pending) > GLA_SCORE_LEAD:
            finish(*pending.pop(0))
            yield GLA_FINISH_COST
    while pending:
        finish(*pending.pop(0))
        yield GLA_FINISH_COST
    o = jnp.concatenate(
        [jnp.concatenate([outs[ci, hd] for hd in range(GLA_HEADS)], axis=-1) for ci in range(rows // chunk)],
        axis=0)
    result["z"] = _gla_out(x, o, r, gn_ref, wo_ref)
    result["states"] = states
    yield 1700


def _ffn_steps(x, g_ref, w1_ref, w3_ref, w2_ref, gf_ref, result, *, final, col_chunks):
    rows = x.shape[0]
    h = _rmsnorm(x, g_ref[...]).astype(BF16)
    yield 200
    acc = x
    lo = 0
    pending = None
    for cols in col_chunks:
        cs = slice(lo, lo + cols)
        lo += cols
        u1 = _dot(h, w1_ref[:, cs])
        yield rows * cols // 256
        u3 = _dot(h, w3_ref[:, cs])
        yield rows * cols // 256
        if pending is not None:
            acc = acc + _dot(pending[0], w2_ref[pending[1], :])
            yield rows * (pending[1].stop - pending[1].start) // 256
        pending = ((_silu(u1) * u3).astype(BF16), cs)
    acc = acc + _dot(pending[0], w2_ref[pending[1], :])
    yield rows * (pending[1].stop - pending[1].start) // 256
    if final:
        acc = _rmsnorm(acc, gf_ref[...])
    result["out"] = acc


def _ffn_weight_chunks(layer, wf_refs, ws_refs, stage_refs, sem_refs):
    chunks = []
    used = {}
    for wf, ws, stage, sem in zip(wf_refs, ws_refs, stage_refs, sem_refs):
        n_rows = ws.shape[0] // FFN_WEIGHT_CHUNKS
        for i in range(FFN_WEIGHT_CHUNKS):
            rs = pl.ds(i * n_rows, n_rows)
            slot = used.get(id(stage), 0) % FFN_STAGE_SLOTS
            used[id(stage)] = used.get(id(stage), 0) + 1
            chunks.append((wf.at[layer, rs, :], stage.at[slot], sem.at[slot], ws.at[rs, :]))
    return chunks


def _fetch_and_cast(chunks):
    ahead = FFN_STAGE_SLOTS - 1
    copies = [pltpu.make_async_copy(src, stage, sem) for src, stage, sem, _ in chunks]
    for i in range(min(ahead, len(chunks))):
        copies[i].start()
    for i, (_, stage, _, dst) in enumerate(chunks):
        if i + ahead < len(chunks):
            copies[i + ahead].start()
        copies[i].wait()
        dst[...] = stage[...].astype(BF16)


def _interleave(*pairs):
    gens = [p[0] for p in pairs]
    totals = [float(p[1]) for p in pairs]
    done = [0.0] * len(gens)
    live = list(range(len(gens)))
    while live:
        i = min(live, key=lambda j: done[j] / totals[j])
        try:
            done[i] += next(gens[i])
        except StopIteration:
            live.remove(i)


def _gla_sample_body(x_ref, s0_ref, g_ref, win_ref, wa2_ref, ba_ref, gn_ref, wo_ref,
                     tri_ref, masks_ref, o_ref, sout_ref, *, levels, seqs, steps):
    x = x_ref[...]
    rows = x.shape[0]
    q, k, v, r, b = _gla_front(x, g_ref, win_ref, wa2_ref, ba_ref, tri_ref)
    dk = q.shape[-1]
    dkh = dk // GLA_HEADS
    dvh = v.shape[-1] // GLA_HEADS
    q16 = q.astype(BF16)
    k16 = k.astype(BF16)
    ops = _gla_level_operands(q16, k16, b, levels)
    b3 = b.reshape(seqs, steps, dk)
    b_last = b3[:, steps - 1:steps, :]
    qb = (q * jnp.exp(b)).astype(BF16)
    kend = (k.reshape(seqs, steps, dk) * jnp.exp(b_last - b3)).reshape(rows, dk).astype(BF16)
    vb = v.astype(BF16)
    all_rows = slice(0, rows)
    outs = []
    for hd in range(GLA_HEADS):
        hs = slice(hd * dkh, (hd + 1) * dkh)
        vs = slice(hd * dvh, (hd + 1) * dvh)
        a = _gla_intra(q16[:, hs], k16[:, hs], ops, masks_ref, all_rows, hs)
        o_h = _dot(a.astype(BF16), vb[:, vs])
        o_state = []
        for s in range(seqs):
            rs = slice(s * steps, (s + 1) * steps)
            s_old = s0_ref[s, hd]
            o_state.append(_dot(qb[rs, hs], s_old.astype(BF16)))
            dec = jnp.exp(_lane_column(b[(s + 1) * steps - 1:(s + 1) * steps, hs]))
            sout_ref[s, hd] = dec * s_old + _dot_tn(kend[rs, hs], vb[rs, vs])
        outs.append(o_h + jnp.concatenate(o_state, axis=0))
    o_ref[...] = _gla_out(x, jnp.concatenate(outs, axis=-1), r, gn_ref, wo_ref)


def _gla_sample(x, state_gla, layer, gla_consts, prev, *, steps):
    n, d = x.shape
    seqs = GLA_SEQS if prev is None else 2 * GLA_SEQS
    rows = seqs * steps
    bs = n // steps
    assert bs % seqs == 0
    win, wa2, wo = gla_consts[1], gla_consts[2], gla_consts[5]
    dk = wa2.shape[1]
    dkh, dvh = dk // GLA_HEADS, wo.shape[0] // GLA_HEADS
    tri, masks, levels = _gla_tables(rows, steps, rows)
    consts = tuple(gla_consts) + (_Const(tri), _Const(masks))
    state_block = seqs * GLA_HEADS * dkh * dvh * 4
    resident = _nbytes(consts) + 4 * rows * d * 4 + 4 * state_block
    resident += rows * (win.shape[1] * 4 + (levels + 4) * dk * 4)
    return _call_with_state_slot(
        functools.partial(_gla_sample_body, levels=levels, seqs=seqs, steps=steps),
        (x, state_gla) + _arrays(consts),
        [pl.BlockSpec((rows, d), lambda i: (i, 0)),
         pl.BlockSpec((None, seqs, GLA_HEADS, dkh, dvh), lambda i: (layer, i, 0, 0, 0))]
        + _specs(consts),
        grid=(bs // seqs,),
        out_shapes=[jax.ShapeDtypeStruct((n, d), F32)],
        out_specs=[pl.BlockSpec((rows, d), lambda i: (i, 0))],
        state_shape=(bs, GLA_HEADS, dkh, dvh), state_block=(seqs, GLA_HEADS, dkh, dvh),
        state_index=lambda i: (i, 0, 0, 0),
        slot=layer, n_slots=state_gla.shape[0], prev=prev,
        scratch_shapes=[], resident=resident, name="gla_sample")


def _ffn_tail_steps(s, n_total, z_ref, xs_ref, o_ref, os_ref, ffn_steps):
    @pl.when(s >= n_total)
    def _():
        ffn = {}
        _interleave(ffn_steps(ffn, jnp.where(s == n_total, z_ref[...], xs_ref[...])))

        @pl.when(s == n_total)
        def _():
            o_ref[...] = ffn["out"]

        @pl.when(s > n_total)
        def _():
            os_ref[...] = ffn["out"]


def _pool_ffn_body(x_ref, xs_ref, gm_ref, wp_ref, sc_ref, gfn_ref, gf_ref, w1f_ref, w3f_ref, w2f_ref,
                   o_ref, os_ref, ctx_ref, ext_ref, z_ref, w1_ref, w3_ref, w2_ref, st13_ref, st2_ref, sem13, sem2,
                   *, rows, n_tiles, n_total, final, ffn_layer):
    s = pl.program_id(0)
    d = x_ref.shape[-1]
    t = jnp.minimum(s, n_total - 1) % n_tiles

    @pl.when(s == 0)
    def _():
        _fetch_and_cast(_ffn_weight_chunks(ffn_layer, (w1f_ref, w3f_ref, w2f_ref), (w1_ref, w3_ref, w2_ref),
                                           (st13_ref, st13_ref, st2_ref), (sem13, sem13, sem2)))

    def ffn_steps(ffn, src=None):
        return (_ffn_steps(z_ref[...] if src is None else src, gfn_ref, w1_ref, w3_ref, w2_ref, gf_ref, ffn,
                           final=final, col_chunks=FFN_COL_CHUNKS), FFN_STEP_COST * rows // 256)

    def mixer_step(with_ffn):
        @pl.when(t == 0)
        def _():
            ext_ref[0:POOL_MAXW, :] = jnp.zeros((POOL_MAXW, d), F32)

        x = x_ref[...]
        h = _rmsnorm(x, gm_ref[...])
        ext_ref[POOL_MAXW:POOL_MAXW + rows, :] = h
        pos = t * rows + lax.broadcasted_iota(jnp.int32, (rows, 1), 0)

        def ext_read(k, cs):
            return ext_ref[POOL_MAXW - k:POOL_MAXW - k + rows, cs]

        mix, ffn = {}, {}
        streams = [(_pool_group_steps(ext_read, h, pos, wp_ref, sc_ref[...], mix), POOL_STEP_COST * rows // 256)]
        if with_ffn:
            streams.append(ffn_steps(ffn))
        _interleave(*streams)
        z_ref[...] = x + mix["y"]
        if with_ffn:
            o_ref[...] = ffn["out"]
        ext_ref[0:POOL_MAXW, :] = ext_ref[rows:rows + POOL_MAXW, :]

        @pl.when(t == n_tiles - 1)
        def _():
            ctx_ref[...] = ext_ref[rows + 1:rows + POOL_MAXW, :]

    @pl.when(s == 0)
    def _():
        mixer_step(False)

    @pl.when(jnp.logical_and(s > 0, s < n_total))
    def _():
        mixer_step(True)

    _ffn_tail_steps(s, n_total, z_ref, xs_ref, o_ref, os_ref, ffn_steps)


def _gla_ffn_body(x_ref, xs_ref, gm_ref, win_ref, wa2_ref, ba_ref, gn_ref, wo_ref, tri_ref, masks_ref,
                  gfn_ref, gf_ref, w1f_ref, w3f_ref, w2f_ref,
                  o_ref, os_ref, sout_ref, s_ref, z_ref, w1_ref, w3_ref, w2_ref, st13_ref, st2_ref, sem13, sem2,
                  *, levels, chunk, n_tiles, n_total, final, ffn_layer):
    s = pl.program_id(0)
    t = jnp.minimum(s, n_total - 1) % n_tiles

    @pl.when(s == 0)
    def _():
        _fetch_and_cast(_ffn_weight_chunks(ffn_layer, (w1f_ref, w3f_ref, w2f_ref), (w1_ref, w3_ref, w2_ref),
                                           (st13_ref, st13_ref, st2_ref), (sem13, sem13, sem2)))

    def ffn_steps(ffn, src=None):
        return (_ffn_steps(z_ref[...] if src is None else src, gfn_ref, w1_ref, w3_ref, w2_ref, gf_ref, ffn,
                           final=final, col_chunks=FFN_COL_CHUNKS), FFN_STEP_COST)

    def mixer_step(with_ffn):
        @pl.when(t == 0)
        def _():
            s_ref[...] = jnp.zeros(s_ref.shape, F32)

        states = [s_ref[hd] for hd in range(GLA_HEADS)]
        mix, ffn = {}, {}
        streams = [(_gla_prompt_steps(x_ref[...], states, gm_ref, win_ref, wa2_ref, ba_ref, gn_ref,
                                      wo_ref, tri_ref, masks_ref, mix, levels=levels, chunk=chunk), GLA_STEP_COST)]
        if with_ffn:
            streams.append(ffn_steps(ffn))
        _interleave(*streams)
        states = mix["states"]
        z_ref[...] = mix["z"]
        if with_ffn:
            o_ref[...] = ffn["out"]
        for hd in range(GLA_HEADS):
            s_ref[hd] = states[hd]

        @pl.when(t == n_tiles - 1)
        def _():
            for hd in range(GLA_HEADS):
                sout_ref[hd] = states[hd]

    @pl.when(s == 0)
    def _():
        mixer_step(False)

    @pl.when(jnp.logical_and(s > 0, s < n_total))
    def _():
        mixer_step(True)

    _ffn_tail_steps(s, n_total, z_ref, xs_ref, o_ref, os_ref, ffn_steps)


def _prompt_layer(kind, x, xs, mixer_consts, ffn_consts, *, batch, seq, final, slot, n_slots, prev):
    n, d = x.shape
    rows = PROMPT_ROWS[kind]
    n_tiles = seq // rows
    n_total = batch * n_tiles
    assert seq % rows == 0 and rows >= POOL_MAXW and rows % GLA_CHUNK == 0
    g_ffn, w1f, w3f, w2f, gf = ffn_consts
    dff = w1f.shape[1]
    ffn_vmem = (g_ffn, gf)
    stage13 = (FFN_STAGE_SLOTS, d // FFN_WEIGHT_CHUNKS, dff)
    stage2 = (FFN_STAGE_SLOTS, dff // FFN_WEIGHT_CHUNKS, d)
    assert d % FFN_WEIGHT_CHUNKS == 0 and (dff // FFN_WEIGHT_CHUNKS) % (2 * V7X_SUBLANES) == 0
    ffn_scratch = [pltpu.VMEM((d, dff), BF16), pltpu.VMEM((d, dff), BF16), pltpu.VMEM((dff, d), BF16),
                   pltpu.VMEM(stage13, F32), pltpu.VMEM(stage2, F32),
                   pltpu.SemaphoreType.DMA((FFN_STAGE_SLOTS,)), pltpu.SemaphoreType.DMA((FFN_STAGE_SLOTS,))]
    resident = _nbytes(mixer_consts) + 3 * d * dff * 2 + 4 * int(np.prod(stage13)) * 2
    resident += 6 * rows * d * 4 + 3 * rows * dff * 4
    ns = xs.shape[0]
    assert ns % rows == 0
    n_sample = ns // rows
    x_spec = pl.BlockSpec((rows, d), lambda s: (jnp.minimum(s, n_total - 1), 0))
    o_spec = pl.BlockSpec((rows, d), lambda s: (jnp.clip(s - 1, 0, n_total - 1), 0))
    xs_spec = pl.BlockSpec((rows, d), lambda s: (jnp.clip(s - n_total - 1, 0, n_sample - 1), 0))
    seq_of = lambda s: jnp.minimum(s, n_total - 1) // n_tiles
    carry = [pltpu.VMEM((rows, d), F32)]
    if kind == "pool":
        body = functools.partial(_pool_ffn_body, rows=rows, n_tiles=n_tiles, n_total=n_total, final=final,
                                 ffn_layer=w1f.layer)
        state_shape = (batch, POOL_BUF, d)
        state_block = (None, POOL_BUF, d)
        state_index = lambda s: (seq_of(s), 0, 0)
        scratch = [pltpu.VMEM((rows + POOL_MAXW, d), F32)] + carry
        resident += (rows + POOL_MAXW) * d * 4
        consts = tuple(mixer_consts) + ffn_vmem
    else:
        win, wa2, wo = mixer_consts[1], mixer_consts[2], mixer_consts[5]
        dk = wa2.shape[1]
        dkh, dvh = dk // GLA_HEADS, wo.shape[0] // GLA_HEADS
        tri, masks, levels = _gla_tables(rows, GLA_CHUNK, GLA_CHUNK)
        body = functools.partial(_gla_ffn_body, levels=levels, chunk=GLA_CHUNK, n_tiles=n_tiles,
                                 n_total=n_total, final=final, ffn_layer=w1f.layer)
        state_shape = (batch, GLA_HEADS, dkh, dvh)
        state_block = (None, GLA_HEADS, dkh, dvh)
        state_index = lambda s: (seq_of(s), 0, 0, 0)
        scratch = [pltpu.VMEM((GLA_HEADS, dkh, dvh), F32)] + carry
        tables = (_Const(tri), _Const(masks))
        resident += _nbytes(tables) + 3 * GLA_HEADS * dkh * dvh * 4
        resident += rows * (win.shape[1] * 4 + (levels + 4) * dk * 4)
        consts = tuple(mixer_consts) + tables + ffn_vmem
    hbm = pl.BlockSpec(memory_space=pl.ANY)
    return _call_with_state_slot(
        body, (x, xs) + _arrays(consts) + (w1f.array, w3f.array, w2f.array),
        [x_spec, xs_spec] + _specs(consts) + [hbm] * 3,
        grid=(n_total + 1 + n_sample,),
        out_shapes=[jax.ShapeDtypeStruct((n, d), F32), jax.ShapeDtypeStruct((ns, d), F32)],
        out_specs=[o_spec, xs_spec],
        state_shape=state_shape, state_block=state_block, state_index=state_index,
        slot=slot, n_slots=n_slots, prev=prev,
        scratch_shapes=scratch + ffn_scratch, resident=resident,
        name=kind + ("_ffn_final" if final else "_ffn") + "_prompt")


def kernel(x_prompt, x_sample, state_pool, state_gla, norm_mix, norm_ffn, norm_final, pool_w, pool_scale,
           gla_w_in, gla_w_a2, gla_b_a, gla_norm, gla_w_o, ffn_w1, ffn_w3, ffn_w2):
    batch, seq, d = x_prompt.shape
    bs, steps, _ = x_sample.shape
    depth = norm_mix.shape[0]

    xp = x_prompt.reshape(batch * seq, d)
    xs = x_sample.reshape(bs * steps, d)

    rows_of = lambda a: a.reshape(a.shape[0], 1, -1)
    w_pool, w_in, w_a2, w_o = (a.astype(BF16) for a in (pool_w, gla_w_in, gla_w_a2, gla_w_o))
    g_mix, g_ffn, sc_pool, b_a, g_gla = (rows_of(a) for a in (norm_mix, norm_ffn, pool_scale, gla_b_a, gla_norm))
    gf = _Const(norm_final.reshape(1, -1))

    n_pool, n_gla = state_pool.shape[0], state_gla.shape[0]
    state_pool_t = jnp.swapaxes(state_pool, 1, 2)
    new_pool_p = new_pool_s = new_gla_p = new_gla_s = None
    for i in range(depth):
        j = i // 2
        final = i == depth - 1
        g = _Const(g_mix, i)
        ffn_consts = (_Const(g_ffn, i), _Const(ffn_w1, i), _Const(ffn_w3, i), _Const(ffn_w2, i), gf)
        if i % 2 == 0:
            pool_consts = (g, _Const(w_pool, j), _Const(sc_pool, j))
            xs3, new_pool_s = _pool_sample(xs.reshape(bs, steps, d), state_pool_t, j, pool_consts, new_pool_s)
            xp, xs, new_pool_p = _prompt_layer("pool", xp, xs3.reshape(bs * steps, d), pool_consts, ffn_consts,
                                               batch=batch, seq=seq, final=final, slot=j, n_slots=n_pool,
                                               prev=new_pool_p)
        else:
            gla_consts = (g, _Const(w_in, j), _Const(w_a2, j), _Const(b_a, j), _Const(g_gla, j), _Const(w_o, j))
            xs, new_gla_s = _gla_sample(xs, state_gla, j, gla_consts, new_gla_s, steps=steps)
            xp, xs, new_gla_p = _prompt_layer("gla", xp, xs, gla_consts, ffn_consts, batch=batch, seq=seq,
                                              final=final, slot=j, n_slots=n_gla, prev=new_gla_p)

    return (xp.reshape(batch, seq, d), xs.reshape(bs, steps, d),
            new_pool_p, new_gla_p, jnp.swapaxes(new_pool_s, 1, 2), new_gla_s)
```

```python
import functools

import numpy as np
import jax
import jax.numpy as jnp
from jax import lax
from jax.experimental import pallas as pl
from jax.experimental.pallas import tpu as pltpu

F32 = jnp.float32
BF16 = jnp.bfloat16

PAST_LEN = 16384
POOL_WINDOWS = (2, 4, 8, 16)
POOL_MAXW = max(POOL_WINDOWS)
POOL_BUF = POOL_MAXW - 1
GLA_HEADS = 4
GLA_GATE_TEMP = 16.0
EPS = 1e-6
LOG2_E = 1.4426950408889634

V7X_SUBLANES = 8
V7X_VMEM_BYTES = 64 * 1024 * 1024

PROMPT_ROWS = {"pool": 256, "gla": 256}
GLA_CHUNK = 128
FFN_WEIGHT_CHUNKS = 8
FFN_STAGE_SLOTS = 4
POOL_SEQS = 32
GLA_SEQS = 8
FFN_COL_CHUNKS = (1024, 1024, 768)
GLA_LEVEL_COST = 300
GLA_SCORE_COST = 300
GLA_FINISH_COST = 200
GLA_SCORE_LEAD = 2
FFN_STEP_COST = 8648
GLA_STEP_COST = 5882 + 7 * GLA_LEVEL_COST + 8 * (GLA_SCORE_COST + GLA_FINISH_COST)
POOL_STEP_COST = 2220


def _vmem_limit(resident_bytes):
    return int(min(V7X_VMEM_BYTES - (4 << 20), 2 * resident_bytes + (16 << 20)))


class _Const:
    def __init__(self, array, layer=None):
        self.array = array
        self.layer = layer
        self.shape = tuple(array.shape if layer is None else array.shape[1:])
        self.nbytes = int(np.prod(self.shape)) * array.dtype.itemsize

    def spec(self):
        nd = len(self.shape)
        if self.layer is None:
            return pl.BlockSpec(self.shape, lambda *_: (0,) * nd, pipeline_mode=pl.Buffered(1))
        layer = self.layer
        return pl.BlockSpec((None,) + self.shape, lambda *_: (layer,) + (0,) * nd, pipeline_mode=pl.Buffered(1))


def _nbytes(consts):
    return sum(c.nbytes for c in consts)


def _arrays(consts):
    return tuple(c.array for c in consts)


def _specs(consts):
    return [c.spec() for c in consts]


def _call_with_state_slot(body, args, in_specs, *, grid, out_shapes, out_specs, state_shape, state_block,
                          state_index, slot, n_slots, prev, scratch_shapes, resident, name):
    n_args = len(args)
    n_outs = len(out_shapes)
    inner = body
    aliases = {}
    if prev is None:
        state_spec = pl.BlockSpec((n_slots,) + tuple(state_block), lambda *g: (0,) + tuple(state_index(*g)))
        resident += 2 * (n_slots - 1) * int(np.prod([b for b in state_block if b is not None])) * 4

        def body(*refs):
            refs = list(refs)
            full = refs[n_args + n_outs]
            for other in range(n_slots):
                if other != slot:
                    full[other] = jnp.zeros(full.shape[1:], F32)
            refs[n_args + n_outs] = full.at[slot]
            return inner(*refs)
    else:
        state_spec = pl.BlockSpec((None,) + tuple(state_block), lambda *g: (slot,) + tuple(state_index(*g)))

        def body(*refs):
            return inner(*refs[:n_args], *refs[n_args + 1:])

        args = tuple(args) + (prev,)
        in_specs = list(in_specs) + [pl.BlockSpec(memory_space=pl.ANY)]
        aliases = {n_args: n_outs}
    return pl.pallas_call(
        body,
        out_shape=tuple(out_shapes) + (jax.ShapeDtypeStruct((n_slots,) + tuple(state_shape), F32),),
        grid=grid,
        in_specs=in_specs,
        out_specs=tuple(out_specs) + (state_spec,),
        scratch_shapes=scratch_shapes,
        input_output_aliases=aliases,
        compiler_params=pltpu.CompilerParams(
            dimension_semantics=("arbitrary",) * len(grid), vmem_limit_bytes=_vmem_limit(resident)),
        name=name,
    )(*args)


def _rmsnorm(x, g):
    ms = jnp.mean(x * x, axis=-1, keepdims=True)
    return x * lax.rsqrt(ms + EPS) * g


def _silu(x):
    hx = 0.5 * x
    return hx + hx * jnp.tanh(hx)


def _dot(a, b):
    return jnp.dot(a, b, preferred_element_type=F32)


def _dot_nt(a, b):
    return lax.dot_general(a, b, (((1,), (1,)), ((), ())), preferred_element_type=F32)


def _dot_tn(a, b):
    return lax.dot_general(a, b, (((0,), (0,)), ((), ())), preferred_element_type=F32)


def _split3(x):
    hi = x.astype(BF16)
    r1 = x - hi.astype(F32)
    mid = r1.astype(BF16)
    lo = (r1 - mid.astype(F32)).astype(BF16)
    return hi, mid, lo


def _pool_group_steps(ext_read, h, pos, w_ref, sc, result):
    gw = h.shape[-1] // len(POOL_WINDOWS)
    ys = []
    for g, w in enumerate(POOL_WINDOWS):
        cs = slice(g * gw, (g + 1) * gw)
        hg = h[:, cs]
        s = hg
        for k in range(1, w):
            s = s + ext_read(k, cs)
        cnt = jnp.minimum(pos + 1, w).astype(F32)
        diff = (s / cnt - hg).astype(BF16)
        ys.append(_dot(diff, w_ref[g]))
        yield (100 + 70 * (w - 1)) * h.shape[0] // 256
    result["y"] = jnp.concatenate(ys, axis=-1) * sc


def _pool_groups(ext_read, h, pos, w_ref, sc):
    result = {}
    for _ in _pool_group_steps(ext_read, h, pos, w_ref, sc, result):
        pass
    return result["y"]


def _pool_sample_body(x_ref, ctx_ref, g_ref, w_ref, sc_ref, o_ref, nctx_ref, ext_ref, *, seqs, steps):
    d = x_ref.shape[-1]
    ext_ref[0] = jnp.zeros((seqs, d), F32)
    ext_ref[1:POOL_MAXW] = ctx_ref[...]
    xs = [x_ref[:, t, :] for t in range(steps)]
    for t in range(steps):
        ext_ref[POOL_MAXW + t] = _rmsnorm(xs[t], g_ref[...])
    h = ext_ref[POOL_MAXW:POOL_MAXW + steps].reshape(steps * seqs, d)
    pos = PAST_LEN + lax.broadcasted_iota(jnp.int32, (steps * seqs, 1), 0) // seqs

    def ext_read(k, cs):
        return ext_ref[POOL_MAXW - k:POOL_MAXW - k + steps, :, cs].reshape(steps * seqs, cs.stop - cs.start)

    y = _pool_groups(ext_read, h, pos, w_ref, sc_ref[...])
    for t in range(steps):
        o_ref[:, t, :] = xs[t] + y[t * seqs:(t + 1) * seqs]
    nctx_ref[...] = ext_ref[steps + 1:steps + POOL_MAXW]


def _pool_sample(x, state_pool_t, layer, pool_consts, prev):
    bs, steps, d = x.shape
    seqs = POOL_SEQS
    assert bs % seqs == 0 and steps == V7X_SUBLANES
    resident = 2 * seqs * (2 * steps + 2 * POOL_MAXW) * d * 4 + seqs * (steps + POOL_MAXW) * d * 4
    return _call_with_state_slot(
        functools.partial(_pool_sample_body, seqs=seqs, steps=steps),
        (x, state_pool_t) + _arrays(pool_consts),
        [pl.BlockSpec((seqs, steps, d), lambda i: (i, 0, 0)),
         pl.BlockSpec((None, POOL_BUF, seqs, d), lambda i: (layer, 0, i, 0))]
        + _specs(pool_consts),
        grid=(bs // seqs,),
        out_shapes=[jax.ShapeDtypeStruct((bs, steps, d), F32)],
        out_specs=[pl.BlockSpec((seqs, steps, d), lambda i: (i, 0, 0))],
        state_shape=(POOL_BUF, bs, d), state_block=(POOL_BUF, seqs, d), state_index=lambda i: (0, i, 0),
        slot=layer, n_slots=state_pool_t.shape[0], prev=prev,
        scratch_shapes=[pltpu.VMEM((steps + POOL_MAXW, seqs, d), F32)],
        resident=resident, name="pool_sample")


def _gla_tables(rows, seg, score_rows):
    levels = int(np.log2(seg))
    assert 2 ** levels == seg and rows % score_rows == 0 and score_rows % seg == 0
    idx = np.arange(rows)
    tri = (idx[None, :] >= (idx[:, None] // seg) * seg) & (idx[None, :] <= idx[:, None])
    row = np.arange(score_rows)[:, None]
    col = np.arange(score_rows)[None, :]
    masks = [row == col]
    for l in range(1, levels + 1):
        s = 2 ** l
        mid = (row // s) * s + s // 2 - 1
        masks.append((row // s == col // s) & (row > mid) & (col <= mid))
    masks = np.stack(masks, axis=0).astype(np.float32)
    return jnp.asarray(tri.astype(np.float32), BF16), jnp.asarray(masks, F32), levels


def _gla_front(x, g_ref, win_ref, wa2_ref, ba_ref, tri_ref):
    rank, dk = wa2_ref.shape
    dkh = dk // GLA_HEADS
    h = _rmsnorm(x, g_ref[...]).astype(BF16)
    a_low = _dot(h, win_ref[:, 6 * dk:6 * dk + rank])
    xg = _dot(a_low.astype(BF16), wa2_ref[...]) + ba_ref[...]
    la = -(jnp.maximum(-xg, 0.0) + jnp.log(1.0 + jnp.exp(-jnp.abs(xg)))) * (1.0 / GLA_GATE_TEMP)
    hi, mid, lo = _split3(la)
    tri = tri_ref[...]
    b = _dot(tri, hi) + _dot(tri, mid) + _dot(tri, lo)
    proj = _dot(h, win_ref[:, 0:6 * dk])
    q = proj[:, 0:dk] * (dkh ** -0.5)
    k = proj[:, dk:2 * dk]
    v = proj[:, 2 * dk:4 * dk]
    r = proj[:, 4 * dk:6 * dk]
    return q, k, v, r, b


def _gla_level_operand_steps(q16, k16, b, levels, ops):
    rows, dk = b.shape
    row = lax.broadcasted_iota(jnp.int32, (rows, 1), 0)
    sub = lax.broadcasted_iota(jnp.int32, (1, V7X_SUBLANES, 1), 1)
    b = b * LOG2_E
    for l in range(1, levels + 1):
        s = 2 ** l
        if s >= V7X_SUBLANES:
            b3 = b.reshape(rows // s, s, dk)
            m = b3[:, s // 2 - 1:s // 2, :]
        else:
            b3 = b.reshape(rows // V7X_SUBLANES, V7X_SUBLANES, dk)
            picks = [b3[:, j + s // 2 - 1:j + s // 2, :] for j in range(0, V7X_SUBLANES, s)]
            m = picks[-1]
            for idx in range(len(picks) - 2, -1, -1):
                m = jnp.where(sub < (idx + 1) * s, picks[idx], m)
        e = jnp.exp2(-jnp.abs(b3 - m)).reshape(rows, dk).astype(BF16)
        second = ((row >> (l - 1)) & 1) == 1
        ops.append(jnp.where(second, q16, k16) * e)
        yield GLA_LEVEL_COST


def _gla_level_operands(q16, k16, b, levels):
    ops = []
    for _ in _gla_level_operand_steps(q16, k16, b, levels, ops):
        pass
    return ops


def _gla_intra(qb16_h, kb16_h, ops, masks_ref, rs, hs):
    a = _dot_nt(qb16_h, kb16_h) * masks_ref[0]
    for l, t in enumerate(ops, start=1):
        th = t[rs, hs]
        a = a + _dot_nt(th, th) * masks_ref[l]
    return a


def _lane_column(row_vec):
    return jnp.transpose(row_vec, (1, 0))


def _gla_out(x, o, r, gn_ref, wo_ref):
    dv = o.shape[-1]
    dvh = dv // GLA_HEADS
    normed = []
    for hd in range(GLA_HEADS):
        oh = o[:, hd * dvh:(hd + 1) * dvh]
        normed.append(oh * lax.rsqrt(jnp.mean(oh * oh, axis=-1, keepdims=True) + EPS))
    o = jnp.concatenate(normed, axis=-1) * gn_ref[...]
    o = (o * _silu(r)).astype(BF16)
    return x + _dot(o, wo_ref[...])


def _gla_prompt_steps(x, states, g_ref, win_ref, wa2_ref, ba_ref, gn_ref, wo_ref,
                      tri_ref, masks_ref, result, *, levels, chunk):
    rows = x.shape[0]
    rank, dk = wa2_ref.shape
    dkh = dk // GLA_HEADS
    h = _rmsnorm(x, g_ref[...]).astype(BF16)
    a_low = _dot(h, win_ref[:, 6 * dk:6 * dk + rank])
    yield 150
    q = _dot(h, win_ref[:, 0:dk]) * (dkh ** -0.5)
    q16 = q.astype(BF16)
    yield 512
    xg = _dot(a_low.astype(BF16), wa2_ref[...]) + ba_ref[...]
    yield 150
    k = _dot(h, win_ref[:, dk:2 * dk])
    k16 = k.astype(BF16)
    yield 512
    la = -(jnp.maximum(-xg, 0.0) + jnp.log(1.0 + jnp.exp(-jnp.abs(xg)))) * (1.0 / GLA_GATE_TEMP)
    hi, mid, lo = _split3(la)
    yield 400
    vb = _dot(h, win_ref[:, 2 * dk:4 * dk]).astype(BF16)
    yield 1024
    dvh = vb.shape[-1] // GLA_HEADS
    tri = tri_ref[...]
    b = _dot(tri, hi) + _dot(tri, mid) + _dot(tri, lo)
    yield 350
    r = _dot(h, win_ref[:, 4 * dk:6 * dk])
    yield 1024
    ops = []
    yield from _gla_level_operand_steps(q16, k16, b, levels, ops)
    qb = (q * jnp.exp(b)).astype(BF16)
    yield 60

    states = list(states)
    pieces = [(ci, hd) for ci in range(rows // chunk) for hd in range(GLA_HEADS)]
    kends = {}
    outs = {}

    def scores(ci, hd):
        rs = slice(ci * chunk, (ci + 1) * chunk)
        hs = slice(hd * dkh, (hd + 1) * dkh)
        return _gla_intra(q16[rs, hs], k16[rs, hs], ops, masks_ref, rs, hs).astype(BF16)

    def finish(ci, hd, a16):
        rs = slice(ci * chunk, (ci + 1) * chunk)
        hs = slice(hd * dkh, (hd + 1) * dkh)
        vs = slice(hd * dvh, (hd + 1) * dvh)
        b_c = b[rs]
        b_last = b_c[chunk - 1:chunk, :]
        if ci not in kends:
            kends[ci] = (k[rs] * jnp.exp(b_last - b_c)).astype(BF16)
        s_old = states[hd]
        outs[ci, hd] = _dot(a16, vb[rs, vs]) + _dot(qb[rs, hs], s_old.astype(BF16))
        dec = jnp.exp(_lane_column(b_last[:, hs]))
        states[hd] = dec * s_old + _dot_tn(kends[ci][:, hs], vb[rs, vs])

    pending = []
    for piece in pieces:
        pending.append((*piece, scores(*piece)))
        yield GLA_SCORE_COST
        if len(pending) > GLA_SCORE_LEAD:
            finish(*pending.pop(0))
            yield GLA_FINISH_COST
    while pending:
        finish(*pending.pop(0))
        yield GLA_FINISH_COST
    o = jnp.concatenate(
        [jnp.concatenate([outs[ci, hd] for hd in range(GLA_HEADS)], axis=-1) for ci in range(rows // chunk)],
        axis=0)
    result["z"] = _gla_out(x, o, r, gn_ref, wo_ref)
    result["states"] = states
    yield 1700


def _ffn_steps(x, g_ref, w1_ref, w3_ref, w2_ref, gf_ref, result, *, final, col_chunks):
    rows = x.shape[0]
    h = _rmsnorm(x, g_ref[...]).astype(BF16)
    yield 200
    acc = x
    lo = 0
    pending = None
    for cols in col_chunks:
        cs = slice(lo, lo + cols)
        lo += cols
        u1 = _dot(h, w1_ref[:, cs])
        yield rows * cols // 256
        u3 = _dot(h, w3_ref[:, cs])
        yield rows * cols // 256
        if pending is not None:
            acc = acc + _dot(pending[0], w2_ref[pending[1], :])
            yield rows * (pending[1].stop - pending[1].start) // 256
        pending = ((_silu(u1) * u3).astype(BF16), cs)
    acc = acc + _dot(pending[0], w2_ref[pending[1], :])
    yield rows * (pending[1].stop - pending[1].start) // 256
    if final:
        acc = _rmsnorm(acc, gf_ref[...])
    result["out"] = acc


def _ffn_weight_chunks(layer, wf_refs, ws_refs, stage_refs, sem_refs):
    chunks = []
    used = {}
    for wf, ws, stage, sem in zip(wf_refs, ws_refs, stage_refs, sem_refs):
        n_rows = ws.shape[0] // FFN_WEIGHT_CHUNKS
        for i in range(FFN_WEIGHT_CHUNKS):
            rs = pl.ds(i * n_rows, n_rows)
            slot = used.get(id(stage), 0) % FFN_STAGE_SLOTS
            used[id(stage)] = used.get(id(stage), 0) + 1
            chunks.append((wf.at[layer, rs, :], stage.at[slot], sem.at[slot], ws.at[rs, :]))
    return chunks


def _fetch_and_cast(chunks):
    ahead = FFN_STAGE_SLOTS - 1
    copies = [pltpu.make_async_copy(src, stage, sem) for src, stage, sem, _ in chunks]
    for i in range(min(ahead, len(chunks))):
        copies[i].start()
    for i, (_, stage, _, dst) in enumerate(chunks):
        if i + ahead < len(chunks):
            copies[i + ahead].start()
        copies[i].wait()
        dst[...] = stage[...].astype(BF16)


def _interleave(*pairs):
    gens = [p[0] for p in pairs]
    totals = [float(p[1]) for p in pairs]
    done = [0.0] * len(gens)
    live = list(range(len(gens)))
    while live:
        i = min(live, key=lambda j: done[j] / totals[j])
        try:
            done[i] += next(gens[i])
        except StopIteration:
            live.remove(i)


def _gla_sample_body(x_ref, s0_ref, g_ref, win_ref, wa2_ref, ba_ref, gn_ref, wo_ref,
                     tri_ref, masks_ref, o_ref, sout_ref, *, levels, seqs, steps):
    x = x_ref[...]
    rows = x.shape[0]
    q, k, v, r, b = _gla_front(x, g_ref, win_ref, wa2_ref, ba_ref, tri_ref)
    dk = q.shape[-1]
    dkh = dk // GLA_HEADS
    dvh = v.shape[-1] // GLA_HEADS
    q16 = q.astype(BF16)
    k16 = k.astype(BF16)
    ops = _gla_level_operands(q16, k16, b, levels)
    b3 = b.reshape(seqs, steps, dk)
    b_last = b3[:, steps - 1:steps, :]
    qb = (q * jnp.exp(b)).astype(BF16)
    kend = (k.reshape(seqs, steps, dk) * jnp.exp(b_last - b3)).reshape(rows, dk).astype(BF16)
    vb = v.astype(BF16)
    all_rows = slice(0, rows)
    outs = []
    for hd in range(GLA_HEADS):
        hs = slice(hd * dkh, (hd + 1) * dkh)
        vs = slice(hd * dvh, (hd + 1) * dvh)
        a = _gla_intra(q16[:, hs], k16[:, hs], ops, masks_ref, all_rows, hs)
        o_h = _dot(a.astype(BF16), vb[:, vs])
        o_state = []
        for s in range(seqs):
            rs = slice(s * steps, (s + 1) * steps)
            s_old = s0_ref[s, hd]
            o_state.append(_dot(qb[rs, hs], s_old.astype(BF16)))
            dec = jnp.exp(_lane_column(b[(s + 1) * steps - 1:(s + 1) * steps, hs]))
            sout_ref[s, hd] = dec * s_old + _dot_tn(kend[rs, hs], vb[rs, vs])
        outs.append(o_h + jnp.concatenate(o_state, axis=0))
    o_ref[...] = _gla_out(x, jnp.concatenate(outs, axis=-1), r, gn_ref, wo_ref)


def _gla_sample(x, state_gla, layer, gla_consts, prev, *, steps):
    n, d = x.shape
    seqs = GLA_SEQS if prev is None else 2 * GLA_SEQS
    rows = seqs * steps
    bs = n // steps
    assert bs % seqs == 0
    win, wa2, wo = gla_consts[1], gla_consts[2], gla_consts[5]
    dk = wa2.shape[1]
    dkh, dvh = dk // GLA_HEADS, wo.shape[0] // GLA_HEADS
    tri, masks, levels = _gla_tables(rows, steps, rows)
    consts = tuple(gla_consts) + (_Const(tri), _Const(masks))
    state_block = seqs * GLA_HEADS * dkh * dvh * 4
    resident = _nbytes(consts) + 4 * rows * d * 4 + 4 * state_block
    resident += rows * (win.shape[1] * 4 + (levels + 4) * dk * 4)
    return _call_with_state_slot(
        functools.partial(_gla_sample_body, levels=levels, seqs=seqs, steps=steps),
        (x, state_gla) + _arrays(consts),
        [pl.BlockSpec((rows, d), lambda i: (i, 0)),
         pl.BlockSpec((None, seqs, GLA_HEADS, dkh, dvh), lambda i: (layer, i, 0, 0, 0))]
        + _specs(consts),
        grid=(bs // seqs,),
        out_shapes=[jax.ShapeDtypeStruct((n, d), F32)],
        out_specs=[pl.BlockSpec((rows, d), lambda i: (i, 0))],
        state_shape=(bs, GLA_HEADS, dkh, dvh), state_block=(seqs, GLA_HEADS, dkh, dvh),
        state_index=lambda i: (i, 0, 0, 0),
        slot=layer, n_slots=state_gla.shape[0], prev=prev,
        scratch_shapes=[], resident=resident, name="gla_sample")


def _ffn_tail_steps(s, n_total, z_ref, xs_ref, o_ref, os_ref, ffn_steps):
    @pl.when(s >= n_total)
    def _():
        ffn = {}
        _interleave(ffn_steps(ffn, jnp.where(s == n_total, z_ref[...], xs_ref[...])))

        @pl.when(s == n_total)
        def _():
            o_ref[...] = ffn["out"]

        @pl.when(s > n_total)
        def _():
            os_ref[...] = ffn["out"]


def _pool_ffn_body(x_ref, xs_ref, gm_ref, wp_ref, sc_ref, gfn_ref, gf_ref, w1f_ref, w3f_ref, w2f_ref,
                   o_ref, os_ref, ctx_ref, ext_ref, z_ref, w1_ref, w3_ref, w2_ref, st13_ref, st2_ref, sem13, sem2,
                   *, rows, n_tiles, n_total, final, ffn_layer):
    s = pl.program_id(0)
    d = x_ref.shape[-1]
    t = jnp.minimum(s, n_total - 1) % n_tiles

    @pl.when(s == 0)
    def _():
        _fetch_and_cast(_ffn_weight_chunks(ffn_layer, (w1f_ref, w3f_ref, w2f_ref), (w1_ref, w3_ref, w2_ref),
                                           (st13_ref, st13_ref, st2_ref), (sem13, sem13, sem2)))

    def ffn_steps(ffn, src=None):
        return (_ffn_steps(z_ref[...] if src is None else src, gfn_ref, w1_ref, w3_ref, w2_ref, gf_ref, ffn,
                           final=final, col_chunks=FFN_COL_CHUNKS), FFN_STEP_COST * rows // 256)

    def mixer_step(with_ffn):
        @pl.when(t == 0)
        def _():
            ext_ref[0:POOL_MAXW, :] = jnp.zeros((POOL_MAXW, d), F32)

        x = x_ref[...]
        h = _rmsnorm(x, gm_ref[...])
        ext_ref[POOL_MAXW:POOL_MAXW + rows, :] = h
        pos = t * rows + lax.broadcasted_iota(jnp.int32, (rows, 1), 0)

        def ext_read(k, cs):
            return ext_ref[POOL_MAXW - k:POOL_MAXW - k + rows, cs]

        mix, ffn = {}, {}
        streams = [(_pool_group_steps(ext_read, h, pos, wp_ref, sc_ref[...], mix), POOL_STEP_COST * rows // 256)]
        if with_ffn:
            streams.append(ffn_steps(ffn))
        _interleave(*streams)
        z_ref[...] = x + mix["y"]
        if with_ffn:
            o_ref[...] = ffn["out"]
        ext_ref[0:POOL_MAXW, :] = ext_ref[rows:rows + POOL_MAXW, :]

        @pl.when(t == n_tiles - 1)
        def _():
            ctx_ref[...] = ext_ref[rows + 1:rows + POOL_MAXW, :]

    @pl.when(s == 0)
    def _():
        mixer_step(False)

    @pl.when(jnp.logical_and(s > 0, s < n_total))
    def _():
        mixer_step(True)

    _ffn_tail_steps(s, n_total, z_ref, xs_ref, o_ref, os_ref, ffn_steps)


def _gla_ffn_body(x_ref, xs_ref, gm_ref, win_ref, wa2_ref, ba_ref, gn_ref, wo_ref, tri_ref, masks_ref,
                  gfn_ref, gf_ref, w1f_ref, w3f_ref, w2f_ref,
                  o_ref, os_ref, sout_ref, s_ref, z_ref, w1_ref, w3_ref, w2_ref, st13_ref, st2_ref, sem13, sem2,
                  *, levels, chunk, n_tiles, n_total, final, ffn_layer):
    s = pl.program_id(0)
    t = jnp.minimum(s, n_total - 1) % n_tiles

    @pl.when(s == 0)
    def _():
        _fetch_and_cast(_ffn_weight_chunks(ffn_layer, (w1f_ref, w3f_ref, w2f_ref), (w1_ref, w3_ref, w2_ref),
                                           (st13_ref, st13_ref, st2_ref), (sem13, sem13, sem2)))

    def ffn_steps(ffn, src=None):
        return (_ffn_steps(z_ref[...] if src is None else src, gfn_ref, w1_ref, w3_ref, w2_ref, gf_ref, ffn,
                           final=final, col_chunks=FFN_COL_CHUNKS), FFN_STEP_COST)

    def mixer_step(with_ffn):
        @pl.when(t == 0)
        def _():
            s_ref[...] = jnp.zeros(s_ref.shape, F32)

        states = [s_ref[hd] for hd in range(GLA_HEADS)]
        mix, ffn = {}, {}
        streams = [(_gla_prompt_steps(x_ref[...], states, gm_ref, win_ref, wa2_ref, ba_ref, gn_ref,
                                      wo_ref, tri_ref, masks_ref, mix, levels=levels, chunk=chunk), GLA_STEP_COST)]
        if with_ffn:
            streams.append(ffn_steps(ffn))
        _interleave(*streams)
        states = mix["states"]
        z_ref[...] = mix["z"]
        if with_ffn:
            o_ref[...] = ffn["out"]
        for hd in range(GLA_HEADS):
            s_ref[hd] = states[hd]

        @pl.when(t == n_tiles - 1)
        def _():
            for hd in range(GLA_HEADS):
                sout_ref[hd] = states[hd]

    @pl.when(s == 0)
    def _():
        mixer_step(False)

    @pl.when(jnp.logical_and(s > 0, s < n_total))
    def _():
        mixer_step(True)

    _ffn_tail_steps(s, n_total, z_ref, xs_ref, o_ref, os_ref, ffn_steps)


def _prompt_layer(kind, x, xs, mixer_consts, ffn_consts, *, batch, seq, final, slot, n_slots, prev):
    n, d = x.shape
    rows = PROMPT_ROWS[kind]
    n_tiles = seq // rows
    n_total = batch * n_tiles
    assert seq % rows == 0 and rows >= POOL_MAXW and rows % GLA_CHUNK == 0
    g_ffn, w1f, w3f, w2f, gf = ffn_consts
    dff = w1f.shape[1]
    ffn_vmem = (g_ffn, gf)
    stage13 = (FFN_STAGE_SLOTS, d // FFN_WEIGHT_CHUNKS, dff)
    stage2 = (FFN_STAGE_SLOTS, dff // FFN_WEIGHT_CHUNKS, d)
    assert d % FFN_WEIGHT_CHUNKS == 0 and (dff // FFN_WEIGHT_CHUNKS) % (2 * V7X_SUBLANES) == 0
    ffn_scratch = [pltpu.VMEM((d, dff), BF16), pltpu.VMEM((d, dff), BF16), pltpu.VMEM((dff, d), BF16),
                   pltpu.VMEM(stage13, F32), pltpu.VMEM(stage2, F32),
                   pltpu.SemaphoreType.DMA((FFN_STAGE_SLOTS,)), pltpu.SemaphoreType.DMA((FFN_STAGE_SLOTS,))]
    resident = _nbytes(mixer_consts) + 3 * d * dff * 2 + 4 * int(np.prod(stage13)) * 2
    resident += 6 * rows * d * 4 + 3 * rows * dff * 4
    ns = xs.shape[0]
    assert ns % rows == 0
    n_sample = ns // rows
    x_spec = pl.BlockSpec((rows, d), lambda s: (jnp.minimum(s, n_total - 1), 0))
    o_spec = pl.BlockSpec((rows, d), lambda s: (jnp.clip(s - 1, 0, n_total - 1), 0))
    xs_spec = pl.BlockSpec((rows, d), lambda s: (jnp.clip(s - n_total - 1, 0, n_sample - 1), 0))
    seq_of = lambda s: jnp.minimum(s, n_total - 1) // n_tiles
    carry = [pltpu.VMEM((rows, d), F32)]
    if kind == "pool":
        body = functools.partial(_pool_ffn_body, rows=rows, n_tiles=n_tiles, n_total=n_total, final=final,
                                 ffn_layer=w1f.layer)
        state_shape = (batch, POOL_BUF, d)
        state_block = (None, POOL_BUF, d)
        state_index = lambda s: (seq_of(s), 0, 0)
        scratch = [pltpu.VMEM((rows + POOL_MAXW, d), F32)] + carry
        resident += (rows + POOL_MAXW) * d * 4
        consts = tuple(mixer_consts) + ffn_vmem
    else:
        win, wa2, wo = mixer_consts[1], mixer_consts[2], mixer_consts[5]
        dk = wa2.shape[1]
        dkh, dvh = dk // GLA_HEADS, wo.shape[0] // GLA_HEADS
        tri, masks, levels = _gla_tables(rows, GLA_CHUNK, GLA_CHUNK)
        body = functools.partial(_gla_ffn_body, levels=levels, chunk=GLA_CHUNK, n_tiles=n_tiles,
                                 n_total=n_total, final=final, ffn_layer=w1f.layer)
        state_shape = (batch, GLA_HEADS, dkh, dvh)
        state_block = (None, GLA_HEADS, dkh, dvh)
        state_index = lambda s: (seq_of(s), 0, 0, 0)
        scratch = [pltpu.VMEM((GLA_HEADS, dkh, dvh), F32)] + carry
        tables = (_Const(tri), _Const(masks))
        resident += _nbytes(tables) + 3 * GLA_HEADS * dkh * dvh * 4
        resident += rows * (win.shape[1] * 4 + (levels + 4) * dk * 4)
        consts = tuple(mixer_consts) + tables + ffn_vmem
    hbm = pl.BlockSpec(memory_space=pl.ANY)
    return _call_with_state_slot(
        body, (x, xs) + _arrays(consts) + (w1f.array, w3f.array, w2f.array),
        [x_spec, xs_spec] + _specs(consts) + [hbm] * 3,
        grid=(n_total + 1 + n_sample,),
        out_shapes=[jax.ShapeDtypeStruct((n, d), F32), jax.ShapeDtypeStruct((ns, d), F32)],
        out_specs=[o_spec, xs_spec],
        state_shape=state_shape, state_block=state_block, state_index=state_index,
        slot=slot, n_slots=n_slots, prev=prev,
        scratch_shapes=scratch + ffn_scratch, resident=resident,
        name=kind + ("_ffn_final" if final else "_ffn") + "_prompt")


def kernel(x_prompt, x_sample, state_pool, state_gla, norm_mix, norm_ffn, norm_final, pool_w, pool_scale,
           gla_w_in, gla_w_a2, gla_b_a, gla_norm, gla_w_o, ffn_w1, ffn_w3, ffn_w2):
    batch, seq, d = x_prompt.shape
    bs, steps, _ = x_sample.shape
    depth = norm_mix.shape[0]

    xp = x_prompt.reshape(batch * seq, d)
    xs = x_sample.reshape(bs * steps, d)

    rows_of = lambda a: a.reshape(a.shape[0], 1, -1)
    w_pool, w_in, w_a2, w_o = (a.astype(BF16) for a in (pool_w, gla_w_in, gla_w_a2, gla_w_o))
    g_mix, g_ffn, sc_pool, b_a, g_gla = (rows_of(a) for a in (norm_mix, norm_ffn, pool_scale, gla_b_a, gla_norm))
    gf = _Const(norm_final.reshape(1, -1))

    n_pool, n_gla = state_pool.shape[0], state_gla.shape[0]
    state_pool_t = jnp.swapaxes(state_pool, 1, 2)
    new_pool_p = new_pool_s = new_gla_p = new_gla_s = None
    for i in range(depth):
        j = i // 2
        final = i == depth - 1
        g = _Const(g_mix, i)
        ffn_consts = (_Const(g_ffn, i), _Const(ffn_w1, i), _Const(ffn_w3, i), _Const(ffn_w2, i), gf)
        if i % 2 == 0:
            pool_consts = (g, _Const(w_pool, j), _Const(sc_pool, j))
            xs3, new_pool_s = _pool_sample(xs.reshape(bs, steps, d), state_pool_t, j, pool_consts, new_pool_s)
            xp, xs, new_pool_p = _prompt_layer("pool", xp, xs3.reshape(bs * steps, d), pool_consts, ffn_consts,
                                               batch=batch, seq=seq, final=final, slot=j, n_slots=n_pool,
                                               prev=new_pool_p)
        else:
            gla_consts = (g, _Const(w_in, j), _Const(w_a2, j), _Const(b_a, j), _Const(g_gla, j), _Const(w_o, j))
            xs, new_gla_s = _gla_sample(xs, state_gla, j, gla_consts, new_gla_s, steps=steps)
            xp, xs, new_gla_p = _prompt_layer("gla", xp, xs, gla_consts, ffn_consts, batch=batch, seq=seq,
                                              final=final, slot=j, n_slots=n_gla, prev=new_gla_p)

    return (xp.reshape(batch, seq, d), xs.reshape(bs, steps, d),
            new_pool_p, new_gla_p, jnp.swapaxes(new_pool_s, 1, 2), new_gla_s)
```

```python
import functools

import numpy as np
import jax
import jax.numpy as jnp
from jax import lax
from jax.experimental import pallas as pl
from jax.experimental.pallas import tpu as pltpu

F32 = jnp.float32
BF16 = jnp.bfloat16

PAST_LEN = 16384
POOL_WINDOWS = (2, 4, 8, 16)
POOL_MAXW = max(POOL_WINDOWS)
POOL_BUF = POOL_MAXW - 1
GLA_HEADS = 4
GLA_GATE_TEMP = 16.0
EPS = 1e-6
LOG2_E = 1.4426950408889634

V7X_SUBLANES = 8
V7X_VMEM_BYTES = 64 * 1024 * 1024

PROMPT_ROWS = {"pool": 256, "gla": 256}
GLA_CHUNK = 128
FFN_WEIGHT_CHUNKS = 8
FFN_STAGE_SLOTS = 4
POOL_SEQS = 32
GLA_SEQS = 8
FFN_COL_CHUNKS = (1024, 1024, 768)
GLA_LEVEL_COST = 300
GLA_SCORE_COST = 300
GLA_FINISH_COST = 200
GLA_SCORE_LEAD = 2
FFN_STEP_COST = 8648
GLA_STEP_COST = 5882 + 7 * GLA_LEVEL_COST + 8 * (GLA_SCORE_COST + GLA_FINISH_COST)
POOL_STEP_COST = 2220


def _vmem_limit(resident_bytes):
    return int(min(V7X_VMEM_BYTES - (4 << 20), 2 * resident_bytes + (16 << 20)))


class _Const:
    def __init__(self, array, layer=None):
        self.array = array
        self.layer = layer
        self.shape = tuple(array.shape if layer is None else array.shape[1:])
        self.nbytes = int(np.prod(self.shape)) * array.dtype.itemsize

    def spec(self):
        nd = len(self.shape)
        if self.layer is None:
            return pl.BlockSpec(self.shape, lambda *_: (0,) * nd, pipeline_mode=pl.Buffered(1))
        layer = self.layer
        return pl.BlockSpec((None,) + self.shape, lambda *_: (layer,) + (0,) * nd, pipeline_mode=pl.Buffered(1))


def _nbytes(consts):
    return sum(c.nbytes for c in consts)


def _arrays(consts):
    return tuple(c.array for c in consts)


def _specs(consts):
    return [c.spec() for c in consts]


def _call_with_state_slot(body, args, in_specs, *, grid, out_shapes, out_specs, state_shape, state_block,
                          state_index, slot, n_slots, prev, scratch_shapes, resident, name):
    n_args = len(args)
    n_outs = len(out_shapes)
    inner = body
    aliases = {}
    if prev is None:
        state_spec = pl.BlockSpec((n_slots,) + tuple(state_block), lambda *g: (0,) + tuple(state_index(*g)))
        resident += 2 * (n_slots - 1) * int(np.prod([b for b in state_block if b is not None])) * 4

        def body(*refs):
            refs = list(refs)
            full = refs[n_args + n_outs]
            for other in range(n_slots):
                if other != slot:
                    full[other] = jnp.zeros(full.shape[1:], F32)
            refs[n_args + n_outs] = full.at[slot]
            return inner(*refs)
    else:
        state_spec = pl.BlockSpec((None,) + tuple(state_block), lambda *g: (slot,) + tuple(state_index(*g)))

        def body(*refs):
            return inner(*refs[:n_args], *refs[n_args + 1:])

        args = tuple(args) + (prev,)
        in_specs = list(in_specs) + [pl.BlockSpec(memory_space=pl.ANY)]
        aliases = {n_args: n_outs}
    return pl.pallas_call(
        body,
        out_shape=tuple(out_shapes) + (jax.ShapeDtypeStruct((n_slots,) + tuple(state_shape), F32),),
        grid=grid,
        in_specs=in_specs,
        out_specs=tuple(out_specs) + (state_spec,),
        scratch_shapes=scratch_shapes,
        input_output_aliases=aliases,
        compiler_params=pltpu.CompilerParams(
            dimension_semantics=("arbitrary",) * len(grid), vmem_limit_bytes=_vmem_limit(resident)),
        name=name,
    )(*args)


def _rmsnorm(x, g):
    ms = jnp.mean(x * x, axis=-1, keepdims=True)
    return x * lax.rsqrt(ms + EPS) * g


def _silu(x):
    hx = 0.5 * x
    return hx + hx * jnp.tanh(hx)


def _dot(a, b):
    return jnp.dot(a, b, preferred_element_type=F32)


def _dot_nt(a, b):
    return lax.dot_general(a, b, (((1,), (1,)), ((), ())), preferred_element_type=F32)


def _dot_tn(a, b):
    return lax.dot_general(a, b, (((0,), (0,)), ((), ())), preferred_element_type=F32)


def _split3(x):
    hi = x.astype(BF16)
    r1 = x - hi.astype(F32)
    mid = r1.astype(BF16)
    lo = (r1 - mid.astype(F32)).astype(BF16)
    return hi, mid, lo


def _pool_group_steps(ext_read, h, pos, w_ref, sc, result):
    gw = h.shape[-1] // len(POOL_WINDOWS)
    ys = []
    for g, w in enumerate(POOL_WINDOWS):
        cs = slice(g * gw, (g + 1) * gw)
        hg = h[:, cs]
        s = hg
        for k in range(1, w):
            s = s + ext_read(k, cs)
        cnt = jnp.minimum(pos + 1, w).astype(F32)
        diff = (s / cnt - hg).astype(BF16)
        ys.append(_dot(diff, w_ref[g]))
        yield (100 + 70 * (w - 1)) * h.shape[0] // 256
    result["y"] = jnp.concatenate(ys, axis=-1) * sc


def _pool_groups(ext_read, h, pos, w_ref, sc):
    result = {}
    for _ in _pool_group_steps(ext_read, h, pos, w_ref, sc, result):
        pass
    return result["y"]


def _pool_sample_body(x_ref, ctx_ref, g_ref, w_ref, sc_ref, o_ref, nctx_ref, ext_ref, *, seqs, steps):
    d = x_ref.shape[-1]
    ext_ref[0] = jnp.zeros((seqs, d), F32)
    ext_ref[1:POOL_MAXW] = ctx_ref[...]
    xs = [x_ref[:, t, :] for t in range(steps)]
    for t in range(steps):
        ext_ref[POOL_MAXW + t] = _rmsnorm(xs[t], g_ref[...])
    h = ext_ref[POOL_MAXW:POOL_MAXW + steps].reshape(steps * seqs, d)
    pos = PAST_LEN + lax.broadcasted_iota(jnp.int32, (steps * seqs, 1), 0) // seqs

    def ext_read(k, cs):
        return ext_ref[POOL_MAXW - k:POOL_MAXW - k + steps, :, cs].reshape(steps * seqs, cs.stop - cs.start)

    y = _pool_groups(ext_read, h, pos, w_ref, sc_ref[...])
    for t in range(steps):
        o_ref[:, t, :] = xs[t] + y[t * seqs:(t + 1) * seqs]
    nctx_ref[...] = ext_ref[steps + 1:steps + POOL_MAXW]


def _pool_sample(x, state_pool_t, layer, pool_consts, prev):
    bs, steps, d = x.shape
    seqs = POOL_SEQS
    assert bs % seqs == 0 and steps == V7X_SUBLANES
    resident = 2 * seqs * (2 * steps + 2 * POOL_MAXW) * d * 4 + seqs * (steps + POOL_MAXW) * d * 4
    return _call_with_state_slot(
        functools.partial(_pool_sample_body, seqs=seqs, steps=steps),
        (x, state_pool_t) + _arrays(pool_consts),
        [pl.BlockSpec((seqs, steps, d), lambda i: (i, 0, 0)),
         pl.BlockSpec((None, POOL_BUF, seqs, d), lambda i: (layer, 0, i, 0))]
        + _specs(pool_consts),
        grid=(bs // seqs,),
        out_shapes=[jax.ShapeDtypeStruct((bs, steps, d), F32)],
        out_specs=[pl.BlockSpec((seqs, steps, d), lambda i: (i, 0, 0))],
        state_shape=(POOL_BUF, bs, d), state_block=(POOL_BUF, seqs, d), state_index=lambda i: (0, i, 0),
        slot=layer, n_slots=state_pool_t.shape[0], prev=prev,
        scratch_shapes=[pltpu.VMEM((steps + POOL_MAXW, seqs, d), F32)],
        resident=resident, name="pool_sample")


def _gla_tables(rows, seg, score_rows):
    levels = int(np.log2(seg))
    assert 2 ** levels == seg and rows % score_rows == 0 and score_rows % seg == 0
    idx = np.arange(rows)
    tri = (idx[None, :] >= (idx[:, None] // seg) * seg) & (idx[None, :] <= idx[:, None])
    row = np.arange(score_rows)[:, None]
    col = np.arange(score_rows)[None, :]
    masks = [row == col]
    for l in range(1, levels + 1):
        s = 2 ** l
        mid = (row // s) * s + s // 2 - 1
        masks.append((row // s == col // s) & (row > mid) & (col <= mid))
    masks = np.stack(masks, axis=0).astype(np.float32)
    return jnp.asarray(tri.astype(np.float32), BF16), jnp.asarray(masks, BF16), levels


def _gla_front(x, g_ref, win_ref, wa2_ref, ba_ref, tri_ref):
    rank, dk = wa2_ref.shape
    dkh = dk // GLA_HEADS
    h = _rmsnorm(x, g_ref[...]).astype(BF16)
    a_low = _dot(h, win_ref[:, 6 * dk:6 * dk + rank])
    xg = _dot(a_low.astype(BF16), wa2_ref[...]) + ba_ref[...]
    la = -(jnp.maximum(-xg, 0.0) + jnp.log(1.0 + jnp.exp(-jnp.abs(xg)))) * (1.0 / GLA_GATE_TEMP)
    hi, mid, lo = _split3(la)
    tri = tri_ref[...]
    b = _dot(tri, hi) + _dot(tri, mid) + _dot(tri, lo)
    proj = _dot(h, win_ref[:, 0:6 * dk])
    q = proj[:, 0:dk] * (dkh ** -0.5)
    k = proj[:, dk:2 * dk]
    v = proj[:, 2 * dk:4 * dk]
    r = proj[:, 4 * dk:6 * dk]
    return q, k, v, r, b


def _gla_level_operand_steps(q16, k16, b, levels, ops):
    rows, dk = b.shape
    row = lax.broadcasted_iota(jnp.int32, (rows, 1), 0)
    sub = lax.broadcasted_iota(jnp.int32, (1, V7X_SUBLANES, 1), 1)
    b = b * LOG2_E
    for l in range(1, levels + 1):
        s = 2 ** l
        if s >= 2 * V7X_SUBLANES:
            b4 = b.reshape(rows // s, 2, s // 2, dk)
            m = b4[:, 0:1, s // 2 - 1:s // 2, :]
            x = jnp.concatenate([m - b4[:, 0:1], b4[:, 1:2] - m], axis=1)
        else:
            b3 = b.reshape(rows // V7X_SUBLANES, V7X_SUBLANES, dk)
            picks = [b3[:, j + s // 2 - 1:j + s // 2, :] for j in range(0, V7X_SUBLANES, s)]
            m = picks[-1]
            for idx in range(len(picks) - 2, -1, -1):
                m = jnp.where(sub < (idx + 1) * s, picks[idx], m)
            x = -jnp.abs(b3 - m)
        e = jnp.exp2(x).reshape(rows, dk).astype(BF16)
        second = ((row >> (l - 1)) & 1) == 1
        ops.append(jnp.where(second, q16, k16) * e)
        yield GLA_LEVEL_COST


def _gla_level_operands(q16, k16, b, levels):
    ops = []
    for _ in _gla_level_operand_steps(q16, k16, b, levels, ops):
        pass
    return ops


def _gla_intra(qb16_h, kb16_h, ops, masks_ref, rs, hs):
    a = _dot_nt(qb16_h, kb16_h).astype(BF16) * masks_ref[0]
    for l, t in enumerate(ops, start=1):
        th = t[rs, hs]
        a = a + _dot_nt(th, th).astype(BF16) * masks_ref[l]
    return a


def _lane_column(row_vec):
    return jnp.transpose(row_vec, (1, 0))


def _gla_out(x, o, r, gn_ref, wo_ref):
    dv = o.shape[-1]
    dvh = dv // GLA_HEADS
    normed = []
    for hd in range(GLA_HEADS):
        oh = o[:, hd * dvh:(hd + 1) * dvh]
        normed.append(oh * lax.rsqrt(jnp.mean(oh * oh, axis=-1, keepdims=True) + EPS))
    o = jnp.concatenate(normed, axis=-1) * gn_ref[...]
    o = (o * _silu(r)).astype(BF16)
    return x + _dot(o, wo_ref[...])


def _gla_prompt_steps(x, states, g_ref, win_ref, wa2_ref, ba_ref, gn_ref, wo_ref,
                      tri_ref, masks_ref, result, *, levels, chunk):
    rows = x.shape[0]
    rank, dk = wa2_ref.shape
    dkh = dk // GLA_HEADS
    h = _rmsnorm(x, g_ref[...]).astype(BF16)
    a_low = _dot(h, win_ref[:, 6 * dk:6 * dk + rank])
    yield 150
    q = _dot(h, win_ref[:, 0:dk]) * (dkh ** -0.5)
    q16 = q.astype(BF16)
    yield 512
    xg = _dot(a_low.astype(BF16), wa2_ref[...]) + ba_ref[...]
    yield 150
    k = _dot(h, win_ref[:, dk:2 * dk])
    k16 = k.astype(BF16)
    yield 512
    la = -(jnp.maximum(-xg, 0.0) + jnp.log(1.0 + jnp.exp(-jnp.abs(xg)))) * (1.0 / GLA_GATE_TEMP)
    hi, mid, lo = _split3(la)
    yield 400
    vb = _dot(h, win_ref[:, 2 * dk:4 * dk]).astype(BF16)
    yield 1024
    dvh = vb.shape[-1] // GLA_HEADS
    tri = tri_ref[...]
    b = _dot(tri, hi) + _dot(tri, mid) + _dot(tri, lo)
    yield 350
    r = _dot(h, win_ref[:, 4 * dk:6 * dk])
    yield 1024
    ops = []
    yield from _gla_level_operand_steps(q16, k16, b, levels, ops)
    qb = (q * jnp.exp(b)).astype(BF16)
    yield 60

    states = list(states)
    pieces = [(ci, hd) for ci in range(rows // chunk) for hd in range(GLA_HEADS)]
    kends = {}
    outs = {}

    def scores(ci, hd):
        rs = slice(ci * chunk, (ci + 1) * chunk)
        hs = slice(hd * dkh, (hd + 1) * dkh)
        return _gla_intra(q16[rs, hs], k16[rs, hs], ops, masks_ref, rs, hs)

    def finish(ci, hd, a16):
        rs = slice(ci * chunk, (ci + 1) * chunk)
        hs = slice(hd * dkh, (hd + 1) * dkh)
        vs = slice(hd * dvh, (hd + 1) * dvh)
        b_c = b[rs]
        b_last = b_c[chunk - 1:chunk, :]
        if ci not in kends:
            kends[ci] = (k[rs] * jnp.exp(b_last - b_c)).astype(BF16)
        s_old = states[hd]
        outs[ci, hd] = _dot(a16, vb[rs, vs]) + _dot(qb[rs, hs], s_old.astype(BF16))
        dec = jnp.exp(_lane_column(b_last[:, hs]))
        states[hd] = dec * s_old + _dot_tn(kends[ci][:, hs], vb[rs, vs])

    pending = []
    for piece in pieces:
        pending.append((*piece, scores(*piece)))
        yield GLA_SCORE_COST
        if len(pending) > GLA_SCORE_LEAD:
            finish(*pending.pop(0))
            yield GLA_FINISH_COST
    while pending:
        finish(*pending.pop(0))
        yield GLA_FINISH_COST
    o = jnp.concatenate(
        [jnp.concatenate([outs[ci, hd] for hd in range(GLA_HEADS)], axis=-1) for ci in range(rows // chunk)],
        axis=0)
    result["z"] = _gla_out(x, o, r, gn_ref, wo_ref)
    result["states"] = states
    yield 1700


def _ffn_steps(x, g_ref, w1_ref, w3_ref, w2_ref, gf_ref, result, *, final, col_chunks):
    rows = x.shape[0]
    h = _rmsnorm(x, g_ref[...]).astype(BF16)
    yield 200
    acc = x
    lo = 0
    pending = None
    for cols in col_chunks:
        cs = slice(lo, lo + cols)
        lo += cols
        u1 = _dot(h, w1_ref[:, cs])
        yield rows * cols // 256
        u3 = _dot(h, w3_ref[:, cs])
        yield rows * cols // 256
        if pending is not None:
            acc = acc + _dot(pending[0], w2_ref[pending[1], :])
            yield rows * (pending[1].stop - pending[1].start) // 256
        pending = ((_silu(u1) * u3).astype(BF16), cs)
    acc = acc + _dot(pending[0], w2_ref[pending[1], :])
    yield rows * (pending[1].stop - pending[1].start) // 256
    if final:
        acc = _rmsnorm(acc, gf_ref[...])
    result["out"] = acc


def _ffn_weight_chunks(layer, wf_refs, ws_refs, stage_refs, sem_refs):
    chunks = []
    used = {}
    for wf, ws, stage, sem in zip(wf_refs, ws_refs, stage_refs, sem_refs):
        n_rows = ws.shape[0] // FFN_WEIGHT_CHUNKS
        for i in range(FFN_WEIGHT_CHUNKS):
            rs = pl.ds(i * n_rows, n_rows)
            slot = used.get(id(stage), 0) % FFN_STAGE_SLOTS
            used[id(stage)] = used.get(id(stage), 0) + 1
            chunks.append((wf.at[layer, rs, :], stage.at[slot], sem.at[slot], ws.at[rs, :]))
    return chunks


def _fetch_and_cast(chunks):
    ahead = FFN_STAGE_SLOTS - 1
    copies = [pltpu.make_async_copy(src, stage, sem) for src, stage, sem, _ in chunks]
    for i in range(min(ahead, len(chunks))):
        copies[i].start()
    for i, (_, stage, _, dst) in enumerate(chunks):
        if i + ahead < len(chunks):
            copies[i + ahead].start()
        copies[i].wait()
        dst[...] = stage[...].astype(BF16)


def _interleave(*pairs):
    gens = [p[0] for p in pairs]
    totals = [float(p[1]) for p in pairs]
    done = [0.0] * len(gens)
    live = list(range(len(gens)))
    while live:
        i = min(live, key=lambda j: done[j] / totals[j])
        try:
            done[i] += next(gens[i])
        except StopIteration:
            live.remove(i)


def _gla_sample_body(x_ref, s0_ref, g_ref, win_ref, wa2_ref, ba_ref, gn_ref, wo_ref,
                     tri_ref, masks_ref, o_ref, sout_ref, *, levels, seqs, steps):
    x = x_ref[...]
    rows = x.shape[0]
    q, k, v, r, b = _gla_front(x, g_ref, win_ref, wa2_ref, ba_ref, tri_ref)
    dk = q.shape[-1]
    dkh = dk // GLA_HEADS
    dvh = v.shape[-1] // GLA_HEADS
    q16 = q.astype(BF16)
    k16 = k.astype(BF16)
    ops = _gla_level_operands(q16, k16, b, levels)
    b3 = b.reshape(seqs, steps, dk)
    b_last = b3[:, steps - 1:steps, :]
    qb = (q * jnp.exp(b)).astype(BF16)
    kend = (k.reshape(seqs, steps, dk) * jnp.exp(b_last - b3)).reshape(rows, dk).astype(BF16)
    vb = v.astype(BF16)
    all_rows = slice(0, rows)
    outs = []
    for hd in range(GLA_HEADS):
        hs = slice(hd * dkh, (hd + 1) * dkh)
        vs = slice(hd * dvh, (hd + 1) * dvh)
        a = _gla_intra(q16[:, hs], k16[:, hs], ops, masks_ref, all_rows, hs)
        o_h = _dot(a, vb[:, vs])
        o_state = []
        for s in range(seqs):
            rs = slice(s * steps, (s + 1) * steps)
            s_old = s0_ref[s, hd]
            o_state.append(_dot(qb[rs, hs], s_old.astype(BF16)))
            dec = jnp.exp(_lane_column(b[(s + 1) * steps - 1:(s + 1) * steps, hs]))
            sout_ref[s, hd] = dec * s_old + _dot_tn(kend[rs, hs], vb[rs, vs])
        outs.append(o_h + jnp.concatenate(o_state, axis=0))
    o_ref[...] = _gla_out(x, jnp.concatenate(outs, axis=-1), r, gn_ref, wo_ref)


def _gla_sample(x, state_gla, layer, gla_consts, prev, *, steps):
    n, d = x.shape
    seqs = GLA_SEQS if prev is None else 2 * GLA_SEQS
    rows = seqs * steps
    bs = n // steps
    assert bs % seqs == 0
    win, wa2, wo = gla_consts[1], gla_consts[2], gla_consts[5]
    dk = wa2.shape[1]
    dkh, dvh = dk // GLA_HEADS, wo.shape[0] // GLA_HEADS
    tri, masks, levels = _gla_tables(rows, steps, rows)
    consts = tuple(gla_consts) + (_Const(tri), _Const(masks))
    state_block = seqs * GLA_HEADS * dkh * dvh * 4
    resident = _nbytes(consts) + 4 * rows * d * 4 + 4 * state_block
    resident += rows * (win.shape[1] * 4 + (levels + 4) * dk * 4)
    return _call_with_state_slot(
        functools.partial(_gla_sample_body, levels=levels, seqs=seqs, steps=steps),
        (x, state_gla) + _arrays(consts),
        [pl.BlockSpec((rows, d), lambda i: (i, 0)),
         pl.BlockSpec((None, seqs, GLA_HEADS, dkh, dvh), lambda i: (layer, i, 0, 0, 0))]
        + _specs(consts),
        grid=(bs // seqs,),
        out_shapes=[jax.ShapeDtypeStruct((n, d), F32)],
        out_specs=[pl.BlockSpec((rows, d), lambda i: (i, 0))],
        state_shape=(bs, GLA_HEADS, dkh, dvh), state_block=(seqs, GLA_HEADS, dkh, dvh),
        state_index=lambda i: (i, 0, 0, 0),
        slot=layer, n_slots=state_gla.shape[0], prev=prev,
        scratch_shapes=[], resident=resident, name="gla_sample")


def _ffn_tail_steps(s, n_total, z_ref, xs_ref, o_ref, os_ref, ffn_steps):
    @pl.when(s >= n_total)
    def _():
        ffn = {}
        _interleave(ffn_steps(ffn, jnp.where(s == n_total, z_ref[...], xs_ref[...])))

        @pl.when(s == n_total)
        def _():
            o_ref[...] = ffn["out"]

        @pl.when(s > n_total)
        def _():
            os_ref[...] = ffn["out"]


def _pool_ffn_body(x_ref, xs_ref, gm_ref, wp_ref, sc_ref, gfn_ref, gf_ref, w1f_ref, w3f_ref, w2f_ref,
                   o_ref, os_ref, ctx_ref, ext_ref, z_ref, w1_ref, w3_ref, w2_ref, st13_ref, st2_ref, sem13, sem2,
                   *, rows, n_tiles, n_total, final, ffn_layer):
    s = pl.program_id(0)
    d = x_ref.shape[-1]
    t = jnp.minimum(s, n_total - 1) % n_tiles

    @pl.when(s == 0)
    def _():
        _fetch_and_cast(_ffn_weight_chunks(ffn_layer, (w1f_ref, w3f_ref, w2f_ref), (w1_ref, w3_ref, w2_ref),
                                           (st13_ref, st13_ref, st2_ref), (sem13, sem13, sem2)))

    def ffn_steps(ffn, src=None):
        return (_ffn_steps(z_ref[...] if src is None else src, gfn_ref, w1_ref, w3_ref, w2_ref, gf_ref, ffn,
                           final=final, col_chunks=FFN_COL_CHUNKS), FFN_STEP_COST * rows // 256)

    def mixer_step(with_ffn):
        @pl.when(t == 0)
        def _():
            ext_ref[0:POOL_MAXW, :] = jnp.zeros((POOL_MAXW, d), F32)

        x = x_ref[...]
        h = _rmsnorm(x, gm_ref[...])
        ext_ref[POOL_MAXW:POOL_MAXW + rows, :] = h
        pos = t * rows + lax.broadcasted_iota(jnp.int32, (rows, 1), 0)

        def ext_read(k, cs):
            return ext_ref[POOL_MAXW - k:POOL_MAXW - k + rows, cs]

        mix, ffn = {}, {}
        streams = [(_pool_group_steps(ext_read, h, pos, wp_ref, sc_ref[...], mix), POOL_STEP_COST * rows // 256)]
        if with_ffn:
            streams.append(ffn_steps(ffn))
        _interleave(*streams)
        z_ref[...] = x + mix["y"]
        if with_ffn:
            o_ref[...] = ffn["out"]
        ext_ref[0:POOL_MAXW, :] = ext_ref[rows:rows + POOL_MAXW, :]

        @pl.when(t == n_tiles - 1)
        def _():
            ctx_ref[...] = ext_ref[rows + 1:rows + POOL_MAXW, :]

    @pl.when(s == 0)
    def _():
        mixer_step(False)

    @pl.when(jnp.logical_and(s > 0, s < n_total))
    def _():
        mixer_step(True)

    _ffn_tail_steps(s, n_total, z_ref, xs_ref, o_ref, os_ref, ffn_steps)


def _gla_ffn_body(x_ref, xs_ref, gm_ref, win_ref, wa2_ref, ba_ref, gn_ref, wo_ref, tri_ref, masks_ref,
                  gfn_ref, gf_ref, w1f_ref, w3f_ref, w2f_ref,
                  o_ref, os_ref, sout_ref, s_ref, z_ref, w1_ref, w3_ref, w2_ref, st13_ref, st2_ref, sem13, sem2,
                  *, levels, chunk, n_tiles, n_total, final, ffn_layer):
    s = pl.program_id(0)
    t = jnp.minimum(s, n_total - 1) % n_tiles

    @pl.when(s == 0)
    def _():
        _fetch_and_cast(_ffn_weight_chunks(ffn_layer, (w1f_ref, w3f_ref, w2f_ref), (w1_ref, w3_ref, w2_ref),
                                           (st13_ref, st13_ref, st2_ref), (sem13, sem13, sem2)))

    def ffn_steps(ffn, src=None):
        return (_ffn_steps(z_ref[...] if src is None else src, gfn_ref, w1_ref, w3_ref, w2_ref, gf_ref, ffn,
                           final=final, col_chunks=FFN_COL_CHUNKS), FFN_STEP_COST)

    def mixer_step(with_ffn):
        @pl.when(t == 0)
        def _():
            s_ref[...] = jnp.zeros(s_ref.shape, F32)

        states = [s_ref[hd] for hd in range(GLA_HEADS)]
        mix, ffn = {}, {}
        streams = [(_gla_prompt_steps(x_ref[...], states, gm_ref, win_ref, wa2_ref, ba_ref, gn_ref,
                                      wo_ref, tri_ref, masks_ref, mix, levels=levels, chunk=chunk), GLA_STEP_COST)]
        if with_ffn:
            streams.append(ffn_steps(ffn))
        _interleave(*streams)
        states = mix["states"]
        z_ref[...] = mix["z"]
        if with_ffn:
            o_ref[...] = ffn["out"]
        for hd in range(GLA_HEADS):
            s_ref[hd] = states[hd]

        @pl.when(t == n_tiles - 1)
        def _():
            for hd in range(GLA_HEADS):
                sout_ref[hd] = states[hd]

    @pl.when(s == 0)
    def _():
        mixer_step(False)

    @pl.when(jnp.logical_and(s > 0, s < n_total))
    def _():
        mixer_step(True)

    _ffn_tail_steps(s, n_total, z_ref, xs_ref, o_ref, os_ref, ffn_steps)


def _prompt_layer(kind, x, xs, mixer_consts, ffn_consts, *, batch, seq, final, slot, n_slots, prev):
    n, d = x.shape
    rows = PROMPT_ROWS[kind]
    n_tiles = seq // rows
    n_total = batch * n_tiles
    assert seq % rows == 0 and rows >= POOL_MAXW and rows % GLA_CHUNK == 0
    g_ffn, w1f, w3f, w2f, gf = ffn_consts
    dff = w1f.shape[1]
    ffn_vmem = (g_ffn, gf)
    stage13 = (FFN_STAGE_SLOTS, d // FFN_WEIGHT_CHUNKS, dff)
    stage2 = (FFN_STAGE_SLOTS, dff // FFN_WEIGHT_CHUNKS, d)
    assert d % FFN_WEIGHT_CHUNKS == 0 and (dff // FFN_WEIGHT_CHUNKS) % (2 * V7X_SUBLANES) == 0
    ffn_scratch = [pltpu.VMEM((d, dff), BF16), pltpu.VMEM((d, dff), BF16), pltpu.VMEM((dff, d), BF16),
                   pltpu.VMEM(stage13, F32), pltpu.VMEM(stage2, F32),
                   pltpu.SemaphoreType.DMA((FFN_STAGE_SLOTS,)), pltpu.SemaphoreType.DMA((FFN_STAGE_SLOTS,))]
    resident = _nbytes(mixer_consts) + 3 * d * dff * 2 + 4 * int(np.prod(stage13)) * 2
    resident += 6 * rows * d * 4 + 3 * rows * dff * 4
    ns = xs.shape[0]
    assert ns % rows == 0
    n_sample = ns // rows
    x_spec = pl.BlockSpec((rows, d), lambda s: (jnp.minimum(s, n_total - 1), 0))
    o_spec = pl.BlockSpec((rows, d), lambda s: (jnp.clip(s - 1, 0, n_total - 1), 0))
    xs_spec = pl.BlockSpec((rows, d), lambda s: (jnp.clip(s - n_total - 1, 0, n_sample - 1), 0))
    seq_of = lambda s: jnp.minimum(s, n_total - 1) // n_tiles
    carry = [pltpu.VMEM((rows, d), F32)]
    if kind == "pool":
        body = functools.partial(_pool_ffn_body, rows=rows, n_tiles=n_tiles, n_total=n_total, final=final,
                                 ffn_layer=w1f.layer)
        state_shape = (batch, POOL_BUF, d)
        state_block = (None, POOL_BUF, d)
        state_index = lambda s: (seq_of(s), 0, 0)
        scratch = [pltpu.VMEM((rows + POOL_MAXW, d), F32)] + carry
        resident += (rows + POOL_MAXW) * d * 4
        consts = tuple(mixer_consts) + ffn_vmem
    else:
        win, wa2, wo = mixer_consts[1], mixer_consts[2], mixer_consts[5]
        dk = wa2.shape[1]
        dkh, dvh = dk // GLA_HEADS, wo.shape[0] // GLA_HEADS
        tri, masks, levels = _gla_tables(rows, GLA_CHUNK, GLA_CHUNK)
        body = functools.partial(_gla_ffn_body, levels=levels, chunk=GLA_CHUNK, n_tiles=n_tiles,
                                 n_total=n_total, final=final, ffn_layer=w1f.layer)
        state_shape = (batch, GLA_HEADS, dkh, dvh)
        state_block = (None, GLA_HEADS, dkh, dvh)
        state_index = lambda s: (seq_of(s), 0, 0, 0)
        scratch = [pltpu.VMEM((GLA_HEADS, dkh, dvh), F32)] + carry
        tables = (_Const(tri), _Const(masks))
        resident += _nbytes(tables) + 3 * GLA_HEADS * dkh * dvh * 4
        resident += rows * (win.shape[1] * 4 + (levels + 4) * dk * 4)
        consts = tuple(mixer_consts) + tables + ffn_vmem
    hbm = pl.BlockSpec(memory_space=pl.ANY)
    return _call_with_state_slot(
        body, (x, xs) + _arrays(consts) + (w1f.array, w3f.array, w2f.array),
        [x_spec, xs_spec] + _specs(consts) + [hbm] * 3,
        grid=(n_total + 1 + n_sample,),
        out_shapes=[jax.ShapeDtypeStruct((n, d), F32), jax.ShapeDtypeStruct((ns, d), F32)],
        out_specs=[o_spec, xs_spec],
        state_shape=state_shape, state_block=state_block, state_index=state_index,
        slot=slot, n_slots=n_slots, prev=prev,
        scratch_shapes=scratch + ffn_scratch, resident=resident,
        name=kind + ("_ffn_final" if final else "_ffn") + "_prompt")


def kernel(x_prompt, x_sample, state_pool, state_gla, norm_mix, norm_ffn, norm_final, pool_w, pool_scale,
           gla_w_in, gla_w_a2, gla_b_a, gla_norm, gla_w_o, ffn_w1, ffn_w3, ffn_w2):
    batch, seq, d = x_prompt.shape
    bs, steps, _ = x_sample.shape
    depth = norm_mix.shape[0]

    xp = x_prompt.reshape(batch * seq, d)
    xs = x_sample.reshape(bs * steps, d)

    rows_of = lambda a: a.reshape(a.shape[0], 1, -1)
    w_pool, w_in, w_a2, w_o = (a.astype(BF16) for a in (pool_w, gla_w_in, gla_w_a2, gla_w_o))
    g_mix, g_ffn, sc_pool, b_a, g_gla = (rows_of(a) for a in (norm_mix, norm_ffn, pool_scale, gla_b_a, gla_norm))
    gf = _Const(norm_final.reshape(1, -1))

    n_pool, n_gla = state_pool.shape[0], state_gla.shape[0]
    state_pool_t = jnp.swapaxes(state_pool, 1, 2)
    new_pool_p = new_pool_s = new_gla_p = new_gla_s = None
    for i in range(depth):
        j = i // 2
        final = i == depth - 1
        g = _Const(g_mix, i)
        ffn_consts = (_Const(g_ffn, i), _Const(ffn_w1, i), _Const(ffn_w3, i), _Const(ffn_w2, i), gf)
        if i % 2 == 0:
            pool_consts = (g, _Const(w_pool, j), _Const(sc_pool, j))
            xs3, new_pool_s = _pool_sample(xs.reshape(bs, steps, d), state_pool_t, j, pool_consts, new_pool_s)
            xp, xs, new_pool_p = _prompt_layer("pool", xp, xs3.reshape(bs * steps, d), pool_consts, ffn_consts,
                                               batch=batch, seq=seq, final=final, slot=j, n_slots=n_pool,
                                               prev=new_pool_p)
        else:
            gla_consts = (g, _Const(w_in, j), _Const(w_a2, j), _Const(b_a, j), _Const(g_gla, j), _Const(w_o, j))
            xs, new_gla_s = _gla_sample(xs, state_gla, j, gla_consts, new_gla_s, steps=steps)
            xp, xs, new_gla_p = _prompt_layer("gla", xp, xs, gla_consts, ffn_consts, batch=batch, seq=seq,
                                              final=final, slot=j, n_slots=n_gla, prev=new_gla_p)

    return (xp.reshape(batch, seq, d), xs.reshape(bs, steps, d),
            new_pool_p, new_gla_p, jnp.swapaxes(new_pool_s, 1, 2), new_gla_s)
```

```python
import functools

import numpy as np
import jax
import jax.numpy as jnp
from jax import lax
from jax.experimental import pallas as pl
from jax.experimental.pallas import tpu as pltpu

F32 = jnp.float32
BF16 = jnp.bfloat16

PAST_LEN = 16384
POOL_WINDOWS = (2, 4, 8, 16)
POOL_MAXW = max(POOL_WINDOWS)
POOL_BUF = POOL_MAXW - 1
GLA_HEADS = 4
GLA_GATE_TEMP = 16.0
EPS = 1e-6
LOG2_E = 1.4426950408889634

V7X_SUBLANES = 8
V7X_VMEM_BYTES = 64 * 1024 * 1024

PROMPT_ROWS = {"pool": 256, "gla": 256}
GLA_CHUNK = 128
FFN_WEIGHT_CHUNKS = 8
FFN_STAGE_SLOTS = 4
POOL_SEQS = 32
GLA_SEQS = 8
FFN_COL_CHUNKS = (1024, 1024, 768)
GLA_LEVEL_COST = 300
GLA_SCORE_COST = 300
GLA_FINISH_COST = 200
GLA_SCORE_LEAD = 2
FFN_STEP_COST = 8648
GLA_STEP_COST = 5882 + 7 * GLA_LEVEL_COST + 8 * (GLA_SCORE_COST + GLA_FINISH_COST)
POOL_STEP_COST = 2220


def _vmem_limit(resident_bytes):
    return int(min(V7X_VMEM_BYTES - (4 << 20), 2 * resident_bytes + (16 << 20)))


class _Const:
    def __init__(self, array, layer=None):
        self.array = array
        self.layer = layer
        self.shape = tuple(array.shape if layer is None else array.shape[1:])
        self.nbytes = int(np.prod(self.shape)) * array.dtype.itemsize

    def spec(self):
        nd = len(self.shape)
        if self.layer is None:
            return pl.BlockSpec(self.shape, lambda *_: (0,) * nd, pipeline_mode=pl.Buffered(1))
        layer = self.layer
        return pl.BlockSpec((None,) + self.shape, lambda *_: (layer,) + (0,) * nd, pipeline_mode=pl.Buffered(1))


def _nbytes(consts):
    return sum(c.nbytes for c in consts)


def _arrays(consts):
    return tuple(c.array for c in consts)


def _specs(consts):
    return [c.spec() for c in consts]


def _call_with_state_slot(body, args, in_specs, *, grid, out_shapes, out_specs, state_shape, state_block,
                          state_index, slot, n_slots, prev, scratch_shapes, resident, name):
    n_args = len(args)
    n_outs = len(out_shapes)
    inner = body
    aliases = {}
    if prev is None:
        state_spec = pl.BlockSpec((n_slots,) + tuple(state_block), lambda *g: (0,) + tuple(state_index(*g)))
        resident += 2 * (n_slots - 1) * int(np.prod([b for b in state_block if b is not None])) * 4

        def body(*refs):
            refs = list(refs)
            full = refs[n_args + n_outs]
            for other in range(n_slots):
                if other != slot:
                    full[other] = jnp.zeros(full.shape[1:], F32)
            refs[n_args + n_outs] = full.at[slot]
            return inner(*refs)
    else:
        state_spec = pl.BlockSpec((None,) + tuple(state_block), lambda *g: (slot,) + tuple(state_index(*g)))

        def body(*refs):
            return inner(*refs[:n_args], *refs[n_args + 1:])

        args = tuple(args) + (prev,)
        in_specs = list(in_specs) + [pl.BlockSpec(memory_space=pl.ANY)]
        aliases = {n_args: n_outs}
    return pl.pallas_call(
        body,
        out_shape=tuple(out_shapes) + (jax.ShapeDtypeStruct((n_slots,) + tuple(state_shape), F32),),
        grid=grid,
        in_specs=in_specs,
        out_specs=tuple(out_specs) + (state_spec,),
        scratch_shapes=scratch_shapes,
        input_output_aliases=aliases,
        compiler_params=pltpu.CompilerParams(
            dimension_semantics=("arbitrary",) * len(grid), vmem_limit_bytes=_vmem_limit(resident)),
        name=name,
    )(*args)


def _rmsnorm(x, g):
    ms = jnp.mean(x * x, axis=-1, keepdims=True)
    return x * lax.rsqrt(ms + EPS) * g


def _silu(x):
    hx = 0.5 * x
    return hx + hx * jnp.tanh(hx)


def _dot(a, b):
    return jnp.dot(a, b, preferred_element_type=F32)


def _dot_nt(a, b):
    return lax.dot_general(a, b, (((1,), (1,)), ((), ())), preferred_element_type=F32)


def _dot_tn(a, b):
    return lax.dot_general(a, b, (((0,), (0,)), ((), ())), preferred_element_type=F32)


def _split3(x):
    hi = x.astype(BF16)
    r1 = x - hi.astype(F32)
    mid = r1.astype(BF16)
    lo = (r1 - mid.astype(F32)).astype(BF16)
    return hi, mid, lo


def _pool_group_steps(ext_read, h, pos, w_ref, sc, result):
    gw = h.shape[-1] // len(POOL_WINDOWS)
    ys = []
    for g, w in enumerate(POOL_WINDOWS):
        cs = slice(g * gw, (g + 1) * gw)
        hg = h[:, cs]
        s = hg
        for k in range(1, w):
            s = s + ext_read(k, cs)
        cnt = jnp.minimum(pos + 1, w).astype(F32)
        diff = (s / cnt - hg).astype(BF16)
        ys.append(_dot(diff, w_ref[g]))
        yield (100 + 70 * (w - 1)) * h.shape[0] // 256
    result["y"] = jnp.concatenate(ys, axis=-1) * sc


def _pool_groups(ext_read, h, pos, w_ref, sc):
    result = {}
    for _ in _pool_group_steps(ext_read, h, pos, w_ref, sc, result):
        pass
    return result["y"]


def _pool_sample_body(x_ref, ctx_ref, g_ref, w_ref, sc_ref, o_ref, nctx_ref, ext_ref, *, seqs, steps):
    d = x_ref.shape[-1]
    ext_ref[0] = jnp.zeros((seqs, d), F32)
    ext_ref[1:POOL_MAXW] = ctx_ref[...]
    xs = [x_ref[:, t, :] for t in range(steps)]
    for t in range(steps):
        ext_ref[POOL_MAXW + t] = _rmsnorm(xs[t], g_ref[...])
    h = ext_ref[POOL_MAXW:POOL_MAXW + steps].reshape(steps * seqs, d)
    pos = PAST_LEN + lax.broadcasted_iota(jnp.int32, (steps * seqs, 1), 0) // seqs

    def ext_read(k, cs):
        return ext_ref[POOL_MAXW - k:POOL_MAXW - k + steps, :, cs].reshape(steps * seqs, cs.stop - cs.start)

    y = _pool_groups(ext_read, h, pos, w_ref, sc_ref[...])
    for t in range(steps):
        o_ref[:, t, :] = xs[t] + y[t * seqs:(t + 1) * seqs]
    nctx_ref[...] = ext_ref[steps + 1:steps + POOL_MAXW]


def _pool_sample(x, state_pool_t, layer, pool_consts, prev):
    bs, steps, d = x.shape
    seqs = POOL_SEQS
    assert bs % seqs == 0 and steps == V7X_SUBLANES
    resident = 2 * seqs * (2 * steps + 2 * POOL_MAXW) * d * 4 + seqs * (steps + POOL_MAXW) * d * 4
    return _call_with_state_slot(
        functools.partial(_pool_sample_body, seqs=seqs, steps=steps),
        (x, state_pool_t) + _arrays(pool_consts),
        [pl.BlockSpec((seqs, steps, d), lambda i: (i, 0, 0)),
         pl.BlockSpec((None, POOL_BUF, seqs, d), lambda i: (layer, 0, i, 0))]
        + _specs(pool_consts),
        grid=(bs // seqs,),
        out_shapes=[jax.ShapeDtypeStruct((bs, steps, d), F32)],
        out_specs=[pl.BlockSpec((seqs, steps, d), lambda i: (i, 0, 0))],
        state_shape=(POOL_BUF, bs, d), state_block=(POOL_BUF, seqs, d), state_index=lambda i: (0, i, 0),
        slot=layer, n_slots=state_pool_t.shape[0], prev=prev,
        scratch_shapes=[pltpu.VMEM((steps + POOL_MAXW, seqs, d), F32)],
        resident=resident, name="pool_sample")


def _gla_tables(rows, seg, score_rows):
    levels = int(np.log2(seg))
    assert 2 ** levels == seg and rows % score_rows == 0 and score_rows % seg == 0
    idx = np.arange(rows)
    tri = (idx[None, :] >= (idx[:, None] // seg) * seg) & (idx[None, :] <= idx[:, None])
    row = np.arange(score_rows)[:, None]
    col = np.arange(score_rows)[None, :]
    masks = [row == col]
    for l in range(1, levels + 1):
        s = 2 ** l
        mid = (row // s) * s + s // 2 - 1
        masks.append((row // s == col // s) & (row > mid) & (col <= mid))
    masks = np.stack(masks, axis=0).astype(np.float32)
    return jnp.asarray(tri.astype(np.float32), BF16), jnp.asarray(masks, BF16), levels


def _gla_front(x, g_ref, win_ref, wa2_ref, ba_ref, tri_ref):
    rank, dk = wa2_ref.shape
    dkh = dk // GLA_HEADS
    h = _rmsnorm(x, g_ref[...]).astype(BF16)
    a_low = _dot(h, win_ref[:, 6 * dk:6 * dk + rank])
    xg = _dot(a_low.astype(BF16), wa2_ref[...]) + ba_ref[...]
    la = -(jnp.maximum(-xg, 0.0) + jnp.log(1.0 + jnp.exp(-jnp.abs(xg)))) * (1.0 / GLA_GATE_TEMP)
    hi, mid, lo = _split3(la)
    tri = tri_ref[...]
    b = _dot(tri, hi) + _dot(tri, mid) + _dot(tri, lo)
    proj = _dot(h, win_ref[:, 0:6 * dk])
    q = proj[:, 0:dk] * (dkh ** -0.5)
    k = proj[:, dk:2 * dk]
    v = proj[:, 2 * dk:4 * dk]
    r = proj[:, 4 * dk:6 * dk]
    return q, k, v, r, b


def _gla_level_operand_steps(q16, k16, b, levels, ops):
    rows, dk = b.shape
    row = lax.broadcasted_iota(jnp.int32, (rows, 1), 0)
    sub = lax.broadcasted_iota(jnp.int32, (1, V7X_SUBLANES, 1), 1)
    b = b * LOG2_E
    for l in range(1, levels + 1):
        s = 2 ** l
        if s >= 2 * V7X_SUBLANES:
            b4 = b.reshape(rows // s, 2, s // 2, dk)
            m = b4[:, 0:1, s // 2 - 1:s // 2, :]
            x = jnp.concatenate([m - b4[:, 0:1], b4[:, 1:2] - m], axis=1)
        else:
            b3 = b.reshape(rows // V7X_SUBLANES, V7X_SUBLANES, dk)
            picks = [b3[:, j + s // 2 - 1:j + s // 2, :] for j in range(0, V7X_SUBLANES, s)]
            m = picks[-1]
            for idx in range(len(picks) - 2, -1, -1):
                m = jnp.where(sub < (idx + 1) * s, picks[idx], m)
            x = -jnp.abs(b3 - m)
        e = jnp.exp2(x).reshape(rows, dk).astype(BF16)
        second = ((row >> (l - 1)) & 1) == 1
        ops.append(jnp.where(second, q16, k16) * e)
        yield GLA_LEVEL_COST


def _gla_level_operands(q16, k16, b, levels):
    ops = []
    for _ in _gla_level_operand_steps(q16, k16, b, levels, ops):
        pass
    return ops


def _gla_intra(qb16_h, kb16_h, ops, masks_ref, rs, hs):
    a = _dot_nt(qb16_h, kb16_h).astype(BF16) * masks_ref[0]
    for l, t in enumerate(ops, start=1):
        th = t[rs, hs]
        a = a + _dot_nt(th, th).astype(BF16) * masks_ref[l]
    return a


def _lane_column(row_vec):
    return jnp.transpose(row_vec, (1, 0))


def _gla_out(x, o, r, gn_ref, wo_ref):
    dv = o.shape[-1]
    dvh = dv // GLA_HEADS
    normed = []
    for hd in range(GLA_HEADS):
        oh = o[:, hd * dvh:(hd + 1) * dvh]
        normed.append(oh * lax.rsqrt(jnp.mean(oh * oh, axis=-1, keepdims=True) + EPS))
    o = jnp.concatenate(normed, axis=-1) * gn_ref[...]
    o = (o * _silu(r)).astype(BF16)
    return x + _dot(o, wo_ref[...])


def _gla_prompt_steps(x, states, g_ref, win_ref, wa2_ref, ba_ref, gn_ref, wo_ref,
                      tri_ref, masks_ref, result, *, levels, chunk):
    rows = x.shape[0]
    rank, dk = wa2_ref.shape
    dkh = dk // GLA_HEADS
    h = _rmsnorm(x, g_ref[...]).astype(BF16)
    a_low = _dot(h, win_ref[:, 6 * dk:6 * dk + rank])
    yield 150
    q = _dot(h, win_ref[:, 0:dk]) * (dkh ** -0.5)
    q16 = q.astype(BF16)
    yield 512
    xg = _dot(a_low.astype(BF16), wa2_ref[...]) + ba_ref[...]
    yield 150
    k = _dot(h, win_ref[:, dk:2 * dk])
    k16 = k.astype(BF16)
    yield 512
    la = -(jnp.maximum(-xg, 0.0) + jnp.log(1.0 + jnp.exp(-jnp.abs(xg)))) * (1.0 / GLA_GATE_TEMP)
    hi, mid, lo = _split3(la)
    yield 400
    vb = _dot(h, win_ref[:, 2 * dk:4 * dk]).astype(BF16)
    yield 1024
    dvh = vb.shape[-1] // GLA_HEADS
    tri = tri_ref[...]
    b = _dot(tri, hi) + _dot(tri, mid) + _dot(tri, lo)
    yield 350
    r = _dot(h, win_ref[:, 4 * dk:6 * dk])
    yield 1024
    ops = []
    yield from _gla_level_operand_steps(q16, k16, b, levels, ops)
    qb = (q * jnp.exp(b)).astype(BF16)
    yield 60

    states = list(states)
    pieces = [(ci, hd) for ci in range(rows // chunk) for hd in range(GLA_HEADS)]
    kends = {}
    outs = {}

    def scores(ci, hd):
        rs = slice(ci * chunk, (ci + 1) * chunk)
        hs = slice(hd * dkh, (hd + 1) * dkh)
        return _gla_intra(q16[rs, hs], k16[rs, hs], ops, masks_ref, rs, hs)

    def finish(ci, hd, a16):
        rs = slice(ci * chunk, (ci + 1) * chunk)
        hs = slice(hd * dkh, (hd + 1) * dkh)
        vs = slice(hd * dvh, (hd + 1) * dvh)
        b_c = b[rs]
        b_last = b_c[chunk - 1:chunk, :]
        if ci not in kends:
            kends[ci] = (k[rs] * jnp.exp(b_last - b_c)).astype(BF16)
        s_old = states[hd]
        outs[ci, hd] = _dot(a16, vb[rs, vs]) + _dot(qb[rs, hs], s_old.astype(BF16))
        dec = jnp.exp(_lane_column(b_last[:, hs]))
        states[hd] = dec * s_old + _dot_tn(kends[ci][:, hs], vb[rs, vs])

    pending = []
    for piece in pieces:
        pending.append((*piece, scores(*piece)))
        yield GLA_SCORE_COST
        if len(pending) > GLA_SCORE_LEAD:
            finish(*pending.pop(0))
            yield GLA_FINISH_COST
    while pending:
        finish(*pending.pop(0))
        yield GLA_FINISH_COST
    o = jnp.concatenate(
        [jnp.concatenate([outs[ci, hd] for hd in range(GLA_HEADS)], axis=-1) for ci in range(rows // chunk)],
        axis=0)
    result["z"] = _gla_out(x, o, r, gn_ref, wo_ref)
    result["states"] = states
    yield 1700


def _ffn_steps(x, g_ref, w1_ref, w3_ref, w2_ref, gf_ref, result, *, final, col_chunks):
    rows = x.shape[0]
    h = _rmsnorm(x, g_ref[...]).astype(BF16)
    yield 200
    acc = x
    lo = 0
    pending = None
    for cols in col_chunks:
        cs = slice(lo, lo + cols)
        lo += cols
        u1 = _dot(h, w1_ref[:, cs])
        yield rows * cols // 256
        u3 = _dot(h, w3_ref[:, cs])
        yield rows * cols // 256
        if pending is not None:
            acc = acc + _dot(pending[0], w2_ref[pending[1], :])
            yield rows * (pending[1].stop - pending[1].start) // 256
        pending = ((_silu(u1) * u3).astype(BF16), cs)
    acc = acc + _dot(pending[0], w2_ref[pending[1], :])
    yield rows * (pending[1].stop - pending[1].start) // 256
    if final:
        acc = _rmsnorm(acc, gf_ref[...])
    result["out"] = acc


def _ffn_weight_chunks(layer, wf_refs, ws_refs, stage_refs, sem_refs):
    chunks = []
    used = {}
    for wf, ws, stage, sem in zip(wf_refs, ws_refs, stage_refs, sem_refs):
        n_rows = ws.shape[0] // FFN_WEIGHT_CHUNKS
        for i in range(FFN_WEIGHT_CHUNKS):
            rs = pl.ds(i * n_rows, n_rows)
            slot = used.get(id(stage), 0) % FFN_STAGE_SLOTS
            used[id(stage)] = used.get(id(stage), 0) + 1
            chunks.append((wf.at[layer, rs, :], stage.at[slot], sem.at[slot], ws.at[rs, :]))
    return chunks


def _fetch_steps(chunks):
    ahead = FFN_STAGE_SLOTS - 1
    copies = [pltpu.make_async_copy(src, stage, sem) for src, stage, sem, _ in chunks]
    for i in range(min(ahead, len(chunks))):
        copies[i].start()
    yield 0.01
    for i, (_, stage, _, dst) in enumerate(chunks):
        if i + ahead < len(chunks):
            copies[i + ahead].start()
        copies[i].wait()
        dst[...] = stage[...].astype(BF16)
        yield 1


def _interleave(*pairs):
    gens = [p[0] for p in pairs]
    totals = [float(p[1]) for p in pairs]
    done = [0.0] * len(gens)
    live = list(range(len(gens)))
    while live:
        i = min(live, key=lambda j: done[j] / totals[j])
        try:
            done[i] += next(gens[i])
        except StopIteration:
            live.remove(i)


def _gla_sample_body(x_ref, s0_ref, g_ref, win_ref, wa2_ref, ba_ref, gn_ref, wo_ref,
                     tri_ref, masks_ref, o_ref, sout_ref, *, levels, seqs, steps):
    x = x_ref[...]
    rows = x.shape[0]
    q, k, v, r, b = _gla_front(x, g_ref, win_ref, wa2_ref, ba_ref, tri_ref)
    dk = q.shape[-1]
    dkh = dk // GLA_HEADS
    dvh = v.shape[-1] // GLA_HEADS
    q16 = q.astype(BF16)
    k16 = k.astype(BF16)
    ops = _gla_level_operands(q16, k16, b, levels)
    b3 = b.reshape(seqs, steps, dk)
    b_last = b3[:, steps - 1:steps, :]
    qb = (q * jnp.exp(b)).astype(BF16)
    kend = (k.reshape(seqs, steps, dk) * jnp.exp(b_last - b3)).reshape(rows, dk).astype(BF16)
    vb = v.astype(BF16)
    all_rows = slice(0, rows)
    outs = []
    for hd in range(GLA_HEADS):
        hs = slice(hd * dkh, (hd + 1) * dkh)
        vs = slice(hd * dvh, (hd + 1) * dvh)
        a = _gla_intra(q16[:, hs], k16[:, hs], ops, masks_ref, all_rows, hs)
        o_h = _dot(a, vb[:, vs])
        o_state = []
        for s in range(seqs):
            rs = slice(s * steps, (s + 1) * steps)
            s_old = s0_ref[s, hd]
            o_state.append(_dot(qb[rs, hs], s_old.astype(BF16)))
            dec = jnp.exp(_lane_column(b[(s + 1) * steps - 1:(s + 1) * steps, hs]))
            sout_ref[s, hd] = dec * s_old + _dot_tn(kend[rs, hs], vb[rs, vs])
        outs.append(o_h + jnp.concatenate(o_state, axis=0))
    o_ref[...] = _gla_out(x, jnp.concatenate(outs, axis=-1), r, gn_ref, wo_ref)


def _gla_sample(x, state_gla, layer, gla_consts, prev, *, steps):
    n, d = x.shape
    seqs = GLA_SEQS if prev is None else 2 * GLA_SEQS
    rows = seqs * steps
    bs = n // steps
    assert bs % seqs == 0
    win, wa2, wo = gla_consts[1], gla_consts[2], gla_consts[5]
    dk = wa2.shape[1]
    dkh, dvh = dk // GLA_HEADS, wo.shape[0] // GLA_HEADS
    tri, masks, levels = _gla_tables(rows, steps, rows)
    consts = tuple(gla_consts) + (_Const(tri), _Const(masks))
    state_block = seqs * GLA_HEADS * dkh * dvh * 4
    resident = _nbytes(consts) + 4 * rows * d * 4 + 4 * state_block
    resident += rows * (win.shape[1] * 4 + (levels + 4) * dk * 4)
    return _call_with_state_slot(
        functools.partial(_gla_sample_body, levels=levels, seqs=seqs, steps=steps),
        (x, state_gla) + _arrays(consts),
        [pl.BlockSpec((rows, d), lambda i: (i, 0)),
         pl.BlockSpec((None, seqs, GLA_HEADS, dkh, dvh), lambda i: (layer, i, 0, 0, 0))]
        + _specs(consts),
        grid=(bs // seqs,),
        out_shapes=[jax.ShapeDtypeStruct((n, d), F32)],
        out_specs=[pl.BlockSpec((rows, d), lambda i: (i, 0))],
        state_shape=(bs, GLA_HEADS, dkh, dvh), state_block=(seqs, GLA_HEADS, dkh, dvh),
        state_index=lambda i: (i, 0, 0, 0),
        slot=layer, n_slots=state_gla.shape[0], prev=prev,
        scratch_shapes=[], resident=resident, name="gla_sample")


def _ffn_tail_steps(s, n_total, z_ref, xs_ref, o_ref, os_ref, ffn_steps):
    @pl.when(s >= n_total)
    def _():
        ffn = {}
        _interleave(ffn_steps(ffn, jnp.where(s == n_total, z_ref[...], xs_ref[...])))

        @pl.when(s == n_total)
        def _():
            o_ref[...] = ffn["out"]

        @pl.when(s > n_total)
        def _():
            os_ref[...] = ffn["out"]


def _pool_ffn_body(x_ref, xs_ref, gm_ref, wp_ref, sc_ref, gfn_ref, gf_ref, w1f_ref, w3f_ref, w2f_ref,
                   o_ref, os_ref, ctx_ref, ext_ref, z_ref, w1_ref, w3_ref, w2_ref, st13_ref, st2_ref, sem13, sem2,
                   *, rows, n_tiles, n_total, final, ffn_layer):
    s = pl.program_id(0)
    d = x_ref.shape[-1]
    t = jnp.minimum(s, n_total - 1) % n_tiles

    def fetch_steps():
        chunks = _ffn_weight_chunks(ffn_layer, (w1f_ref, w3f_ref, w2f_ref), (w1_ref, w3_ref, w2_ref),
                                    (st13_ref, st13_ref, st2_ref), (sem13, sem13, sem2))
        return (_fetch_steps(chunks), len(chunks))

    def ffn_steps(ffn, src=None):
        return (_ffn_steps(z_ref[...] if src is None else src, gfn_ref, w1_ref, w3_ref, w2_ref, gf_ref, ffn,
                           final=final, col_chunks=FFN_COL_CHUNKS), FFN_STEP_COST * rows // 256)

    def mixer_step(with_ffn):
        @pl.when(t == 0)
        def _():
            ext_ref[0:POOL_MAXW, :] = jnp.zeros((POOL_MAXW, d), F32)

        x = x_ref[...]
        h = _rmsnorm(x, gm_ref[...])
        ext_ref[POOL_MAXW:POOL_MAXW + rows, :] = h
        pos = t * rows + lax.broadcasted_iota(jnp.int32, (rows, 1), 0)

        def ext_read(k, cs):
            return ext_ref[POOL_MAXW - k:POOL_MAXW - k + rows, cs]

        mix, ffn = {}, {}
        mixer = (_pool_group_steps(ext_read, h, pos, wp_ref, sc_ref[...], mix), POOL_STEP_COST * rows // 256)
        streams = [mixer, ffn_steps(ffn)] if with_ffn else [fetch_steps(), mixer]
        _interleave(*streams)
        z_ref[...] = x + mix["y"]
        if with_ffn:
            o_ref[...] = ffn["out"]
        ext_ref[0:POOL_MAXW, :] = ext_ref[rows:rows + POOL_MAXW, :]

        @pl.when(t == n_tiles - 1)
        def _():
            ctx_ref[...] = ext_ref[rows + 1:rows + POOL_MAXW, :]

    @pl.when(s == 0)
    def _():
        mixer_step(False)

    @pl.when(jnp.logical_and(s > 0, s < n_total))
    def _():
        mixer_step(True)

    _ffn_tail_steps(s, n_total, z_ref, xs_ref, o_ref, os_ref, ffn_steps)


def _gla_ffn_body(x_ref, xs_ref, gm_ref, win_ref, wa2_ref, ba_ref, gn_ref, wo_ref, tri_ref, masks_ref,
                  gfn_ref, gf_ref, w1f_ref, w3f_ref, w2f_ref,
                  o_ref, os_ref, sout_ref, s_ref, z_ref, w1_ref, w3_ref, w2_ref, st13_ref, st2_ref, sem13, sem2,
                  *, levels, chunk, n_tiles, n_total, final, ffn_layer):
    s = pl.program_id(0)
    t = jnp.minimum(s, n_total - 1) % n_tiles

    def fetch_steps():
        chunks = _ffn_weight_chunks(ffn_layer, (w1f_ref, w3f_ref, w2f_ref), (w1_ref, w3_ref, w2_ref),
                                    (st13_ref, st13_ref, st2_ref), (sem13, sem13, sem2))
        return (_fetch_steps(chunks), len(chunks))

    def ffn_steps(ffn, src=None):
        return (_ffn_steps(z_ref[...] if src is None else src, gfn_ref, w1_ref, w3_ref, w2_ref, gf_ref, ffn,
                           final=final, col_chunks=FFN_COL_CHUNKS), FFN_STEP_COST)

    def mixer_step(with_ffn):
        @pl.when(t == 0)
        def _():
            s_ref[...] = jnp.zeros(s_ref.shape, F32)

        states = [s_ref[hd] for hd in range(GLA_HEADS)]
        mix, ffn = {}, {}
        mixer = (_gla_prompt_steps(x_ref[...], states, gm_ref, win_ref, wa2_ref, ba_ref, gn_ref,
                                   wo_ref, tri_ref, masks_ref, mix, levels=levels, chunk=chunk), GLA_STEP_COST)
        streams = [mixer, ffn_steps(ffn)] if with_ffn else [fetch_steps(), mixer]
        _interleave(*streams)
        states = mix["states"]
        z_ref[...] = mix["z"]
        if with_ffn:
            o_ref[...] = ffn["out"]
        for hd in range(GLA_HEADS):
            s_ref[hd] = states[hd]

        @pl.when(t == n_tiles - 1)
        def _():
            for hd in range(GLA_HEADS):
                sout_ref[hd] = states[hd]

    @pl.when(s == 0)
    def _():
        mixer_step(False)

    @pl.when(jnp.logical_and(s > 0, s < n_total))
    def _():
        mixer_step(True)

    _ffn_tail_steps(s, n_total, z_ref, xs_ref, o_ref, os_ref, ffn_steps)


def _prompt_layer(kind, x, xs, mixer_consts, ffn_consts, *, batch, seq, final, slot, n_slots, prev):
    n, d = x.shape
    rows = PROMPT_ROWS[kind]
    n_tiles = seq // rows
    n_total = batch * n_tiles
    assert seq % rows == 0 and rows >= POOL_MAXW and rows % GLA_CHUNK == 0
    g_ffn, w1f, w3f, w2f, gf = ffn_consts
    dff = w1f.shape[1]
    ffn_vmem = (g_ffn, gf)
    stage13 = (FFN_STAGE_SLOTS, d // FFN_WEIGHT_CHUNKS, dff)
    stage2 = (FFN_STAGE_SLOTS, dff // FFN_WEIGHT_CHUNKS, d)
    assert d % FFN_WEIGHT_CHUNKS == 0 and (dff // FFN_WEIGHT_CHUNKS) % (2 * V7X_SUBLANES) == 0
    ffn_scratch = [pltpu.VMEM((d, dff), BF16), pltpu.VMEM((d, dff), BF16), pltpu.VMEM((dff, d), BF16),
                   pltpu.VMEM(stage13, F32), pltpu.VMEM(stage2, F32),
                   pltpu.SemaphoreType.DMA((FFN_STAGE_SLOTS,)), pltpu.SemaphoreType.DMA((FFN_STAGE_SLOTS,))]
    resident = _nbytes(mixer_consts) + 3 * d * dff * 2 + 4 * int(np.prod(stage13)) * 2
    resident += 6 * rows * d * 4 + 3 * rows * dff * 4
    ns = xs.shape[0]
    assert ns % rows == 0
    n_sample = ns // rows
    x_spec = pl.BlockSpec((rows, d), lambda s: (jnp.minimum(s, n_total - 1), 0))
    o_spec = pl.BlockSpec((rows, d), lambda s: (jnp.clip(s - 1, 0, n_total - 1), 0))
    xs_spec = pl.BlockSpec((rows, d), lambda s: (jnp.clip(s - n_total - 1, 0, n_sample - 1), 0))
    seq_of = lambda s: jnp.minimum(s, n_total - 1) // n_tiles
    carry = [pltpu.VMEM((rows, d), F32)]
    if kind == "pool":
        body = functools.partial(_pool_ffn_body, rows=rows, n_tiles=n_tiles, n_total=n_total, final=final,
                                 ffn_layer=w1f.layer)
        state_shape = (batch, POOL_BUF, d)
        state_block = (None, POOL_BUF, d)
        state_index = lambda s: (seq_of(s), 0, 0)
        scratch = [pltpu.VMEM((rows + POOL_MAXW, d), F32)] + carry
        resident += (rows + POOL_MAXW) * d * 4
        consts = tuple(mixer_consts) + ffn_vmem
    else:
        win, wa2, wo = mixer_consts[1], mixer_consts[2], mixer_consts[5]
        dk = wa2.shape[1]
        dkh, dvh = dk // GLA_HEADS, wo.shape[0] // GLA_HEADS
        tri, masks, levels = _gla_tables(rows, GLA_CHUNK, GLA_CHUNK)
        body = functools.partial(_gla_ffn_body, levels=levels, chunk=GLA_CHUNK, n_tiles=n_tiles,
                                 n_total=n_total, final=final, ffn_layer=w1f.layer)
        state_shape = (batch, GLA_HEADS, dkh, dvh)
        state_block = (None, GLA_HEADS, dkh, dvh)
        state_index = lambda s: (seq_of(s), 0, 0, 0)
        scratch = [pltpu.VMEM((GLA_HEADS, dkh, dvh), F32)] + carry
        tables = (_Const(tri), _Const(masks))
        resident += _nbytes(tables) + 3 * GLA_HEADS * dkh * dvh * 4
        resident += rows * (win.shape[1] * 4 + (levels + 4) * dk * 4)
        consts = tuple(mixer_consts) + tables + ffn_vmem
    hbm = pl.BlockSpec(memory_space=pl.ANY)
    return _call_with_state_slot(
        body, (x, xs) + _arrays(consts) + (w1f.array, w3f.array, w2f.array),
        [x_spec, xs_spec] + _specs(consts) + [hbm] * 3,
        grid=(n_total + 1 + n_sample,),
        out_shapes=[jax.ShapeDtypeStruct((n, d), F32), jax.ShapeDtypeStruct((ns, d), F32)],
        out_specs=[o_spec, xs_spec],
        state_shape=state_shape, state_block=state_block, state_index=state_index,
        slot=slot, n_slots=n_slots, prev=prev,
        scratch_shapes=scratch + ffn_scratch, resident=resident,
        name=kind + ("_ffn_final" if final else "_ffn") + "_prompt")


def kernel(x_prompt, x_sample, state_pool, state_gla, norm_mix, norm_ffn, norm_final, pool_w, pool_scale,
           gla_w_in, gla_w_a2, gla_b_a, gla_norm, gla_w_o, ffn_w1, ffn_w3, ffn_w2):
    batch, seq, d = x_prompt.shape
    bs, steps, _ = x_sample.shape
    depth = norm_mix.shape[0]

    xp = x_prompt.reshape(batch * seq, d)
    xs = x_sample.reshape(bs * steps, d)

    rows_of = lambda a: a.reshape(a.shape[0], 1, -1)
    w_pool, w_in, w_a2, w_o = (a.astype(BF16) for a in (pool_w, gla_w_in, gla_w_a2, gla_w_o))
    g_mix, g_ffn, sc_pool, b_a, g_gla = (rows_of(a) for a in (norm_mix, norm_ffn, pool_scale, gla_b_a, gla_norm))
    gf = _Const(norm_final.reshape(1, -1))

    n_pool, n_gla = state_pool.shape[0], state_gla.shape[0]
    state_pool_t = jnp.swapaxes(state_pool, 1, 2)
    new_pool_p = new_pool_s = new_gla_p = new_gla_s = None
    for i in range(depth):
        j = i // 2
        final = i == depth - 1
        g = _Const(g_mix, i)
        ffn_consts = (_Const(g_ffn, i), _Const(ffn_w1, i), _Const(ffn_w3, i), _Const(ffn_w2, i), gf)
        if i % 2 == 0:
            pool_consts = (g, _Const(w_pool, j), _Const(sc_pool, j))
            xs3, new_pool_s = _pool_sample(xs.reshape(bs, steps, d), state_pool_t, j, pool_consts, new_pool_s)
            xp, xs, new_pool_p = _prompt_layer("pool", xp, xs3.reshape(bs * steps, d), pool_consts, ffn_consts,
                                               batch=batch, seq=seq, final=final, slot=j, n_slots=n_pool,
                                               prev=new_pool_p)
        else:
            gla_consts = (g, _Const(w_in, j), _Const(w_a2, j), _Const(b_a, j), _Const(g_gla, j), _Const(w_o, j))
            xs, new_gla_s = _gla_sample(xs, state_gla, j, gla_consts, new_gla_s, steps=steps)
            xp, xs, new_gla_p = _prompt_layer("gla", xp, xs, gla_consts, ffn_consts, batch=batch, seq=seq,
                                              final=final, slot=j, n_slots=n_gla, prev=new_gla_p)

    return (xp.reshape(batch, seq, d), xs.reshape(bs, steps, d),
            new_pool_p, new_gla_p, jnp.swapaxes(new_pool_s, 1, 2), new_gla_s)
```

```python
import functools

import numpy as np
import jax
import jax.numpy as jnp
from jax import lax
from jax.experimental import pallas as pl
from jax.experimental.pallas import tpu as pltpu

F32 = jnp.float32
BF16 = jnp.bfloat16

PAST_LEN = 16384
POOL_WINDOWS = (2, 4, 8, 16)
POOL_MAXW = max(POOL_WINDOWS)
POOL_BUF = POOL_MAXW - 1
GLA_HEADS = 4
GLA_GATE_TEMP = 16.0
EPS = 1e-6
LOG2_E = 1.4426950408889634

V7X_SUBLANES = 8
V7X_VMEM_BYTES = 64 * 1024 * 1024

PROMPT_ROWS = {"pool": 256, "gla": 256}
GLA_CHUNK = 128
FFN_WEIGHT_CHUNKS = 8
FFN_STAGE_SLOTS = 4
POOL_SEQS = 32
GLA_SEQS = 8
FFN_COL_CHUNKS = (1024, 1024, 768)
GLA_LEVEL_COST = 300
GLA_SCORE_COST = 300
GLA_FINISH_COST = 200
GLA_SCORE_LEAD = 2
FFN_STEP_COST = 8648
GLA_STEP_COST = 5882 + 7 * GLA_LEVEL_COST + 8 * (GLA_SCORE_COST + GLA_FINISH_COST)
POOL_STEP_COST = 2220


def _vmem_limit(resident_bytes):
    return int(min(V7X_VMEM_BYTES - (4 << 20), 2 * resident_bytes + (16 << 20)))


class _Const:
    def __init__(self, array, layer=None):
        self.array = array
        self.layer = layer
        self.shape = tuple(array.shape if layer is None else array.shape[1:])
        self.nbytes = int(np.prod(self.shape)) * array.dtype.itemsize

    def spec(self):
        nd = len(self.shape)
        if self.layer is None:
            return pl.BlockSpec(self.shape, lambda *_: (0,) * nd, pipeline_mode=pl.Buffered(1))
        layer = self.layer
        return pl.BlockSpec((None,) + self.shape, lambda *_: (layer,) + (0,) * nd, pipeline_mode=pl.Buffered(1))


def _nbytes(consts):
    return sum(c.nbytes for c in consts)


def _arrays(consts):
    return tuple(c.array for c in consts)


def _specs(consts):
    return [c.spec() for c in consts]


def _call_with_state_slot(body, args, in_specs, *, grid, out_shapes, out_specs, state_shape, state_block,
                          state_index, slot, n_slots, prev, scratch_shapes, resident, name):
    n_args = len(args)
    n_outs = len(out_shapes)
    inner = body
    aliases = {}
    if prev is None:
        state_spec = pl.BlockSpec((n_slots,) + tuple(state_block), lambda *g: (0,) + tuple(state_index(*g)))
        resident += 2 * (n_slots - 1) * int(np.prod([b for b in state_block if b is not None])) * 4

        def body(*refs):
            refs = list(refs)
            full = refs[n_args + n_outs]
            for other in range(n_slots):
                if other != slot:
                    full[other] = jnp.zeros(full.shape[1:], F32)
            refs[n_args + n_outs] = full.at[slot]
            return inner(*refs)
    else:
        state_spec = pl.BlockSpec((None,) + tuple(state_block), lambda *g: (slot,) + tuple(state_index(*g)))

        def body(*refs):
            return inner(*refs[:n_args], *refs[n_args + 1:])

        args = tuple(args) + (prev,)
        in_specs = list(in_specs) + [pl.BlockSpec(memory_space=pl.ANY)]
        aliases = {n_args: n_outs}
    return pl.pallas_call(
        body,
        out_shape=tuple(out_shapes) + (jax.ShapeDtypeStruct((n_slots,) + tuple(state_shape), F32),),
        grid=grid,
        in_specs=in_specs,
        out_specs=tuple(out_specs) + (state_spec,),
        scratch_shapes=scratch_shapes,
        input_output_aliases=aliases,
        compiler_params=pltpu.CompilerParams(
            dimension_semantics=("arbitrary",) * len(grid), vmem_limit_bytes=_vmem_limit(resident)),
        name=name,
    )(*args)


def _rmsnorm(x, g):
    ms = jnp.mean(x * x, axis=-1, keepdims=True)
    return x * lax.rsqrt(ms + EPS) * g


def _silu(x):
    hx = 0.5 * x
    return hx + hx * jnp.tanh(hx)


def _dot(a, b):
    return jnp.dot(a, b, preferred_element_type=F32)


def _dot_nt(a, b):
    return lax.dot_general(a, b, (((1,), (1,)), ((), ())), preferred_element_type=F32)


def _dot_tn(a, b):
    return lax.dot_general(a, b, (((0,), (0,)), ((), ())), preferred_element_type=F32)


def _split3(x):
    hi = x.astype(BF16)
    r1 = x - hi.astype(F32)
    mid = r1.astype(BF16)
    lo = (r1 - mid.astype(F32)).astype(BF16)
    return hi, mid, lo


def _pool_group_steps(ext_read, h, pos, w_ref, sc, result):
    gw = h.shape[-1] // len(POOL_WINDOWS)
    ys = []
    for g, w in enumerate(POOL_WINDOWS):
        cs = slice(g * gw, (g + 1) * gw)
        hg = h[:, cs]
        s = hg
        for k in range(1, w):
            s = s + ext_read(k, cs)
        cnt = jnp.minimum(pos + 1, w).astype(F32)
        diff = (s / cnt - hg).astype(BF16)
        ys.append(_dot(diff, w_ref[g]))
        yield (100 + 70 * (w - 1)) * h.shape[0] // 256
    result["y"] = jnp.concatenate(ys, axis=-1) * sc


def _pool_groups(ext_read, h, pos, w_ref, sc):
    result = {}
    for _ in _pool_group_steps(ext_read, h, pos, w_ref, sc, result):
        pass
    return result["y"]


def _pool_sample_body(x_ref, ctx_ref, g_ref, w_ref, sc_ref, o_ref, nctx_ref, ext_ref, *, seqs, steps):
    d = x_ref.shape[-1]
    ext_ref[0] = jnp.zeros((seqs, d), F32)
    ext_ref[1:POOL_MAXW] = ctx_ref[...]
    xs = [x_ref[:, t, :] for t in range(steps)]
    for t in range(steps):
        ext_ref[POOL_MAXW + t] = _rmsnorm(xs[t], g_ref[...])
    h = ext_ref[POOL_MAXW:POOL_MAXW + steps].reshape(steps * seqs, d)
    pos = PAST_LEN + lax.broadcasted_iota(jnp.int32, (steps * seqs, 1), 0) // seqs

    def ext_read(k, cs):
        return ext_ref[POOL_MAXW - k:POOL_MAXW - k + steps, :, cs].reshape(steps * seqs, cs.stop - cs.start)

    y = _pool_groups(ext_read, h, pos, w_ref, sc_ref[...])
    for t in range(steps):
        o_ref[:, t, :] = xs[t] + y[t * seqs:(t + 1) * seqs]
    nctx_ref[...] = ext_ref[steps + 1:steps + POOL_MAXW]


def _pool_sample(x, state_pool_t, layer, pool_consts, prev):
    bs, steps, d = x.shape
    seqs = POOL_SEQS
    assert bs % seqs == 0 and steps == V7X_SUBLANES
    resident = 2 * seqs * (2 * steps + 2 * POOL_MAXW) * d * 4 + seqs * (steps + POOL_MAXW) * d * 4
    return _call_with_state_slot(
        functools.partial(_pool_sample_body, seqs=seqs, steps=steps),
        (x, state_pool_t) + _arrays(pool_consts),
        [pl.BlockSpec((seqs, steps, d), lambda i: (i, 0, 0)),
         pl.BlockSpec((None, POOL_BUF, seqs, d), lambda i: (layer, 0, i, 0))]
        + _specs(pool_consts),
        grid=(bs // seqs,),
        out_shapes=[jax.ShapeDtypeStruct((bs, steps, d), F32)],
        out_specs=[pl.BlockSpec((seqs, steps, d), lambda i: (i, 0, 0))],
        state_shape=(POOL_BUF, bs, d), state_block=(POOL_BUF, seqs, d), state_index=lambda i: (0, i, 0),
        slot=layer, n_slots=state_pool_t.shape[0], prev=prev,
        scratch_shapes=[pltpu.VMEM((steps + POOL_MAXW, seqs, d), F32)],
        resident=resident, name="pool_sample")


def _gla_tables(rows, seg, score_rows):
    levels = int(np.log2(seg))
    assert 2 ** levels == seg and rows % score_rows == 0 and score_rows % seg == 0
    idx = np.arange(rows)
    tri = (idx[None, :] >= (idx[:, None] // seg) * seg) & (idx[None, :] <= idx[:, None])
    row = np.arange(score_rows)[:, None]
    col = np.arange(score_rows)[None, :]
    masks = [row == col]
    for l in range(1, levels + 1):
        s = 2 ** l
        mid = (row // s) * s + s // 2 - 1
        masks.append((row // s == col // s) & (row > mid) & (col <= mid))
    masks = np.stack(masks, axis=0).astype(np.float32)
    return jnp.asarray(tri.astype(np.float32), BF16), jnp.asarray(masks, BF16), levels


def _gla_front(x, g_ref, win_ref, wa2_ref, ba_ref, tri_ref):
    rank, dk = wa2_ref.shape
    dkh = dk // GLA_HEADS
    h = _rmsnorm(x, g_ref[...]).astype(BF16)
    a_low = _dot(h, win_ref[:, 6 * dk:6 * dk + rank])
    xg = _dot(a_low.astype(BF16), wa2_ref[...]) + ba_ref[...]
    la = -(jnp.maximum(-xg, 0.0) + jnp.log(1.0 + jnp.exp(-jnp.abs(xg)))) * (1.0 / GLA_GATE_TEMP)
    hi, mid, lo = _split3(la)
    tri = tri_ref[...]
    b = _dot(tri, hi) + _dot(tri, mid) + _dot(tri, lo)
    proj = _dot(h, win_ref[:, 0:6 * dk])
    q = proj[:, 0:dk] * (dkh ** -0.5)
    k = proj[:, dk:2 * dk]
    v = proj[:, 2 * dk:4 * dk]
    r = proj[:, 4 * dk:6 * dk]
    return q, k, v, r, b


def _gla_level_operand_steps(q16, k16, b, levels, ops):
    rows, dk = b.shape
    row = lax.broadcasted_iota(jnp.int32, (rows, 1), 0)
    sub = lax.broadcasted_iota(jnp.int32, (1, V7X_SUBLANES, 1), 1)
    b = b * LOG2_E
    for l in range(1, levels + 1):
        s = 2 ** l
        if s >= 2 * V7X_SUBLANES:
            b4 = b.reshape(rows // s, 2, s // 2, dk)
            m = b4[:, 0:1, s // 2 - 1:s // 2, :]
            x = jnp.concatenate([m - b4[:, 0:1], b4[:, 1:2] - m], axis=1)
        else:
            b3 = b.reshape(rows // V7X_SUBLANES, V7X_SUBLANES, dk)
            picks = [b3[:, j + s // 2 - 1:j + s // 2, :] for j in range(0, V7X_SUBLANES, s)]
            m = picks[-1]
            for idx in range(len(picks) - 2, -1, -1):
                m = jnp.where(sub < (idx + 1) * s, picks[idx], m)
            x = -jnp.abs(b3 - m)
        e = jnp.exp2(x).reshape(rows, dk).astype(BF16)
        second = ((row >> (l - 1)) & 1) == 1
        ops.append(jnp.where(second, q16, k16) * e)
        yield GLA_LEVEL_COST


def _gla_level_operands(q16, k16, b, levels):
    ops = []
    for _ in _gla_level_operand_steps(q16, k16, b, levels, ops):
        pass
    return ops


def _gla_intra(qb16_h, kb16_h, ops, masks_ref, rs, hs):
    a = _dot_nt(qb16_h, kb16_h).astype(BF16) * masks_ref[0]
    for l, t in enumerate(ops, start=1):
        th = t[rs, hs]
        a = a + _dot_nt(th, th).astype(BF16) * masks_ref[l]
    return a


def _lane_column(row_vec):
    return jnp.transpose(row_vec, (1, 0))


def _gla_out(x, o, r, gn_ref, wo_ref):
    dv = o.shape[-1]
    dvh = dv // GLA_HEADS
    normed = []
    for hd in range(GLA_HEADS):
        oh = o[:, hd * dvh:(hd + 1) * dvh]
        normed.append(oh * lax.rsqrt(jnp.mean(oh * oh, axis=-1, keepdims=True) + EPS))
    o = jnp.concatenate(normed, axis=-1) * gn_ref[...]
    o = (o * _silu(r)).astype(BF16)
    return x + _dot(o, wo_ref[...])


def _gla_prompt_steps(x, states, g_ref, win_ref, wa2_ref, ba_ref, gn_ref, wo_ref,
                      tri_ref, masks_ref, result, *, levels, chunk):
    rows = x.shape[0]
    rank, dk = wa2_ref.shape
    dkh = dk // GLA_HEADS
    h = _rmsnorm(x, g_ref[...]).astype(BF16)
    a_low = _dot(h, win_ref[:, 6 * dk:6 * dk + rank])
    yield 150
    q = _dot(h, win_ref[:, 0:dk]) * (dkh ** -0.5)
    q16 = q.astype(BF16)
    yield 512
    xg = _dot(a_low.astype(BF16), wa2_ref[...]) + ba_ref[...]
    yield 150
    k = _dot(h, win_ref[:, dk:2 * dk])
    k16 = k.astype(BF16)
    yield 512
    la = -(jnp.maximum(-xg, 0.0) + jnp.log(1.0 + jnp.exp(-jnp.abs(xg)))) * (1.0 / GLA_GATE_TEMP)
    hi, mid, lo = _split3(la)
    yield 400
    vb = _dot(h, win_ref[:, 2 * dk:4 * dk]).astype(BF16)
    yield 1024
    dvh = vb.shape[-1] // GLA_HEADS
    tri = tri_ref[...]
    b = _dot(tri, hi) + _dot(tri, mid) + _dot(tri, lo)
    yield 350
    r = _dot(h, win_ref[:, 4 * dk:6 * dk])
    yield 1024
    ops = []
    yield from _gla_level_operand_steps(q16, k16, b, levels, ops)
    qb = (q * jnp.exp(b)).astype(BF16)
    yield 60

    states = list(states)
    pieces = [(ci, hd) for ci in range(rows // chunk) for hd in range(GLA_HEADS)]
    kends = {}
    outs = {}

    def scores(ci, hd):
        rs = slice(ci * chunk, (ci + 1) * chunk)
        hs = slice(hd * dkh, (hd + 1) * dkh)
        return _gla_intra(q16[rs, hs], k16[rs, hs], ops, masks_ref, rs, hs)

    def finish(ci, hd, a16):
        rs = slice(ci * chunk, (ci + 1) * chunk)
        hs = slice(hd * dkh, (hd + 1) * dkh)
        vs = slice(hd * dvh, (hd + 1) * dvh)
        b_c = b[rs]
        b_last = b_c[chunk - 1:chunk, :]
        if ci not in kends:
            kends[ci] = (k[rs] * jnp.exp(b_last - b_c)).astype(BF16)
        s_old = states[hd]
        outs[ci, hd] = _dot(a16, vb[rs, vs]) + _dot(qb[rs, hs], s_old.astype(BF16))
        dec = jnp.exp(_lane_column(b_last[:, hs]))
        states[hd] = dec * s_old + _dot_tn(kends[ci][:, hs], vb[rs, vs])

    pending = []
    for piece in pieces:
        pending.append((*piece, scores(*piece)))
        yield GLA_SCORE_COST
        if len(pending) > GLA_SCORE_LEAD:
            finish(*pending.pop(0))
            yield GLA_FINISH_COST
    while pending:
        finish(*pending.pop(0))
        yield GLA_FINISH_COST
    o = jnp.concatenate(
        [jnp.concatenate([outs[ci, hd] for hd in range(GLA_HEADS)], axis=-1) for ci in range(rows // chunk)],
        axis=0)
    result["z"] = _gla_out(x, o, r, gn_ref, wo_ref)
    result["states"] = states
    yield 1700


def _ffn_steps(x, g_ref, w1_ref, w3_ref, w2_ref, gf_ref, result, *, final, col_chunks):
    rows = x.shape[0]
    h = _rmsnorm(x, g_ref[...]).astype(BF16)
    yield 200
    acc = x
    lo = 0
    pending = None
    for cols in col_chunks:
        cs = slice(lo, lo + cols)
        lo += cols
        u1 = _dot(h, w1_ref[:, cs])
        yield rows * cols // 256
        u3 = _dot(h, w3_ref[:, cs])
        yield rows * cols // 256
        if pending is not None:
            acc = acc + _dot(pending[0], w2_ref[pending[1], :])
            yield rows * (pending[1].stop - pending[1].start) // 256
        pending = ((_silu(u1) * u3).astype(BF16), cs)
    acc = acc + _dot(pending[0], w2_ref[pending[1], :])
    yield rows * (pending[1].stop - pending[1].start) // 256
    if final:
        acc = _rmsnorm(acc, gf_ref[...])
    result["out"] = acc


def _ffn_weight_chunks(layer, wf_refs, ws_refs, stage_refs, sem_refs):
    chunks = []
    used = {}
    for wf, ws, stage, sem in zip(wf_refs, ws_refs, stage_refs, sem_refs):
        n_rows = ws.shape[0] // FFN_WEIGHT_CHUNKS
        for i in range(FFN_WEIGHT_CHUNKS):
            rs = pl.ds(i * n_rows, n_rows)
            slot = used.get(id(stage), 0) % FFN_STAGE_SLOTS
            used[id(stage)] = used.get(id(stage), 0) + 1
            chunks.append((wf.at[layer, rs, :], stage.at[slot], sem.at[slot], ws.at[rs, :]))
    return chunks


def _fetch_steps(chunks):
    ahead = FFN_STAGE_SLOTS - 1
    copies = [pltpu.make_async_copy(src, stage, sem) for src, stage, sem, _ in chunks]
    for i in range(min(ahead, len(chunks))):
        copies[i].start()
    yield 0.01
    for i, (_, stage, _, dst) in enumerate(chunks):
        if i + ahead < len(chunks):
            copies[i + ahead].start()
        copies[i].wait()
        dst[...] = stage[...].astype(BF16)
        yield 1


def _interleave(*pairs):
    gens = [p[0] for p in pairs]
    totals = [float(p[1]) for p in pairs]
    done = [0.0] * len(gens)
    live = list(range(len(gens)))
    while live:
        i = min(live, key=lambda j: done[j] / totals[j])
        try:
            done[i] += next(gens[i])
        except StopIteration:
            live.remove(i)


def _gla_sample_body(x_ref, s0_ref, g_ref, win_ref, wa2_ref, ba_ref, gn_ref, wo_ref,
                     tri_ref, masks_ref, o_ref, sout_ref, *, levels, seqs, steps):
    x = x_ref[...]
    rows = x.shape[0]
    q, k, v, r, b = _gla_front(x, g_ref, win_ref, wa2_ref, ba_ref, tri_ref)
    dk = q.shape[-1]
    dkh = dk // GLA_HEADS
    dvh = v.shape[-1] // GLA_HEADS
    q16 = q.astype(BF16)
    k16 = k.astype(BF16)
    ops = _gla_level_operands(q16, k16, b, levels)
    b3 = b.reshape(seqs, steps, dk)
    b_last = b3[:, steps - 1:steps, :]
    qb = (q * jnp.exp(b)).astype(BF16)
    kend = (k.reshape(seqs, steps, dk) * jnp.exp(b_last - b3)).reshape(rows, dk).astype(BF16)
    vb = v.astype(BF16)
    all_rows = slice(0, rows)
    outs = []
    for hd in range(GLA_HEADS):
        hs = slice(hd * dkh, (hd + 1) * dkh)
        vs = slice(hd * dvh, (hd + 1) * dvh)
        a = _gla_intra(q16[:, hs], k16[:, hs], ops, masks_ref, all_rows, hs)
        o_h = _dot(a, vb[:, vs])
        o_state = []
        for s in range(seqs):
            rs = slice(s * steps, (s + 1) * steps)
            s_old = s0_ref[s, hd]
            o_state.append(_dot(qb[rs, hs], s_old.astype(BF16)))
            dec = jnp.exp(_lane_column(b[(s + 1) * steps - 1:(s + 1) * steps, hs]))
            sout_ref[s, hd] = dec * s_old + _dot_tn(kend[rs, hs], vb[rs, vs])
        outs.append(o_h + jnp.concatenate(o_state, axis=0))
    o_ref[...] = _gla_out(x, jnp.concatenate(outs, axis=-1), r, gn_ref, wo_ref)


def _gla_sample(x, state_gla, layer, gla_consts, prev, *, steps):
    n, d = x.shape
    seqs = GLA_SEQS if prev is None else 2 * GLA_SEQS
    rows = seqs * steps
    bs = n // steps
    assert bs % seqs == 0
    win, wa2, wo = gla_consts[1], gla_consts[2], gla_consts[5]
    dk = wa2.shape[1]
    dkh, dvh = dk // GLA_HEADS, wo.shape[0] // GLA_HEADS
    tri, masks, levels = _gla_tables(rows, steps, rows)
    consts = tuple(gla_consts) + (_Const(tri), _Const(masks))
    state_block = seqs * GLA_HEADS * dkh * dvh * 4
    resident = _nbytes(consts) + 4 * rows * d * 4 + 4 * state_block
    resident += rows * (win.shape[1] * 4 + (levels + 4) * dk * 4)
    return _call_with_state_slot(
        functools.partial(_gla_sample_body, levels=levels, seqs=seqs, steps=steps),
        (x, state_gla) + _arrays(consts),
        [pl.BlockSpec((rows, d), lambda i: (i, 0)),
         pl.BlockSpec((None, seqs, GLA_HEADS, dkh, dvh), lambda i: (layer, i, 0, 0, 0))]
        + _specs(consts),
        grid=(bs // seqs,),
        out_shapes=[jax.ShapeDtypeStruct((n, d), F32)],
        out_specs=[pl.BlockSpec((rows, d), lambda i: (i, 0))],
        state_shape=(bs, GLA_HEADS, dkh, dvh), state_block=(seqs, GLA_HEADS, dkh, dvh),
        state_index=lambda i: (i, 0, 0, 0),
        slot=layer, n_slots=state_gla.shape[0], prev=prev,
        scratch_shapes=[], resident=resident, name="gla_sample")


def _ffn_tail_steps(s, n_total, z_ref, xs_ref, o_ref, os_ref, ffn_steps):
    @pl.when(s >= n_total)
    def _():
        ffn = {}
        _interleave(ffn_steps(ffn, jnp.where(s == n_total, z_ref[...], xs_ref[...])))

        @pl.when(s == n_total)
        def _():
            o_ref[...] = ffn["out"]

        @pl.when(s > n_total)
        def _():
            os_ref[...] = ffn["out"]


def _pool_ffn_body(x_ref, xs_ref, gm_ref, wp_ref, sc_ref, gfn_ref, gf_ref, w1f_ref, w3f_ref, w2f_ref,
                   o_ref, os_ref, ctx_ref, ext_ref, z_ref, w1_ref, w3_ref, w2_ref, st13_ref, st2_ref, sem13, sem2,
                   *, rows, n_tiles, n_total, final, ffn_layer):
    s = pl.program_id(0)
    d = x_ref.shape[-1]
    t = jnp.minimum(s, n_total - 1) % n_tiles

    def fetch_steps(late):
        chunks = _ffn_weight_chunks(ffn_layer, (w1f_ref, w3f_ref, w2f_ref), (w1_ref, w3_ref, w2_ref),
                                    (st13_ref, st13_ref, st2_ref), (sem13, sem13, sem2))
        chunks = chunks[2 * FFN_WEIGHT_CHUNKS:] if late else chunks[:2 * FFN_WEIGHT_CHUNKS]
        return (_fetch_steps(chunks), len(chunks) * (3 if late else 1))

    def ffn_steps(ffn, src=None):
        return (_ffn_steps(z_ref[...] if src is None else src, gfn_ref, w1_ref, w3_ref, w2_ref, gf_ref, ffn,
                           final=final, col_chunks=FFN_COL_CHUNKS), FFN_STEP_COST * rows // 256)

    def mixer_step(with_ffn, late_fetch=False):
        @pl.when(t == 0)
        def _():
            ext_ref[0:POOL_MAXW, :] = jnp.zeros((POOL_MAXW, d), F32)

        x = x_ref[...]
        h = _rmsnorm(x, gm_ref[...])
        ext_ref[POOL_MAXW:POOL_MAXW + rows, :] = h
        pos = t * rows + lax.broadcasted_iota(jnp.int32, (rows, 1), 0)

        def ext_read(k, cs):
            return ext_ref[POOL_MAXW - k:POOL_MAXW - k + rows, cs]

        mix, ffn = {}, {}
        mixer = (_pool_group_steps(ext_read, h, pos, wp_ref, sc_ref[...], mix), POOL_STEP_COST * rows // 256)
        if with_ffn:
            streams = ([fetch_steps(True)] if late_fetch else []) + [mixer, ffn_steps(ffn)]
        else:
            streams = [fetch_steps(False), mixer]
        _interleave(*streams)
        z_ref[...] = x + mix["y"]
        if with_ffn:
            o_ref[...] = ffn["out"]
        ext_ref[0:POOL_MAXW, :] = ext_ref[rows:rows + POOL_MAXW, :]

        @pl.when(t == n_tiles - 1)
        def _():
            ctx_ref[...] = ext_ref[rows + 1:rows + POOL_MAXW, :]

    @pl.when(s == 0)
    def _():
        mixer_step(False)

    @pl.when(s == 1)
    def _():
        mixer_step(True, late_fetch=True)

    @pl.when(jnp.logical_and(s > 1, s < n_total))
    def _():
        mixer_step(True)

    _ffn_tail_steps(s, n_total, z_ref, xs_ref, o_ref, os_ref, ffn_steps)


def _gla_ffn_body(x_ref, xs_ref, gm_ref, win_ref, wa2_ref, ba_ref, gn_ref, wo_ref, tri_ref, masks_ref,
                  gfn_ref, gf_ref, w1f_ref, w3f_ref, w2f_ref,
                  o_ref, os_ref, sout_ref, s_ref, z_ref, w1_ref, w3_ref, w2_ref, st13_ref, st2_ref, sem13, sem2,
                  *, levels, chunk, n_tiles, n_total, final, ffn_layer):
    s = pl.program_id(0)
    t = jnp.minimum(s, n_total - 1) % n_tiles

    def fetch_steps(late):
        chunks = _ffn_weight_chunks(ffn_layer, (w1f_ref, w3f_ref, w2f_ref), (w1_ref, w3_ref, w2_ref),
                                    (st13_ref, st13_ref, st2_ref), (sem13, sem13, sem2))
        chunks = chunks[2 * FFN_WEIGHT_CHUNKS:] if late else chunks[:2 * FFN_WEIGHT_CHUNKS]
        return (_fetch_steps(chunks), len(chunks) * (3 if late else 1))

    def ffn_steps(ffn, src=None):
        return (_ffn_steps(z_ref[...] if src is None else src, gfn_ref, w1_ref, w3_ref, w2_ref, gf_ref, ffn,
                           final=final, col_chunks=FFN_COL_CHUNKS), FFN_STEP_COST)

    def mixer_step(with_ffn, late_fetch=False):
        @pl.when(t == 0)
        def _():
            s_ref[...] = jnp.zeros(s_ref.shape, F32)

        states = [s_ref[hd] for hd in range(GLA_HEADS)]
        mix, ffn = {}, {}
        mixer = (_gla_prompt_steps(x_ref[...], states, gm_ref, win_ref, wa2_ref, ba_ref, gn_ref,
                                   wo_ref, tri_ref, masks_ref, mix, levels=levels, chunk=chunk), GLA_STEP_COST)
        if with_ffn:
            streams = ([fetch_steps(True)] if late_fetch else []) + [mixer, ffn_steps(ffn)]
        else:
            streams = [fetch_steps(False), mixer]
        _interleave(*streams)
        states = mix["states"]
        z_ref[...] = mix["z"]
        if with_ffn:
            o_ref[...] = ffn["out"]
        for hd in range(GLA_HEADS):
            s_ref[hd] = states[hd]

        @pl.when(t == n_tiles - 1)
        def _():
            for hd in range(GLA_HEADS):
                sout_ref[hd] = states[hd]

    @pl.when(s == 0)
    def _():
        mixer_step(False)

    @pl.when(s == 1)
    def _():
        mixer_step(True, late_fetch=True)

    @pl.when(jnp.logical_and(s > 1, s < n_total))
    def _():
        mixer_step(True)

    _ffn_tail_steps(s, n_total, z_ref, xs_ref, o_ref, os_ref, ffn_steps)


def _prompt_layer(kind, x, xs, mixer_consts, ffn_consts, *, batch, seq, final, slot, n_slots, prev):
    n, d = x.shape
    rows = PROMPT_ROWS[kind]
    n_tiles = seq // rows
    n_total = batch * n_tiles
    assert seq % rows == 0 and rows >= POOL_MAXW and rows % GLA_CHUNK == 0
    g_ffn, w1f, w3f, w2f, gf = ffn_consts
    dff = w1f.shape[1]
    ffn_vmem = (g_ffn, gf)
    stage13 = (FFN_STAGE_SLOTS, d // FFN_WEIGHT_CHUNKS, dff)
    stage2 = (FFN_STAGE_SLOTS, dff // FFN_WEIGHT_CHUNKS, d)
    assert d % FFN_WEIGHT_CHUNKS == 0 and (dff // FFN_WEIGHT_CHUNKS) % (2 * V7X_SUBLANES) == 0
    ffn_scratch = [pltpu.VMEM((d, dff), BF16), pltpu.VMEM((d, dff), BF16), pltpu.VMEM((dff, d), BF16),
                   pltpu.VMEM(stage13, F32), pltpu.VMEM(stage2, F32),
                   pltpu.SemaphoreType.DMA((FFN_STAGE_SLOTS,)), pltpu.SemaphoreType.DMA((FFN_STAGE_SLOTS,))]
    resident = _nbytes(mixer_consts) + 3 * d * dff * 2 + 4 * int(np.prod(stage13)) * 2
    resident += 6 * rows * d * 4 + 3 * rows * dff * 4
    ns = xs.shape[0]
    assert ns % rows == 0
    n_sample = ns // rows
    x_spec = pl.BlockSpec((rows, d), lambda s: (jnp.minimum(s, n_total - 1), 0))
    o_spec = pl.BlockSpec((rows, d), lambda s: (jnp.clip(s - 1, 0, n_total - 1), 0))
    xs_spec = pl.BlockSpec((rows, d), lambda s: (jnp.clip(s - n_total - 1, 0, n_sample - 1), 0))
    seq_of = lambda s: jnp.minimum(s, n_total - 1) // n_tiles
    carry = [pltpu.VMEM((rows, d), F32)]
    if kind == "pool":
        body = functools.partial(_pool_ffn_body, rows=rows, n_tiles=n_tiles, n_total=n_total, final=final,
                                 ffn_layer=w1f.layer)
        state_shape = (batch, POOL_BUF, d)
        state_block = (None, POOL_BUF, d)
        state_index = lambda s: (seq_of(s), 0, 0)
        scratch = [pltpu.VMEM((rows + POOL_MAXW, d), F32)] + carry
        resident += (rows + POOL_MAXW) * d * 4
        consts = tuple(mixer_consts) + ffn_vmem
    else:
        win, wa2, wo = mixer_consts[1], mixer_consts[2], mixer_consts[5]
        dk = wa2.shape[1]
        dkh, dvh = dk // GLA_HEADS, wo.shape[0] // GLA_HEADS
        tri, masks, levels = _gla_tables(rows, GLA_CHUNK, GLA_CHUNK)
        body = functools.partial(_gla_ffn_body, levels=levels, chunk=GLA_CHUNK, n_tiles=n_tiles,
                                 n_total=n_total, final=final, ffn_layer=w1f.layer)
        state_shape = (batch, GLA_HEADS, dkh, dvh)
        state_block = (None, GLA_HEADS, dkh, dvh)
        state_index = lambda s: (seq_of(s), 0, 0, 0)
        scratch = [pltpu.VMEM((GLA_HEADS, dkh, dvh), F32)] + carry
        tables = (_Const(tri), _Const(masks))
        resident += _nbytes(tables) + 3 * GLA_HEADS * dkh * dvh * 4
        resident += rows * (win.shape[1] * 4 + (levels + 4) * dk * 4)
        consts = tuple(mixer_consts) + tables + ffn_vmem
    hbm = pl.BlockSpec(memory_space=pl.ANY)
    return _call_with_state_slot(
        body, (x, xs) + _arrays(consts) + (w1f.array, w3f.array, w2f.array),
        [x_spec, xs_spec] + _specs(consts) + [hbm] * 3,
        grid=(n_total + 1 + n_sample,),
        out_shapes=[jax.ShapeDtypeStruct((n, d), F32), jax.ShapeDtypeStruct((ns, d), F32)],
        out_specs=[o_spec, xs_spec],
        state_shape=state_shape, state_block=state_block, state_index=state_index,
        slot=slot, n_slots=n_slots, prev=prev,
        scratch_shapes=scratch + ffn_scratch, resident=resident,
        name=kind + ("_ffn_final" if final else "_ffn") + "_prompt")


def kernel(x_prompt, x_sample, state_pool, state_gla, norm_mix, norm_ffn, norm_final, pool_w, pool_scale,
           gla_w_in, gla_w_a2, gla_b_a, gla_norm, gla_w_o, ffn_w1, ffn_w3, ffn_w2):
    batch, seq, d = x_prompt.shape
    bs, steps, _ = x_sample.shape
    depth = norm_mix.shape[0]

    xp = x_prompt.reshape(batch * seq, d)
    xs = x_sample.reshape(bs * steps, d)

    rows_of = lambda a: a.reshape(a.shape[0], 1, -1)
    w_pool, w_in, w_a2, w_o = (a.astype(BF16) for a in (pool_w, gla_w_in, gla_w_a2, gla_w_o))
    g_mix, g_ffn, sc_pool, b_a, g_gla = (rows_of(a) for a in (norm_mix, norm_ffn, pool_scale, gla_b_a, gla_norm))
    gf = _Const(norm_final.reshape(1, -1))

    n_pool, n_gla = state_pool.shape[0], state_gla.shape[0]
    state_pool_t = jnp.swapaxes(state_pool, 1, 2)
    new_pool_p = new_pool_s = new_gla_p = new_gla_s = None
    for i in range(depth):
        j = i // 2
        final = i == depth - 1
        g = _Const(g_mix, i)
        ffn_consts = (_Const(g_ffn, i), _Const(ffn_w1, i), _Const(ffn_w3, i), _Const(ffn_w2, i), gf)
        if i % 2 == 0:
            pool_consts = (g, _Const(w_pool, j), _Const(sc_pool, j))
            xs3, new_pool_s = _pool_sample(xs.reshape(bs, steps, d), state_pool_t, j, pool_consts, new_pool_s)
            xp, xs, new_pool_p = _prompt_layer("pool", xp, xs3.reshape(bs * steps, d), pool_consts, ffn_consts,
                                               batch=batch, seq=seq, final=final, slot=j, n_slots=n_pool,
                                               prev=new_pool_p)
        else:
            gla_consts = (g, _Const(w_in, j), _Const(w_a2, j), _Const(b_a, j), _Const(g_gla, j), _Const(w_o, j))
            xs, new_gla_s = _gla_sample(xs, state_gla, j, gla_consts, new_gla_s, steps=steps)
            xp, xs, new_gla_p = _prompt_layer("gla", xp, xs, gla_consts, ffn_consts, batch=batch, seq=seq,
                                              final=final, slot=j, n_slots=n_gla, prev=new_gla_p)

    return (xp.reshape(batch, seq, d), xs.reshape(bs, steps, d),
            new_pool_p, new_gla_p, jnp.swapaxes(new_pool_s, 1, 2), new_gla_s)
```

```python
import functools

import numpy as np
import jax
import jax.numpy as jnp
from jax import lax
from jax.experimental import pallas as pl
from jax.experimental.pallas import tpu as pltpu

F32 = jnp.float32
BF16 = jnp.bfloat16

PAST_LEN = 16384
POOL_WINDOWS = (2, 4, 8, 16)
POOL_MAXW = max(POOL_WINDOWS)
POOL_BUF = POOL_MAXW - 1
GLA_HEADS = 4
GLA_GATE_TEMP = 16.0
EPS = 1e-6
LOG2_E = 1.4426950408889634

V7X_SUBLANES = 8
V7X_VMEM_BYTES = 64 * 1024 * 1024

PROMPT_ROWS = {"pool": 256, "gla": 256}
GLA_CHUNK = 128
FFN_WEIGHT_CHUNKS = 8
FFN_STAGE_SLOTS = 4
POOL_SEQS = 32
GLA_SEQS = 8
GLA_STATE_IN_BUFFERS = 3
FFN_COL_CHUNKS = (1024, 1024, 768)
GLA_LEVEL_COST = 300
GLA_SCORE_COST = 300
GLA_FINISH_COST = 200
GLA_SCORE_LEAD = 2
FFN_STEP_COST = 8648
GLA_STEP_COST = 5882 + 7 * GLA_LEVEL_COST + 8 * (GLA_SCORE_COST + GLA_FINISH_COST)
POOL_STEP_COST = 2220


def _vmem_limit(resident_bytes):
    return int(min(V7X_VMEM_BYTES - (4 << 20), 2 * resident_bytes + (16 << 20)))


class _Const:
    def __init__(self, array, layer=None):
        self.array = array
        self.layer = layer
        self.shape = tuple(array.shape if layer is None else array.shape[1:])
        self.nbytes = int(np.prod(self.shape)) * array.dtype.itemsize

    def spec(self):
        nd = len(self.shape)
        if self.layer is None:
            return pl.BlockSpec(self.shape, lambda *_: (0,) * nd, pipeline_mode=pl.Buffered(1))
        layer = self.layer
        return pl.BlockSpec((None,) + self.shape, lambda *_: (layer,) + (0,) * nd, pipeline_mode=pl.Buffered(1))


def _nbytes(consts):
    return sum(c.nbytes for c in consts)


def _arrays(consts):
    return tuple(c.array for c in consts)


def _specs(consts):
    return [c.spec() for c in consts]


def _call_with_state_slot(body, args, in_specs, *, grid, out_shapes, out_specs, state_shape, state_block,
                          state_index, slot, n_slots, prev, scratch_shapes, resident, name):
    n_args = len(args)
    n_outs = len(out_shapes)
    inner = body
    aliases = {}
    if prev is None:
        state_spec = pl.BlockSpec((n_slots,) + tuple(state_block), lambda *g: (0,) + tuple(state_index(*g)))
        resident += 2 * (n_slots - 1) * int(np.prod([b for b in state_block if b is not None])) * 4

        def body(*refs):
            refs = list(refs)
            full = refs[n_args + n_outs]
            for other in range(n_slots):
                if other != slot:
                    full[other] = jnp.zeros(full.shape[1:], F32)
            refs[n_args + n_outs] = full.at[slot]
            return inner(*refs)
    else:
        state_spec = pl.BlockSpec((None,) + tuple(state_block), lambda *g: (slot,) + tuple(state_index(*g)))

        def body(*refs):
            return inner(*refs[:n_args], *refs[n_args + 1:])

        args = tuple(args) + (prev,)
        in_specs = list(in_specs) + [pl.BlockSpec(memory_space=pl.ANY)]
        aliases = {n_args: n_outs}
    return pl.pallas_call(
        body,
        out_shape=tuple(out_shapes) + (jax.ShapeDtypeStruct((n_slots,) + tuple(state_shape), F32),),
        grid=grid,
        in_specs=in_specs,
        out_specs=tuple(out_specs) + (state_spec,),
        scratch_shapes=scratch_shapes,
        input_output_aliases=aliases,
        compiler_params=pltpu.CompilerParams(
            dimension_semantics=("arbitrary",) * len(grid), vmem_limit_bytes=_vmem_limit(resident)),
        name=name,
    )(*args)


def _rmsnorm(x, g):
    ms = jnp.mean(x * x, axis=-1, keepdims=True)
    return x * lax.rsqrt(ms + EPS) * g


def _silu(x):
    hx = 0.5 * x
    return hx + hx * jnp.tanh(hx)


def _dot(a, b):
    return jnp.dot(a, b, preferred_element_type=F32)


def _dot_nt(a, b):
    return lax.dot_general(a, b, (((1,), (1,)), ((), ())), preferred_element_type=F32)


def _dot_tn(a, b):
    return lax.dot_general(a, b, (((0,), (0,)), ((), ())), preferred_element_type=F32)


def _split3(x):
    hi = x.astype(BF16)
    r1 = x - hi.astype(F32)
    mid = r1.astype(BF16)
    lo = (r1 - mid.astype(F32)).astype(BF16)
    return hi, mid, lo


def _pool_group_steps(ext_read, h, pos, w_ref, sc, result):
    gw = h.shape[-1] // len(POOL_WINDOWS)
    ys = []
    for g, w in enumerate(POOL_WINDOWS):
        cs = slice(g * gw, (g + 1) * gw)
        hg = h[:, cs]
        s = hg
        for k in range(1, w):
            s = s + ext_read(k, cs)
        cnt = jnp.minimum(pos + 1, w).astype(F32)
        diff = (s / cnt - hg).astype(BF16)
        ys.append(_dot(diff, w_ref[g]))
        yield (100 + 70 * (w - 1)) * h.shape[0] // 256
    result["y"] = jnp.concatenate(ys, axis=-1) * sc


def _pool_groups(ext_read, h, pos, w_ref, sc):
    result = {}
    for _ in _pool_group_steps(ext_read, h, pos, w_ref, sc, result):
        pass
    return result["y"]


def _pool_sample_body(x_ref, ctx_ref, g_ref, w_ref, sc_ref, o_ref, nctx_ref, ext_ref, *, seqs, steps):
    d = x_ref.shape[-1]
    ext_ref[0] = jnp.zeros((seqs, d), F32)
    ext_ref[1:POOL_MAXW] = ctx_ref[...]
    xs = [x_ref[:, t, :] for t in range(steps)]
    for t in range(steps):
        ext_ref[POOL_MAXW + t] = _rmsnorm(xs[t], g_ref[...])
    h = ext_ref[POOL_MAXW:POOL_MAXW + steps].reshape(steps * seqs, d)
    pos = PAST_LEN + lax.broadcasted_iota(jnp.int32, (steps * seqs, 1), 0) // seqs

    def ext_read(k, cs):
        return ext_ref[POOL_MAXW - k:POOL_MAXW - k + steps, :, cs].reshape(steps * seqs, cs.stop - cs.start)

    y = _pool_groups(ext_read, h, pos, w_ref, sc_ref[...])
    for t in range(steps):
        o_ref[:, t, :] = xs[t] + y[t * seqs:(t + 1) * seqs]
    nctx_ref[...] = ext_ref[steps + 1:steps + POOL_MAXW]


def _pool_sample(x, state_pool_t, layer, pool_consts, prev):
    bs, steps, d = x.shape
    seqs = POOL_SEQS
    assert bs % seqs == 0 and steps == V7X_SUBLANES
    resident = 2 * seqs * (2 * steps + 2 * POOL_MAXW) * d * 4 + seqs * (steps + POOL_MAXW) * d * 4
    return _call_with_state_slot(
        functools.partial(_pool_sample_body, seqs=seqs, steps=steps),
        (x, state_pool_t) + _arrays(pool_consts),
        [pl.BlockSpec((seqs, steps, d), lambda i: (i, 0, 0)),
         pl.BlockSpec((None, POOL_BUF, seqs, d), lambda i: (layer, 0, i, 0))]
        + _specs(pool_consts),
        grid=(bs // seqs,),
        out_shapes=[jax.ShapeDtypeStruct((bs, steps, d), F32)],
        out_specs=[pl.BlockSpec((seqs, steps, d), lambda i: (i, 0, 0))],
        state_shape=(POOL_BUF, bs, d), state_block=(POOL_BUF, seqs, d), state_index=lambda i: (0, i, 0),
        slot=layer, n_slots=state_pool_t.shape[0], prev=prev,
        scratch_shapes=[pltpu.VMEM((steps + POOL_MAXW, seqs, d), F32)],
        resident=resident, name="pool_sample")


def _gla_tables(rows, seg, score_rows):
    levels = int(np.log2(seg))
    assert 2 ** levels == seg and rows % score_rows == 0 and score_rows % seg == 0
    idx = np.arange(rows)
    tri = (idx[None, :] >= (idx[:, None] // seg) * seg) & (idx[None, :] <= idx[:, None])
    row = np.arange(score_rows)[:, None]
    col = np.arange(score_rows)[None, :]
    masks = [row == col]
    for l in range(1, levels + 1):
        s = 2 ** l
        mid = (row // s) * s + s // 2 - 1
        masks.append((row // s == col // s) & (row > mid) & (col <= mid))
    masks = np.stack(masks, axis=0).astype(np.float32)
    return jnp.asarray(tri.astype(np.float32), BF16), jnp.asarray(masks, BF16), levels


def _gla_front(x, g_ref, win_ref, wa2_ref, ba_ref, tri_ref):
    rank, dk = wa2_ref.shape
    dkh = dk // GLA_HEADS
    h = _rmsnorm(x, g_ref[...]).astype(BF16)
    a_low = _dot(h, win_ref[:, 6 * dk:6 * dk + rank])
    xg = _dot(a_low.astype(BF16), wa2_ref[...]) + ba_ref[...]
    la = -(jnp.maximum(-xg, 0.0) + jnp.log(1.0 + jnp.exp(-jnp.abs(xg)))) * (1.0 / GLA_GATE_TEMP)
    hi, mid, lo = _split3(la)
    tri = tri_ref[...]
    b = _dot(tri, hi) + _dot(tri, mid) + _dot(tri, lo)
    proj = _dot(h, win_ref[:, 0:6 * dk])
    q = proj[:, 0:dk] * (dkh ** -0.5)
    k = proj[:, dk:2 * dk]
    v = proj[:, 2 * dk:4 * dk]
    r = proj[:, 4 * dk:6 * dk]
    return q, k, v, r, b


def _gla_level_operand_steps(q16, k16, b, levels, ops):
    rows, dk = b.shape
    row = lax.broadcasted_iota(jnp.int32, (rows, 1), 0)
    sub = lax.broadcasted_iota(jnp.int32, (1, V7X_SUBLANES, 1), 1)
    b = b * LOG2_E
    for l in range(1, levels + 1):
        s = 2 ** l
        if s >= 2 * V7X_SUBLANES:
            b4 = b.reshape(rows // s, 2, s // 2, dk)
            m = b4[:, 0:1, s // 2 - 1:s // 2, :]
            x = jnp.concatenate([m - b4[:, 0:1], b4[:, 1:2] - m], axis=1)
        else:
            b3 = b.reshape(rows // V7X_SUBLANES, V7X_SUBLANES, dk)
            picks = [b3[:, j + s // 2 - 1:j + s // 2, :] for j in range(0, V7X_SUBLANES, s)]
            m = picks[-1]
            for idx in range(len(picks) - 2, -1, -1):
                m = jnp.where(sub < (idx + 1) * s, picks[idx], m)
            x = -jnp.abs(b3 - m)
        e = jnp.exp2(x).reshape(rows, dk).astype(BF16)
        second = ((row >> (l - 1)) & 1) == 1
        ops.append(jnp.where(second, q16, k16) * e)
        yield GLA_LEVEL_COST


def _gla_level_operands(q16, k16, b, levels):
    ops = []
    for _ in _gla_level_operand_steps(q16, k16, b, levels, ops):
        pass
    return ops


def _gla_intra(qb16_h, kb16_h, ops, masks_ref, rs, hs):
    a = _dot_nt(qb16_h, kb16_h).astype(BF16) * masks_ref[0]
    for l, t in enumerate(ops, start=1):
        th = t[rs, hs]
        a = a + _dot_nt(th, th).astype(BF16) * masks_ref[l]
    return a


def _lane_column(row_vec):
    return jnp.transpose(row_vec, (1, 0))


def _gla_out(x, o, r, gn_ref, wo_ref):
    dv = o.shape[-1]
    dvh = dv // GLA_HEADS
    normed = []
    for hd in range(GLA_HEADS):
        oh = o[:, hd * dvh:(hd + 1) * dvh]
        normed.append(oh * lax.rsqrt(jnp.mean(oh * oh, axis=-1, keepdims=True) + EPS))
    o = jnp.concatenate(normed, axis=-1) * gn_ref[...]
    o = (o * _silu(r)).astype(BF16)
    return x + _dot(o, wo_ref[...])


def _gla_prompt_steps(x, states, g_ref, win_ref, wa2_ref, ba_ref, gn_ref, wo_ref,
                      tri_ref, masks_ref, result, *, levels, chunk):
    rows = x.shape[0]
    rank, dk = wa2_ref.shape
    dkh = dk // GLA_HEADS
    h = _rmsnorm(x, g_ref[...]).astype(BF16)
    a_low = _dot(h, win_ref[:, 6 * dk:6 * dk + rank])
    yield 150
    q = _dot(h, win_ref[:, 0:dk]) * (dkh ** -0.5)
    q16 = q.astype(BF16)
    yield 512
    xg = _dot(a_low.astype(BF16), wa2_ref[...]) + ba_ref[...]
    yield 150
    k = _dot(h, win_ref[:, dk:2 * dk])
    k16 = k.astype(BF16)
    yield 512
    la = -(jnp.maximum(-xg, 0.0) + jnp.log(1.0 + jnp.exp(-jnp.abs(xg)))) * (1.0 / GLA_GATE_TEMP)
    hi, mid, lo = _split3(la)
    yield 400
    vb = _dot(h, win_ref[:, 2 * dk:4 * dk]).astype(BF16)
    yield 1024
    dvh = vb.shape[-1] // GLA_HEADS
    tri = tri_ref[...]
    b = _dot(tri, hi) + _dot(tri, mid) + _dot(tri, lo)
    yield 350
    r = _dot(h, win_ref[:, 4 * dk:6 * dk])
    yield 1024
    ops = []
    yield from _gla_level_operand_steps(q16, k16, b, levels, ops)
    qb = (q * jnp.exp(b)).astype(BF16)
    yield 60

    states = list(states)
    pieces = [(ci, hd) for ci in range(rows // chunk) for hd in range(GLA_HEADS)]
    kends = {}
    outs = {}

    def scores(ci, hd):
        rs = slice(ci * chunk, (ci + 1) * chunk)
        hs = slice(hd * dkh, (hd + 1) * dkh)
        return _gla_intra(q16[rs, hs], k16[rs, hs], ops, masks_ref, rs, hs)

    def finish(ci, hd, a16):
        rs = slice(ci * chunk, (ci + 1) * chunk)
        hs = slice(hd * dkh, (hd + 1) * dkh)
        vs = slice(hd * dvh, (hd + 1) * dvh)
        b_c = b[rs]
        b_last = b_c[chunk - 1:chunk, :]
        if ci not in kends:
            kends[ci] = (k[rs] * jnp.exp(b_last - b_c)).astype(BF16)
        s_old = states[hd]
        outs[ci, hd] = _dot(a16, vb[rs, vs]) + _dot(qb[rs, hs], s_old.astype(BF16))
        dec = jnp.exp(_lane_column(b_last[:, hs]))
        states[hd] = dec * s_old + _dot_tn(kends[ci][:, hs], vb[rs, vs])

    pending = []
    for piece in pieces:
        pending.append((*piece, scores(*piece)))
        yield GLA_SCORE_COST
        if len(pending) > GLA_SCORE_LEAD:
            finish(*pending.pop(0))
            yield GLA_FINISH_COST
    while pending:
        finish(*pending.pop(0))
        yield GLA_FINISH_COST
    o = jnp.concatenate(
        [jnp.concatenate([outs[ci, hd] for hd in range(GLA_HEADS)], axis=-1) for ci in range(rows // chunk)],
        axis=0)
    result["z"] = _gla_out(x, o, r, gn_ref, wo_ref)
    result["states"] = states
    yield 1700


def _ffn_steps(x, g_ref, w1_ref, w3_ref, w2_ref, gf_ref, result, *, final, col_chunks):
    rows = x.shape[0]
    h = _rmsnorm(x, g_ref[...]).astype(BF16)
    yield 200
    acc = x
    lo = 0
    pending = None
    for cols in col_chunks:
        cs = slice(lo, lo + cols)
        lo += cols
        u1 = _dot(h, w1_ref[:, cs])
        yield rows * cols // 256
        u3 = _dot(h, w3_ref[:, cs])
        yield rows * cols // 256
        if pending is not None:
            acc = acc + _dot(pending[0], w2_ref[pending[1], :])
            yield rows * (pending[1].stop - pending[1].start) // 256
        pending = ((_silu(u1) * u3).astype(BF16), cs)
    acc = acc + _dot(pending[0], w2_ref[pending[1], :])
    yield rows * (pending[1].stop - pending[1].start) // 256
    if final:
        acc = _rmsnorm(acc, gf_ref[...])
    result["out"] = acc


def _ffn_weight_chunks(layer, wf_refs, ws_refs, stage_refs, sem_refs):
    chunks = []
    used = {}
    for wf, ws, stage, sem in zip(wf_refs, ws_refs, stage_refs, sem_refs):
        n_rows = ws.shape[0] // FFN_WEIGHT_CHUNKS
        for i in range(FFN_WEIGHT_CHUNKS):
            rs = pl.ds(i * n_rows, n_rows)
            slot = used.get(id(stage), 0) % FFN_STAGE_SLOTS
            used[id(stage)] = used.get(id(stage), 0) + 1
            chunks.append((wf.at[layer, rs, :], stage.at[slot], sem.at[slot], ws.at[rs, :]))
    return chunks


def _fetch_steps(chunks):
    ahead = FFN_STAGE_SLOTS - 1
    copies = [pltpu.make_async_copy(src, stage, sem) for src, stage, sem, _ in chunks]
    for i in range(min(ahead, len(chunks))):
        copies[i].start()
    yield 0.01
    for i, (_, stage, _, dst) in enumerate(chunks):
        if i + ahead < len(chunks):
            copies[i + ahead].start()
        copies[i].wait()
        dst[...] = stage[...].astype(BF16)
        yield 1


def _interleave(*pairs):
    gens = [p[0] for p in pairs]
    totals = [float(p[1]) for p in pairs]
    done = [0.0] * len(gens)
    live = list(range(len(gens)))
    while live:
        i = min(live, key=lambda j: done[j] / totals[j])
        try:
            done[i] += next(gens[i])
        except StopIteration:
            live.remove(i)


def _gla_sample_body(x_ref, sg_ref, g_ref, win_ref, wa2_ref, ba_ref, gn_ref, wo_ref,
                     tri_ref, masks_ref, o_ref, sout_ref, sbuf_ref, ssem, *, levels, seqs, steps, layer):
    i = pl.program_id(0)
    n_steps = pl.num_programs(0)
    depth = GLA_STATE_IN_BUFFERS

    def block_copy(j):
        slot = j % depth
        return pltpu.make_async_copy(sg_ref.at[layer, pl.ds(j * seqs, seqs)], sbuf_ref.at[slot], ssem.at[slot])

    @pl.when(i == 0)
    def _():
        for j in range(depth - 1):
            block_copy(j).start()

    @pl.when(i + (depth - 1) < n_steps)
    def _():
        block_copy(i + (depth - 1)).start()

    x = x_ref[...]
    rows = x.shape[0]
    q, k, v, r, b = _gla_front(x, g_ref, win_ref, wa2_ref, ba_ref, tri_ref)
    block_copy(i).wait()
    s0_ref = sbuf_ref.at[i % depth]
    dk = q.shape[-1]
    dkh = dk // GLA_HEADS
    dvh = v.shape[-1] // GLA_HEADS
    q16 = q.astype(BF16)
    k16 = k.astype(BF16)
    ops = _gla_level_operands(q16, k16, b, levels)
    b3 = b.reshape(seqs, steps, dk)
    b_last = b3[:, steps - 1:steps, :]
    qb = (q * jnp.exp(b)).astype(BF16)
    kend = (k.reshape(seqs, steps, dk) * jnp.exp(b_last - b3)).reshape(rows, dk).astype(BF16)
    vb = v.astype(BF16)
    all_rows = slice(0, rows)
    outs = []
    for hd in range(GLA_HEADS):
        hs = slice(hd * dkh, (hd + 1) * dkh)
        vs = slice(hd * dvh, (hd + 1) * dvh)
        a = _gla_intra(q16[:, hs], k16[:, hs], ops, masks_ref, all_rows, hs)
        o_h = _dot(a, vb[:, vs])
        o_state = []
        for s in range(seqs):
            rs = slice(s * steps, (s + 1) * steps)
            s_old = s0_ref[s, hd]
            o_state.append(_dot(qb[rs, hs], s_old.astype(BF16)))
            dec = jnp.exp(_lane_column(b[(s + 1) * steps - 1:(s + 1) * steps, hs]))
            sout_ref[s, hd] = dec * s_old + _dot_tn(kend[rs, hs], vb[rs, vs])
        outs.append(o_h + jnp.concatenate(o_state, axis=0))
    o_ref[...] = _gla_out(x, jnp.concatenate(outs, axis=-1), r, gn_ref, wo_ref)


def _gla_sample(x, state_gla, layer, gla_consts, prev, *, steps):
    n, d = x.shape
    seqs = GLA_SEQS
    rows = seqs * steps
    bs = n // steps
    assert bs % seqs == 0
    win, wa2, wo = gla_consts[1], gla_consts[2], gla_consts[5]
    dk = wa2.shape[1]
    dkh, dvh = dk // GLA_HEADS, wo.shape[0] // GLA_HEADS
    tri, masks, levels = _gla_tables(rows, steps, rows)
    consts = tuple(gla_consts) + (_Const(tri), _Const(masks))
    state_block = seqs * GLA_HEADS * dkh * dvh * 4
    resident = _nbytes(consts) + 4 * rows * d * 4 + (GLA_STATE_IN_BUFFERS + 2) * state_block
    resident += rows * (win.shape[1] * 4 + (levels + 4) * dk * 4)
    return _call_with_state_slot(
        functools.partial(_gla_sample_body, levels=levels, seqs=seqs, steps=steps, layer=layer),
        (x, state_gla) + _arrays(consts),
        [pl.BlockSpec((rows, d), lambda i: (i, 0)), pl.BlockSpec(memory_space=pl.ANY)] + _specs(consts),
        grid=(bs // seqs,),
        out_shapes=[jax.ShapeDtypeStruct((n, d), F32)],
        out_specs=[pl.BlockSpec((rows, d), lambda i: (i, 0))],
        state_shape=(bs, GLA_HEADS, dkh, dvh), state_block=(seqs, GLA_HEADS, dkh, dvh),
        state_index=lambda i: (i, 0, 0, 0),
        slot=layer, n_slots=state_gla.shape[0], prev=prev,
        scratch_shapes=[pltpu.VMEM((GLA_STATE_IN_BUFFERS, seqs, GLA_HEADS, dkh, dvh), F32),
                        pltpu.SemaphoreType.DMA((GLA_STATE_IN_BUFFERS,))],
        resident=resident, name="gla_sample")


def _ffn_tail_steps(s, n_total, z_ref, xs_ref, o_ref, os_ref, ffn_steps):
    @pl.when(s >= n_total)
    def _():
        ffn = {}
        _interleave(ffn_steps(ffn, jnp.where(s == n_total, z_ref[...], xs_ref[...])))

        @pl.when(s == n_total)
        def _():
            o_ref[...] = ffn["out"]

        @pl.when(s > n_total)
        def _():
            os_ref[...] = ffn["out"]


def _pool_ffn_body(x_ref, xs_ref, gm_ref, wp_ref, sc_ref, gfn_ref, gf_ref, w1f_ref, w3f_ref, w2f_ref,
                   o_ref, os_ref, ctx_ref, ext_ref, z_ref, w1_ref, w3_ref, w2_ref, st13_ref, st2_ref, sem13, sem2,
                   *, rows, n_tiles, n_total, final, ffn_layer):
    s = pl.program_id(0)
    d = x_ref.shape[-1]
    t = jnp.minimum(s, n_total - 1) % n_tiles

    def fetch_steps():
        chunks = _ffn_weight_chunks(ffn_layer, (w1f_ref, w3f_ref, w2f_ref), (w1_ref, w3_ref, w2_ref),
                                    (st13_ref, st13_ref, st2_ref), (sem13, sem13, sem2))
        return (_fetch_steps(chunks), len(chunks))

    def ffn_steps(ffn, src=None):
        return (_ffn_steps(z_ref[...] if src is None else src, gfn_ref, w1_ref, w3_ref, w2_ref, gf_ref, ffn,
                           final=final, col_chunks=FFN_COL_CHUNKS), FFN_STEP_COST * rows // 256)

    def mixer_step(with_ffn):
        @pl.when(t == 0)
        def _():
            ext_ref[0:POOL_MAXW, :] = jnp.zeros((POOL_MAXW, d), F32)

        x = x_ref[...]
        h = _rmsnorm(x, gm_ref[...])
        ext_ref[POOL_MAXW:POOL_MAXW + rows, :] = h
        pos = t * rows + lax.broadcasted_iota(jnp.int32, (rows, 1), 0)

        def ext_read(k, cs):
            return ext_ref[POOL_MAXW - k:POOL_MAXW - k + rows, cs]

        mix, ffn = {}, {}
        mixer = (_pool_group_steps(ext_read, h, pos, wp_ref, sc_ref[...], mix), POOL_STEP_COST * rows // 256)
        streams = [mixer, ffn_steps(ffn)] if with_ffn else [fetch_steps(), mixer]
        _interleave(*streams)
        z_ref[...] = x + mix["y"]
        if with_ffn:
            o_ref[...] = ffn["out"]
        ext_ref[0:POOL_MAXW, :] = ext_ref[rows:rows + POOL_MAXW, :]

        @pl.when(t == n_tiles - 1)
        def _():
            ctx_ref[...] = ext_ref[rows + 1:rows + POOL_MAXW, :]

    @pl.when(s == 0)
    def _():
        mixer_step(False)

    @pl.when(jnp.logical_and(s > 0, s < n_total))
    def _():
        mixer_step(True)

    _ffn_tail_steps(s, n_total, z_ref, xs_ref, o_ref, os_ref, ffn_steps)


def _gla_ffn_body(x_ref, xs_ref, gm_ref, win_ref, wa2_ref, ba_ref, gn_ref, wo_ref, tri_ref, masks_ref,
                  gfn_ref, gf_ref, w1f_ref, w3f_ref, w2f_ref,
                  o_ref, os_ref, sout_ref, s_ref, z_ref, w1_ref, w3_ref, w2_ref, st13_ref, st2_ref, sem13, sem2,
                  *, levels, chunk, n_tiles, n_total, final, ffn_layer):
    s = pl.program_id(0)
    t = jnp.minimum(s, n_total - 1) % n_tiles

    def fetch_steps():
        chunks = _ffn_weight_chunks(ffn_layer, (w1f_ref, w3f_ref, w2f_ref), (w1_ref, w3_ref, w2_ref),
                                    (st13_ref, st13_ref, st2_ref), (sem13, sem13, sem2))
        return (_fetch_steps(chunks), len(chunks))

    def ffn_steps(ffn, src=None):
        return (_ffn_steps(z_ref[...] if src is None else src, gfn_ref, w1_ref, w3_ref, w2_ref, gf_ref, ffn,
                           final=final, col_chunks=FFN_COL_CHUNKS), FFN_STEP_COST)

    def mixer_step(with_ffn):
        @pl.when(t == 0)
        def _():
            s_ref[...] = jnp.zeros(s_ref.shape, F32)

        states = [s_ref[hd] for hd in range(GLA_HEADS)]
        mix, ffn = {}, {}
        mixer = (_gla_prompt_steps(x_ref[...], states, gm_ref, win_ref, wa2_ref, ba_ref, gn_ref,
                                   wo_ref, tri_ref, masks_ref, mix, levels=levels, chunk=chunk), GLA_STEP_COST)
        streams = [mixer, ffn_steps(ffn)] if with_ffn else [fetch_steps(), mixer]
        _interleave(*streams)
        states = mix["states"]
        z_ref[...] = mix["z"]
        if with_ffn:
            o_ref[...] = ffn["out"]
        for hd in range(GLA_HEADS):
            s_ref[hd] = states[hd]

        @pl.when(t == n_tiles - 1)
        def _():
            for hd in range(GLA_HEADS):
                sout_ref[hd] = states[hd]

    @pl.when(s == 0)
    def _():
        mixer_step(False)

    @pl.when(jnp.logical_and(s > 0, s < n_total))
    def _():
        mixer_step(True)

    _ffn_tail_steps(s, n_total, z_ref, xs_ref, o_ref, os_ref, ffn_steps)


def _prompt_layer(kind, x, xs, mixer_consts, ffn_consts, *, batch, seq, final, slot, n_slots, prev):
    n, d = x.shape
    rows = PROMPT_ROWS[kind]
    n_tiles = seq // rows
    n_total = batch * n_tiles
    assert seq % rows == 0 and rows >= POOL_MAXW and rows % GLA_CHUNK == 0
    g_ffn, w1f, w3f, w2f, gf = ffn_consts
    dff = w1f.shape[1]
    ffn_vmem = (g_ffn, gf)
    stage13 = (FFN_STAGE_SLOTS, d // FFN_WEIGHT_CHUNKS, dff)
    stage2 = (FFN_STAGE_SLOTS, dff // FFN_WEIGHT_CHUNKS, d)
    assert d % FFN_WEIGHT_CHUNKS == 0 and (dff // FFN_WEIGHT_CHUNKS) % (2 * V7X_SUBLANES) == 0
    ffn_scratch = [pltpu.VMEM((d, dff), BF16), pltpu.VMEM((d, dff), BF16), pltpu.VMEM((dff, d), BF16),
                   pltpu.VMEM(stage13, F32), pltpu.VMEM(stage2, F32),
                   pltpu.SemaphoreType.DMA((FFN_STAGE_SLOTS,)), pltpu.SemaphoreType.DMA((FFN_STAGE_SLOTS,))]
    resident = _nbytes(mixer_consts) + 3 * d * dff * 2 + 4 * int(np.prod(stage13)) * 2
    resident += 6 * rows * d * 4 + 3 * rows * dff * 4
    ns = xs.shape[0]
    assert ns % rows == 0
    n_sample = ns // rows
    x_spec = pl.BlockSpec((rows, d), lambda s: (jnp.minimum(s, n_total - 1), 0))
    o_spec = pl.BlockSpec((rows, d), lambda s: (jnp.clip(s - 1, 0, n_total - 1), 0))
    xs_spec = pl.BlockSpec((rows, d), lambda s: (jnp.clip(s - n_total - 1, 0, n_sample - 1), 0))
    seq_of = lambda s: jnp.minimum(s, n_total - 1) // n_tiles
    carry = [pltpu.VMEM((rows, d), F32)]
    if kind == "pool":
        body = functools.partial(_pool_ffn_body, rows=rows, n_tiles=n_tiles, n_total=n_total, final=final,
                                 ffn_layer=w1f.layer)
        state_shape = (batch, POOL_BUF, d)
        state_block = (None, POOL_BUF, d)
        state_index = lambda s: (seq_of(s), 0, 0)
        scratch = [pltpu.VMEM((rows + POOL_MAXW, d), F32)] + carry
        resident += (rows + POOL_MAXW) * d * 4
        consts = tuple(mixer_consts) + ffn_vmem
    else:
        win, wa2, wo = mixer_consts[1], mixer_consts[2], mixer_consts[5]
        dk = wa2.shape[1]
        dkh, dvh = dk // GLA_HEADS, wo.shape[0] // GLA_HEADS
        tri, masks, levels = _gla_tables(rows, GLA_CHUNK, GLA_CHUNK)
        body = functools.partial(_gla_ffn_body, levels=levels, chunk=GLA_CHUNK, n_tiles=n_tiles,
                                 n_total=n_total, final=final, ffn_layer=w1f.layer)
        state_shape = (batch, GLA_HEADS, dkh, dvh)
        state_block = (None, GLA_HEADS, dkh, dvh)
        state_index = lambda s: (seq_of(s), 0, 0, 0)
        scratch = [pltpu.VMEM((GLA_HEADS, dkh, dvh), F32)] + carry
        tables = (_Const(tri), _Const(masks))
        resident += _nbytes(tables) + 3 * GLA_HEADS * dkh * dvh * 4
        resident += rows * (win.shape[1] * 4 + (levels + 4) * dk * 4)
        consts = tuple(mixer_consts) + tables + ffn_vmem
    hbm = pl.BlockSpec(memory_space=pl.ANY)
    return _call_with_state_slot(
        body, (x, xs) + _arrays(consts) + (w1f.array, w3f.array, w2f.array),
        [x_spec, xs_spec] + _specs(consts) + [hbm] * 3,
        grid=(n_total + 1 + n_sample,),
        out_shapes=[jax.ShapeDtypeStruct((n, d), F32), jax.ShapeDtypeStruct((ns, d), F32)],
        out_specs=[o_spec, xs_spec],
        state_shape=state_shape, state_block=state_block, state_index=state_index,
        slot=slot, n_slots=n_slots, prev=prev,
        scratch_shapes=scratch + ffn_scratch, resident=resident,
        name=kind + ("_ffn_final" if final else "_ffn") + "_prompt")


def kernel(x_prompt, x_sample, state_pool, state_gla, norm_mix, norm_ffn, norm_final, pool_w, pool_scale,
           gla_w_in, gla_w_a2, gla_b_a, gla_norm, gla_w_o, ffn_w1, ffn_w3, ffn_w2):
    batch, seq, d = x_prompt.shape
    bs, steps, _ = x_sample.shape
    depth = norm_mix.shape[0]

    xp = x_prompt.reshape(batch * seq, d)
    xs = x_sample.reshape(bs * steps, d)

    rows_of = lambda a: a.reshape(a.shape[0], 1, -1)
    w_pool, w_in, w_a2, w_o = (a.astype(BF16) for a in (pool_w, gla_w_in, gla_w_a2, gla_w_o))
    g_mix, g_ffn, sc_pool, b_a, g_gla = (rows_of(a) for a in (norm_mix, norm_ffn, pool_scale, gla_b_a, gla_norm))
    gf = _Const(norm_final.reshape(1, -1))

    n_pool, n_gla = state_pool.shape[0], state_gla.shape[0]
    state_pool_t = jnp.swapaxes(state_pool, 1, 2)
    new_pool_p = new_pool_s = new_gla_p = new_gla_s = None
    for i in range(depth):
        j = i // 2
        final = i == depth - 1
        g = _Const(g_mix, i)
        ffn_consts = (_Const(g_ffn, i), _Const(ffn_w1, i), _Const(ffn_w3, i), _Const(ffn_w2, i), gf)
        if i % 2 == 0:
            pool_consts = (g, _Const(w_pool, j), _Const(sc_pool, j))
            xs3, new_pool_s = _pool_sample(xs.reshape(bs, steps, d), state_pool_t, j, pool_consts, new_pool_s)
            xp, xs, new_pool_p = _prompt_layer("pool", xp, xs3.reshape(bs * steps, d), pool_consts, ffn_consts,
                                               batch=batch, seq=seq, final=final, slot=j, n_slots=n_pool,
                                               prev=new_pool_p)
        else:
            gla_consts = (g, _Const(w_in, j), _Const(w_a2, j), _Const(b_a, j), _Const(g_gla, j), _Const(w_o, j))
            xs, new_gla_s = _gla_sample(xs, state_gla, j, gla_consts, new_gla_s, steps=steps)
            xp, xs, new_gla_p = _prompt_layer("gla", xp, xs, gla_consts, ffn_consts, batch=batch, seq=seq,
                                              final=final, slot=j, n_slots=n_gla, prev=new_gla_p)

    return (xp.reshape(batch, seq, d), xs.reshape(bs, steps, d),
            new_pool_p, new_gla_p, jnp.swapaxes(new_pool_s, 1, 2), new_gla_s)
```
